```python
import math
import jax, jax.numpy as jnp
from jax import lax
import numpy as np

D_MODEL = 2048
BATCH = 4
SEQ = 4096
DEPTH = 1

A_HEADS = 8
A_HEAD_DIM = 64
A_V_DIM = 2 * A_HEAD_DIM
A_WIDTH = A_HEADS * A_V_DIM
B_HEADS = 8
B_Q_LORA = 512
B_KV_LORA = 512
B_NOPE = 128
B_ROPE = 64
B_V_DIM = 128
B_WIDTH = B_HEADS * B_V_DIM
N_BRANCH = 2

ROPE_THETA = 10000.0
NORM_EPS = 1e-6
SUBLN_EPS = 1e-5
Q_BLOCK = 128

A_Q_COLS = A_HEADS * 2 * A_HEAD_DIM
A_K_COLS = A_HEADS * 2 * A_HEAD_DIM
A_V_COLS = A_WIDTH
A_G_COLS = A_WIDTH
B_CQ_COLS = B_Q_LORA
B_CKV_COLS = B_KV_LORA
B_KR_COLS = B_ROPE
B_G_COLS = B_WIDTH
MERGE_COLS = N_BRANCH * D_MODEL
_SIZES = [A_Q_COLS, A_K_COLS, A_V_COLS, A_G_COLS, B_CQ_COLS, B_CKV_COLS, B_KR_COLS, B_G_COLS, MERGE_COLS]
IN_COLS = sum(_SIZES)
SPLIT_POINTS = [int(v) for v in np.cumsum(_SIZES)[:-1]]

kernel_name = "hybrid_diffattn_mla_gated_encoder"


def _rmsnorm(x, g, eps=NORM_EPS):
    xf = x.astype(jnp.float32)
    y = xf * lax.rsqrt(jnp.mean(xf * xf, axis=-1, keepdims=True) + eps)
    return y.astype(x.dtype) * g


def _lambda_init(layer_idx):
    return 0.8 - 0.6 * math.exp(-0.3 * layer_idx)


def _alibi_slopes(n_heads):
    return jnp.asarray([2.0 ** (-8.0 * (h + 1) / n_heads) for h in range(n_heads)], dtype=jnp.float32)


def _rope_cos_sin(pos):
    half = B_ROPE // 2
    inv = ROPE_THETA ** (-jnp.arange(half, dtype=jnp.float32) / half)
    ang = pos.astype(jnp.float32)[..., None] * inv
    return jnp.cos(ang), jnp.sin(ang)


def _apply_rope(t, cos, sin):
    half = t.shape[-1] // 2
    t1 = t[..., :half].astype(jnp.float32)
    t2 = t[..., half:].astype(jnp.float32)
    out = jnp.concatenate([t1 * cos - t2 * sin, t1 * sin + t2 * cos], axis=-1)
    return out.astype(t.dtype)


def _to_blocks(t):
    b, s = t.shape[0], t.shape[1]
    nb = s // Q_BLOCK
    return t.reshape(b, nb, Q_BLOCK, *t.shape[2:]).swapaxes(0, 1)


def _from_blocks(t):
    nb, b = t.shape[0], t.shape[1]
    return t.swapaxes(0, 1).reshape(b, nb * Q_BLOCK, *t.shape[3:])


def _diff_attention(q1, q2, k1, k2, v, pos_f, lam, slopes):
    scale = A_HEAD_DIM ** -0.5

    def one_block(args):
        qb1, qb2, pb = args
        dist = jnp.abs(pb[:, :, None] - pos_f[:, None, :])
        bias = -slopes[None, :, None, None] * dist[:, None]
        s1 = jnp.einsum('bqhd,bkhd->bhqk', qb1, k1).astype(jnp.float32) * scale + bias
        s2 = jnp.einsum('bqhd,bkhd->bhqk', qb2, k2).astype(jnp.float32) * scale + bias
        p = jax.nn.softmax(s1, axis=-1) - lam * jax.nn.softmax(s2, axis=-1)
        return jnp.einsum('bhqk,bkhe->bqhe', p.astype(v.dtype), v)

    out = lax.map(one_block, (_to_blocks(q1), _to_blocks(q2), _to_blocks(pos_f)))
    return _from_blocks(out)


def _mla_attention(q_nope, q_rope, k_nope, k_rope, v):
    scale = (B_NOPE + B_ROPE) ** -0.5

    def one_block(args):
        qn, qr = args
        s = (jnp.einsum('bqhd,bkhd->bhqk', qn, k_nope)
             + jnp.einsum('bqhr,bkr->bhqk', qr, k_rope)).astype(jnp.float32) * scale
        p = jax.nn.softmax(s, axis=-1)
        return jnp.einsum('bhqk,bkhe->bqhe', p.astype(v.dtype), v)

    out = lax.map(one_block, (_to_blocks(q_nope), _to_blocks(q_rope)))
    return _from_blocks(out)


def setup_inputs(seed: int = 0) -> dict:
    key = jax.random.key(seed)
    ks = jax.random.split(key, 20)
    f32 = jnp.float32
    nrm = lambda k, shape, fan_in: jax.random.normal(k, shape, f32) * (fan_in ** -0.5)
    gain = lambda k, shape: 1.0 + 0.02 * jax.random.normal(k, shape, f32)
    x = jax.random.normal(ks[0], (BATCH, SEQ, D_MODEL), f32)
    offsets = jax.random.randint(ks[1], (BATCH, 1), 0, 1024, dtype=jnp.int32)
    positions = (jnp.arange(SEQ, dtype=jnp.int32)[None, :] + offsets).astype(jnp.int32)
    return {
        "x": x,
        "positions": positions,
        "norm_g": gain(ks[2], (DEPTH, D_MODEL)),
        "w_in": nrm(ks[3], (DEPTH, D_MODEL, IN_COLS), D_MODEL),
        "lam_q1": 0.1 * jax.random.normal(ks[4], (DEPTH, A_HEAD_DIM), f32),
        "lam_k1": 0.1 * jax.random.normal(ks[5], (DEPTH, A_HEAD_DIM), f32),
        "lam_q2": 0.1 * jax.random.normal(ks[6], (DEPTH, A_HEAD_DIM), f32),
        "lam_k2": 0.1 * jax.random.normal(ks[7], (DEPTH, A_HEAD_DIM), f32),
        "a_subln_g": gain(ks[8], (DEPTH, A_V_DIM)),
        "w_oa": nrm(ks[9], (DEPTH, A_WIDTH, D_MODEL), A_WIDTH),
        "q_norm_g": gain(ks[10], (DEPTH, B_Q_LORA)),
        "w_uq": nrm(ks[11], (DEPTH, B_Q_LORA, B_HEADS * (B_NOPE + B_ROPE)), B_Q_LORA),
        "kv_norm_g": gain(ks[12], (DEPTH, B_KV_LORA)),
        "w_ukv": nrm(ks[13], (DEPTH, B_KV_LORA, B_HEADS * (B_NOPE + B_V_DIM)), B_KV_LORA),
        "w_ob": nrm(ks[14], (DEPTH, B_WIDTH, D_MODEL), B_WIDTH),
        "w_out": nrm(ks[15], (DEPTH, D_MODEL, D_MODEL), D_MODEL),
        "final_g": gain(ks[16], (D_MODEL,)),
    }


def reference(x, positions, norm_g, w_in, lam_q1, lam_k1, lam_q2, lam_k2, a_subln_g, w_oa,
              q_norm_g, w_uq, kv_norm_g, w_ukv, w_ob, w_out, final_g):
    b, s, _ = x.shape
    pos_f = positions.astype(jnp.float32)
    slopes = _alibi_slopes(A_HEADS)
    cos, sin = _rope_cos_sin(positions)
    for layer in range(DEPTH):
        lam_init = _lambda_init(layer)
        h = _rmsnorm(x, norm_g[layer])
        proj = h @ w_in[layer]
        aq, ak, av, ag, cq, ckv, kr, bg, mg = jnp.split(proj, SPLIT_POINTS, axis=-1)

        aq = aq.reshape(b, s, A_HEADS, 2, A_HEAD_DIM)
        ak = ak.reshape(b, s, A_HEADS, 2, A_HEAD_DIM)
        av = av.reshape(b, s, A_HEADS, A_V_DIM)
        lam = (jnp.exp(jnp.sum(lam_q1[layer].astype(jnp.float32) * lam_k1[layer].astype(jnp.float32)))
               - jnp.exp(jnp.sum(lam_q2[layer].astype(jnp.float32) * lam_k2[layer].astype(jnp.float32)))
               + lam_init)
        oa = _diff_attention(aq[..., 0, :], aq[..., 1, :], ak[..., 0, :], ak[..., 1, :], av,
                             pos_f, lam, slopes)
        oa = _rmsnorm(oa, a_subln_g[layer], SUBLN_EPS) * (1.0 - lam_init)
        ya = (oa.reshape(b, s, A_WIDTH) * jax.nn.silu(ag)) @ w_oa[layer]

        q = (_rmsnorm(cq, q_norm_g[layer]) @ w_uq[layer]).reshape(b, s, B_HEADS, B_NOPE + B_ROPE)
        kv = (_rmsnorm(ckv, kv_norm_g[layer]) @ w_ukv[layer]).reshape(b, s, B_HEADS, B_NOPE + B_V_DIM)
        q_nope, q_rope = q[..., :B_NOPE], q[..., B_NOPE:]
        k_nope, vb = kv[..., :B_NOPE], kv[..., B_NOPE:]
        q_rope = _apply_rope(q_rope, cos[:, :, None, :], sin[:, :, None, :])
        k_rope = _apply_rope(kr, cos, sin)
        ob = _mla_attention(q_nope, q_rope, k_nope, k_rope, vb)
        yb = (ob.reshape(b, s, B_WIDTH) * jax.nn.silu(bg)) @ w_ob[layer]

        ga, gb = jnp.split(mg, 2, axis=-1)
        merged = jax.nn.sigmoid(ga) * ya + jax.nn.sigmoid(gb) * yb
        x = x + merged @ w_out[layer]
    return _rmsnorm(x, final_g)
```

```python
import functools
import math

import numpy as np
import jax
import jax.numpy as jnp
from jax import lax
from jax.experimental import pallas as pl
from jax.experimental.pallas import tpu as pltpu

F32 = jnp.float32
BF16 = jnp.bfloat16

D_MODEL = 2048
A_HEADS = 8
A_HEAD_DIM = 64
A_V_DIM = 128
A_WIDTH = 1024
B_HEADS = 8
B_Q_LORA = 512
B_KV_LORA = 512
B_NOPE = 128
B_ROPE = 64
B_V_DIM = 128
B_WIDTH = 1024
ROPE_THETA = 10000.0
NORM_EPS = 1e-6
SUBLN_EPS = 1e-5
LAM_INIT = 0.8 - 0.6 * math.exp(-0.3 * 0)

_SIZES = [1024, 1024, 1024, 1024, 512, 512, 64, 1024, 4096]
_SPLITS = [int(v) for v in np.cumsum(_SIZES)[:-1]]

LANE = 128
COL_AQ, COL_AK, COL_AV, COL_AG = 0, 8, 16, 24
COL_GA, COL_GB, COL_BG = 32, 48, 64
COL_CQ, COL_CKV, COL_KR = 72, 76, 80
PROJ_COLS = 10752

VMEM_LIMIT = 56 * 1024 * 1024


def _dot_nt(a, b):
    return lax.dot_general(a, b, (((1,), (1,)), ((), ())), preferred_element_type=F32)


def _inproj_kernel(x_ref, g_ref, w_ref, o_ref, h_ref, *, tm):
    @pl.when(pl.program_id(1) == 0)
    def _():
        def body(r, c):
            rows = pl.ds(pl.multiple_of(r * 128, 128), 128)
            x = x_ref[rows, :]
            ms = jnp.mean(x * x, axis=-1, keepdims=True)
            h_ref[rows, :] = (x * lax.rsqrt(ms + NORM_EPS) * g_ref[...]).astype(BF16)
            return c
        lax.fori_loop(0, tm // 128, body, 0)

    o_ref[...] = jnp.dot(h_ref[...], w_ref[...], preferred_element_type=F32).astype(BF16)


def _inproj(x2, g, w, *, tm=1024, tn=512):
    m = x2.shape[0]
    return pl.pallas_call(
        functools.partial(_inproj_kernel, tm=tm),
        grid=(m // tm, PROJ_COLS // tn),
        in_specs=[
            pl.BlockSpec((tm, D_MODEL), lambda i, j: (i, 0)),
            pl.BlockSpec((1, D_MODEL), lambda i, j: (0, 0)),
            pl.BlockSpec((D_MODEL, tn), lambda i, j: (0, j)),
        ],
        out_specs=pl.BlockSpec((tm, tn), lambda i, j: (i, j)),
        out_shape=jax.ShapeDtypeStruct((m, PROJ_COLS), BF16),
        scratch_shapes=[pltpu.VMEM((tm, D_MODEL), BF16)],
        compiler_params=pltpu.CompilerParams(
            dimension_semantics=("arbitrary", "arbitrary"), vmem_limit_bytes=VMEM_LIMIT),
        name="inproj",
    )(x2, g, w)


def _softmax_update(s, v, m_ref, l_ref, acc_ref):
    m_old = m_ref[...]
    m_new = jnp.maximum(m_old, jnp.max(s, axis=-1, keepdims=True))
    p = jnp.exp(s - m_new)
    alpha = jnp.exp(m_old - m_new)
    l_ref[...] = alpha * l_ref[...] + jnp.sum(p, axis=-1, keepdims=True)
    acc_ref[...] = alpha * acc_ref[...] + jnp.dot(p.astype(BF16), v, preferred_element_type=F32)
    m_ref[...] = m_new


M_INIT = -1e30


def _attn_a_kernel(mode_ref, slope_ref, q_ref, k_ref, v_ref, ag_ref, qb_ref, kb_ref, posq_ref,
                   posk_ref, lam_ref, subg_ref, o_ref, qa1, qa2, m1, m2, l1, l2, acc1, acc2,
                   *, tq, tk, nq, nk):
    b = pl.program_id(0)
    h = pl.program_id(1)
    qi = pl.program_id(2)
    slope = slope_ref[h]
    lo = lax.broadcasted_iota(jnp.int32, (1, LANE), 1) < A_HEAD_DIM

    qs = (q_ref[...].astype(F32) * (A_HEAD_DIM ** -0.5)).astype(BF16)
    qbias = (qb_ref[...].astype(F32) * slope).astype(BF16)
    qa1[...] = jnp.where(lo, qs, qbias)
    qa2[...] = jnp.where(lo, qbias, qs)
    for m_ref in (m1, m2):
        m_ref[...] = jnp.full(m_ref.shape, M_INIT, F32)
    for z_ref in (l1, l2, acc1, acc2):
        z_ref[...] = jnp.zeros(z_ref.shape, F32)

    def step(ki, carry):
        mode = mode_ref[(b * nq + qi) * nk + ki]
        ks = pl.ds(pl.multiple_of(ki * tk, tk), tk)
        k = k_ref[ks, :]
        v = v_ref[ks, :]
        kbias = (kb_ref[ks, :].astype(F32) * mode.astype(F32)).astype(BF16)
        ka1 = jnp.where(lo, k, kbias)
        ka2 = jnp.where(lo, kbias, k)
        s1 = _dot_nt(qa1[...], ka1)
        s2 = _dot_nt(qa2[...], ka2)

        @pl.when(mode == 0)
        def _():
            dist = jnp.abs(posq_ref[...] - posk_ref[pl.ds(ki, 1), :])
            bias = -slope * dist
            _softmax_update(s1 + bias, v, m1, l1, acc1)
            _softmax_update(s2 + bias, v, m2, l2, acc2)

        @pl.when(mode != 0)
        def _():
            _softmax_update(s1, v, m1, l1, acc1)
            _softmax_update(s2, v, m2, l2, acc2)

        return carry

    lax.fori_loop(0, nk, step, 0)

    lamv = lam_ref[...]
    lam = (jnp.exp(jnp.sum(lamv[0:1] * lamv[1:2], keepdims=True))
           - jnp.exp(jnp.sum(lamv[2:3] * lamv[3:4], keepdims=True)) + LAM_INIT)
    o = acc1[...] / l1[...] - lam * (acc2[...] / l2[...])
    ms = jnp.mean(o * o, axis=-1, keepdims=True)
    on = o * lax.rsqrt(ms + SUBLN_EPS) * subg_ref[...] * (1.0 - LAM_INIT)
    ag = ag_ref[...].astype(F32)
    o_ref[...] = (on * (ag * jax.nn.sigmoid(ag))).astype(BF16)


def _attn_a(mode, slopes, proj3, qbias, kbias, posq, posk, lamv, subg, *, tq, tk):
    bsz, s, _ = proj3.shape
    nq, nk = s // tq, s // tk
    kern = functools.partial(_attn_a_kernel, tq=tq, tk=tk, nq=nq, nk=nk)
    grid_spec = pltpu.PrefetchScalarGridSpec(
        num_scalar_prefetch=2,
        grid=(bsz, A_HEADS, nq),
        in_specs=[
            pl.BlockSpec((None, tq, LANE), lambda b, h, i, *_: (b, i, COL_AQ + h)),
            pl.BlockSpec((None, s, LANE), lambda b, h, i, *_: (b, 0, COL_AK + h)),
            pl.BlockSpec((None, s, LANE), lambda b, h, i, *_: (b, 0, COL_AV + h)),
            pl.BlockSpec((None, tq, LANE), lambda b, h, i, *_: (b, i, COL_AG + h)),
            pl.BlockSpec((None, tq, LANE), lambda b, h, i, *_: (b, i, 0)),
            pl.BlockSpec((None, s, LANE), lambda b, h, i, *_: (b, 0, 0)),
            pl.BlockSpec((None, tq, 1), lambda b, h, i, *_: (b, i, 0)),
            pl.BlockSpec((None, nk, tk), lambda b, h, i, *_: (b, 0, 0)),
            pl.BlockSpec((4, A_HEAD_DIM), lambda b, h, i, *_: (0, 0)),
            pl.BlockSpec((1, A_V_DIM), lambda b, h, i, *_: (0, 0)),
        ],
        out_specs=pl.BlockSpec((None, tq, LANE), lambda b, h, i, *_: (b, i, h)),
        scratch_shapes=[
            pltpu.VMEM((tq, LANE), BF16), pltpu.VMEM((tq, LANE), BF16),
            pltpu.VMEM((tq, 1), F32), pltpu.VMEM((tq, 1), F32),
            pltpu.VMEM((tq, 1), F32), pltpu.VMEM((tq, 1), F32),
            pltpu.VMEM((tq, A_V_DIM), F32), pltpu.VMEM((tq, A_V_DIM), F32),
        ],
    )
    return pl.pallas_call(
        kern,
        grid_spec=grid_spec,
        out_shape=jax.ShapeDtypeStruct((bsz, s, A_WIDTH), BF16),
        compiler_params=pltpu.CompilerParams(
            dimension_semantics=("arbitrary", "arbitrary", "arbitrary"),
            vmem_limit_bytes=VMEM_LIMIT),
        name="attn_a",
    )(mode, slopes, proj3, proj3, proj3, proj3, qbias, kbias, posq, posk, lamv, subg)


def _mla_prep_kernel(cq_ref, ckv_ref, kr_ref, pos_ref, inv_ref, qg_ref, kvg_ref, wuq_ref, wukv_ref,
                     qo_ref, ko_ref, vo_ref):
    def norm(ref, g_ref):
        t = ref[...].astype(F32)
        ms = jnp.mean(t * t, axis=-1, keepdims=True)
        return (t * lax.rsqrt(ms + NORM_EPS) * g_ref[...]).astype(BF16)

    qf = jnp.dot(norm(cq_ref, qg_ref), wuq_ref[...], preferred_element_type=F32)
    kvf = jnp.dot(norm(ckv_ref, kvg_ref), wukv_ref[...], preferred_element_type=F32)

    lane = lax.broadcasted_iota(jnp.int32, (1, LANE), 1)
    half = B_ROPE // 2
    ang = pos_ref[...] * inv_ref[...]
    cs = jnp.cos(ang)
    sn = jnp.sin(ang)
    c_t = jnp.where(lane < B_ROPE, cs, 0.0)
    s1_t = jnp.where(lane < half, -sn, 0.0)
    s2_t = jnp.where((lane >= half) & (lane < B_ROPE), sn, 0.0)

    def rope(t):
        return (t * c_t + pltpu.roll(t, LANE - half, axis=1) * s1_t
                + pltpu.roll(t, half, axis=1) * s2_t)

    scale = (B_NOPE + B_ROPE) ** -0.5
    krope = rope(kr_ref[...].astype(F32)).astype(BF16)
    for h in range(B_HEADS):
        c0 = h * 2 * LANE
        qo_ref[:, c0:c0 + LANE] = (qf[:, c0:c0 + LANE] * scale).astype(BF16)
        qo_ref[:, c0 + LANE:c0 + 2 * LANE] = (rope(qf[:, c0 + LANE:c0 + 2 * LANE]) * scale).astype(BF16)
        ko_ref[:, c0:c0 + LANE] = kvf[:, c0:c0 + LANE].astype(BF16)
        ko_ref[:, c0 + LANE:c0 + 2 * LANE] = krope
        vo_ref[:, h * LANE:(h + 1) * LANE] = kvf[:, c0 + LANE:c0 + 2 * LANE].astype(BF16)


def _mla_prep(proj2, pos_col, inv_lane, qg, kvg, wuq, wukv, *, tm=512):
    m = proj2.shape[0]
    wq = B_HEADS * 2 * LANE
    return pl.pallas_call(
        _mla_prep_kernel,
        grid=(m // tm,),
        in_specs=[
            pl.BlockSpec((tm, B_Q_LORA), lambda i: (i, COL_CQ // 4)),
            pl.BlockSpec((tm, B_KV_LORA), lambda i: (i, COL_CKV // 4)),
            pl.BlockSpec((tm, LANE), lambda i: (i, COL_KR)),
            pl.BlockSpec((tm, 1), lambda i: (i, 0)),
            pl.BlockSpec((1, LANE), lambda i: (0, 0)),
            pl.BlockSpec((1, B_Q_LORA), lambda i: (0, 0)),
            pl.BlockSpec((1, B_KV_LORA), lambda i: (0, 0)),
            pl.BlockSpec((B_Q_LORA, wq), lambda i: (0, 0)),
            pl.BlockSpec((B_KV_LORA, wq), lambda i: (0, 0)),
        ],
        out_specs=[
            pl.BlockSpec((tm, wq), lambda i: (i, 0)),
            pl.BlockSpec((tm, wq), lambda i: (i, 0)),
            pl.BlockSpec((tm, B_WIDTH), lambda i: (i, 0)),
        ],
        out_shape=[
            jax.ShapeDtypeStruct((m, wq), BF16),
            jax.ShapeDtypeStruct((m, wq), BF16),
            jax.ShapeDtypeStruct((m, B_WIDTH), BF16),
        ],
        compiler_params=pltpu.CompilerParams(
            dimension_semantics=("arbitrary",), vmem_limit_bytes=VMEM_LIMIT),
        name="mla_prep",
    )(proj2, proj2, proj2, pos_col, inv_lane, qg, kvg, wuq, wukv)


def _attn_b_kernel(q_ref, k_ref, v_ref, bg_ref, o_ref, m1, l1, acc1, *, tk, nk):
    m1[...] = jnp.full(m1.shape, M_INIT, F32)
    l1[...] = jnp.zeros(l1.shape, F32)
    acc1[...] = jnp.zeros(acc1.shape, F32)

    def step(ki, carry):
        ks = pl.ds(pl.multiple_of(ki * tk, tk), tk)
        s = _dot_nt(q_ref[...], k_ref[ks, :])
        _softmax_update(s, v_ref[ks, :], m1, l1, acc1)
        return carry

    lax.fori_loop(0, nk, step, 0)
    bg = bg_ref[...].astype(F32)
    o_ref[...] = ((acc1[...] / l1[...]) * (bg * jax.nn.sigmoid(bg))).astype(BF16)


def _attn_b(qb3, kb3, vb3, proj3, *, tq, tk):
    bsz, s, _ = qb3.shape
    nq, nk = s // tq, s // tk
    return pl.pallas_call(
        functools.partial(_attn_b_kernel, tk=tk, nk=nk),
        grid=(bsz, B_HEADS, nq),
        in_specs=[
            pl.BlockSpec((None, tq, 2 * LANE), lambda b, h, i: (b, i, h)),
            pl.BlockSpec((None, s, 2 * LANE), lambda b, h, i: (b, 0, h)),
            pl.BlockSpec((None, s, LANE), lambda b, h, i: (b, 0, h)),
            pl.BlockSpec((None, tq, LANE), lambda b, h, i: (b, i, COL_BG + h)),
        ],
        out_specs=pl.BlockSpec((None, tq, LANE), lambda b, h, i: (b, i, h)),
        out_shape=jax.ShapeDtypeStruct((bsz, s, B_WIDTH), BF16),
        scratch_shapes=[
            pltpu.VMEM((tq, 1), F32), pltpu.VMEM((tq, 1), F32), pltpu.VMEM((tq, B_V_DIM), F32),
        ],
        compiler_params=pltpu.CompilerParams(
            dimension_semantics=("arbitrary", "arbitrary", "arbitrary"),
            vmem_limit_bytes=VMEM_LIMIT),
        name="attn_b",
    )(qb3, kb3, vb3, proj3)


def _post_kernel(za_ref, zb_ref, ga_ref, gb_ref, x_ref, woa_ref, wob_ref, wout_ref, fg_ref, o_ref):
    ya = jnp.dot(za_ref[...], woa_ref[...], preferred_element_type=F32)
    yb = jnp.dot(zb_ref[...], wob_ref[...], preferred_element_type=F32)
    merged = (jax.nn.sigmoid(ga_ref[...].astype(F32)) * ya
              + jax.nn.sigmoid(gb_ref[...].astype(F32)) * yb)
    y = x_ref[...] + jnp.dot(merged.astype(BF16), wout_ref[...], preferred_element_type=F32)
    ms = jnp.mean(y * y, axis=-1, keepdims=True)
    o_ref[...] = y * lax.rsqrt(ms + NORM_EPS) * fg_ref[...]


def _post(za, zb, proj2, x2, woa, wob, wout, fg, *, tm=256):
    m = x2.shape[0]
    const = lambda shape: pl.BlockSpec(shape, lambda i: (0, 0), pipeline_mode=pl.Buffered(1))
    return pl.pallas_call(
        _post_kernel,
        grid=(m // tm,),
        in_specs=[
            pl.BlockSpec((tm, A_WIDTH), lambda i: (i, 0)),
            pl.BlockSpec((tm, B_WIDTH), lambda i: (i, 0)),
            pl.BlockSpec((tm, D_MODEL), lambda i: (i, COL_GA // 16)),
            pl.BlockSpec((tm, D_MODEL), lambda i: (i, COL_GB // 16)),
            pl.BlockSpec((tm, D_MODEL), lambda i: (i, 0)),
            const((A_WIDTH, D_MODEL)),
            const((B_WIDTH, D_MODEL)),
            const((D_MODEL, D_MODEL)),
            const((1, D_MODEL)),
        ],
        out_specs=pl.BlockSpec((tm, D_MODEL), lambda i: (i, 0)),
        out_shape=jax.ShapeDtypeStruct((m, D_MODEL), F32),
        compiler_params=pltpu.CompilerParams(
            dimension_semantics=("arbitrary",), vmem_limit_bytes=VMEM_LIMIT),
        name="post",
    )(za, zb, proj2, proj2, x2, woa, wob, wout, fg)


def _prep_w_in(w):
    aq, ak, av, ag, cq, ckv, kr, bg, mg = jnp.split(w, _SPLITS, axis=1)
    ga, gb = mg[:, :D_MODEL], mg[:, D_MODEL:]
    pad = jnp.zeros((w.shape[0], PROJ_COLS - w.shape[1]), w.dtype)
    return jnp.concatenate([aq, ak, av, ag, ga, gb, bg, cq, ckv, kr, pad], axis=1).astype(BF16)


def _prep_w_uq(w):
    w = w.reshape(B_Q_LORA, B_HEADS, B_NOPE + B_ROPE)
    w = jnp.pad(w, ((0, 0), (0, 0), (0, 2 * LANE - (B_NOPE + B_ROPE))))
    return w.reshape(B_Q_LORA, B_HEADS * 2 * LANE).astype(BF16)


def _alibi_operands(positions, tq, tk):
    bsz, s = positions.shape
    pmin = jnp.min(positions, axis=1, keepdims=True)
    rel = positions - pmin
    ok = ((jnp.min(rel, axis=1) >= 0) & (jnp.max(rel, axis=1) < 65536)
          & (jnp.max(jnp.abs(positions), axis=1) < (1 << 24)))
    qmin = jnp.min(positions.reshape(bsz, s // tq, tq), axis=-1)
    qmax = jnp.max(positions.reshape(bsz, s // tq, tq), axis=-1)
    kmin = jnp.min(positions.reshape(bsz, s // tk, tk), axis=-1)
    kmax = jnp.max(positions.reshape(bsz, s // tk, tk), axis=-1)
    after = qmin[:, :, None] >= kmax[:, None, :]
    before = qmax[:, :, None] <= kmin[:, None, :]
    mode = jnp.where(after, 1, jnp.where(before, -1, 0)) * ok[:, None, None].astype(jnp.int32)
    rel = jnp.where(ok[:, None], rel, 0)
    hi = (rel >> 8).astype(F32)
    lo = (rel & 255).astype(F32)
    one = jnp.ones_like(hi)
    qc = jnp.stack([-256.0 * hi, -lo, 256.0 * one, one], axis=-1)
    kc = jnp.stack([one, one, hi, lo], axis=-1)

    def place(c):
        z = jnp.zeros((bsz, s, A_HEAD_DIM - 4), F32)
        return jnp.concatenate([c, z, c, z], axis=-1).astype(BF16)

    return mode.reshape(-1).astype(jnp.int32), place(qc), place(kc)


def kernel(x, positions, norm_g, w_in, lam_q1, lam_k1, lam_q2, lam_k2, a_subln_g, w_oa, q_norm_g,
           w_uq, kv_norm_g, w_ukv, w_ob, w_out, final_g):
    bsz, s, d = x.shape
    m = bsz * s
    tq = tk = 512
    layer = 0
    x2 = x.reshape(m, d)
    pos_f = positions.astype(F32)

    proj2 = _inproj(x2, norm_g[layer][None, :], _prep_w_in(w_in[layer]))
    proj3 = proj2.reshape(bsz, s, PROJ_COLS)

    mode, qbias, kbias = _alibi_operands(positions, tq, tk)
    slopes = jnp.asarray([2.0 ** (-8.0 * (h + 1) / A_HEADS) for h in range(A_HEADS)], dtype=F32)
    lamv = jnp.stack([lam_q1[layer], lam_k1[layer], lam_q2[layer], lam_k2[layer]]).astype(F32)
    za = _attn_a(mode, slopes, proj3, qbias, kbias, pos_f[:, :, None],
                 pos_f.reshape(bsz, s // tk, tk), lamv, a_subln_g[layer][None, :], tq=tq, tk=tk)

    half = B_ROPE // 2
    inv = ROPE_THETA ** (-jnp.arange(half, dtype=F32) / half)
    inv_lane = jnp.concatenate([inv, inv, jnp.zeros((LANE - B_ROPE,), F32)])[None, :]
    qb2, kb2, vb2 = _mla_prep(proj2, pos_f.reshape(m, 1), inv_lane, q_norm_g[layer][None, :],
                              kv_norm_g[layer][None, :], _prep_w_uq(w_uq[layer]),
                              w_ukv[layer].astype(BF16))
    zb = _attn_b(qb2.reshape(bsz, s, -1), kb2.reshape(bsz, s, -1), vb2.reshape(bsz, s, -1), proj3,
                 tq=tq, tk=tk)

    out = _post(za.reshape(m, A_WIDTH), zb.reshape(m, B_WIDTH), proj2, x2, w_oa[layer].astype(BF16),
                w_ob[layer].astype(BF16), w_out[layer].astype(BF16), final_g[None, :])
    return out.reshape(bsz, s, d)
```

```python
import functools
import math

import numpy as np
import jax
import jax.numpy as jnp
from jax import lax
from jax.experimental import pallas as pl
from jax.experimental.pallas import tpu as pltpu

F32 = jnp.float32
BF16 = jnp.bfloat16

D_MODEL = 2048
A_HEADS = 8
A_HEAD_DIM = 64
A_V_DIM = 128
A_WIDTH = 1024
B_HEADS = 8
B_Q_LORA = 512
B_KV_LORA = 512
B_NOPE = 128
B_ROPE = 64
B_V_DIM = 128
B_WIDTH = 1024
ROPE_THETA = 10000.0
NORM_EPS = 1e-6
SUBLN_EPS = 1e-5
LAM_INIT = 0.8 - 0.6 * math.exp(-0.3 * 0)
LOG2E = math.log2(math.e)

_SIZES = [1024, 1024, 1024, 1024, 512, 512, 64, 1024, 4096]
_SPLITS = [int(v) for v in np.cumsum(_SIZES)[:-1]]

LANE = 128
BF16_ROWS = 16
COL_AQ, COL_AK, COL_AV, COL_AG = 0, 8, 16, 24
COL_GA, COL_GB, COL_BG = 32, 48, 64
COL_CQ, COL_CKV, COL_KR = 72, 76, 80
PROJ_COLS = 10752

TQ = 512
TK = 512
V_ROWS = A_V_DIM + BF16_ROWS
N_BIAS = 12
VMEM_LIMIT = 56 * 1024 * 1024
M_INIT = -1e30


def _dot_nt(a, b):
    return lax.dot_general(a, b, (((1,), (1,)), ((), ())), preferred_element_type=F32)


def _inproj_kernel(x_ref, g_ref, w_ref, o_ref, h_ref, *, tm):
    @pl.when(pl.program_id(1) == 0)
    def _():
        def body(r, c):
            rows = pl.ds(pl.multiple_of(r * 128, 128), 128)
            x = x_ref[rows, :]
            ms = jnp.mean(x * x, axis=-1, keepdims=True)
            h_ref[rows, :] = (x * lax.rsqrt(ms + NORM_EPS) * g_ref[...]).astype(BF16)
            return c
        lax.fori_loop(0, tm // 128, body, 0)

    o_ref[...] = jnp.dot(h_ref[...], w_ref[...], preferred_element_type=F32).astype(BF16)


def _inproj(x2, g, w, *, tm=1024, tn=512):
    m = x2.shape[0]
    return pl.pallas_call(
        functools.partial(_inproj_kernel, tm=tm),
        grid=(m // tm, PROJ_COLS // tn),
        in_specs=[
            pl.BlockSpec((tm, D_MODEL), lambda i, j: (i, 0)),
            pl.BlockSpec((1, D_MODEL), lambda i, j: (0, 0)),
            pl.BlockSpec((D_MODEL, tn), lambda i, j: (0, j)),
        ],
        out_specs=pl.BlockSpec((tm, tn), lambda i, j: (i, j)),
        out_shape=jax.ShapeDtypeStruct((m, PROJ_COLS), BF16),
        scratch_shapes=[pltpu.VMEM((tm, D_MODEL), BF16)],
        compiler_params=pltpu.CompilerParams(
            dimension_semantics=("arbitrary", "arbitrary"), vmem_limit_bytes=VMEM_LIMIT),
        name="inproj",
    )(x2, g, w)


def _softmax_update(s, s_max, vt, m_ref, acc_ref):
    m_old = m_ref[...]
    m_new = jnp.maximum(m_old, s_max)
    p = jnp.exp2(s - m_new)
    alpha = jnp.exp2(m_old - m_new)
    acc_ref[...] = alpha * acc_ref[...] + jnp.dot(vt, p.astype(BF16), preferred_element_type=F32)
    m_ref[...] = m_new


def _normalised(acc_ref):
    return acc_ref[0:A_V_DIM, :] / acc_ref[A_V_DIM:A_V_DIM + 1, :]


def _attn_a_kernel(mode_ref, fast_ref, cb_ref, q_ref, k_ref, v_ref, ag_ref, qbt_ref, kb_ref, cq_ref,
                   ck_ref, posq_ref, posk_ref, lam_ref, subg_ref, o_ref,
                   qa1, qa2, vt, s1a, s2a, s1b, s2b, mt1a, mt2a, mt1b, mt2b, m1, m2, acc1, acc2,
                   *, tk, nq, nk):
    b = pl.program_id(0)
    h = pl.program_id(1)
    qi = pl.program_id(2)
    cb = cb_ref[h]

    @pl.when(qi == 0)
    def _():
        ones_rows = (lax.broadcasted_iota(jnp.int32, (BF16_ROWS, tk), 0) == 0).astype(BF16)

        def tr(ki, c):
            ks = pl.ds(pl.multiple_of(ki * tk, tk), tk)
            vt[ki, 0:A_V_DIM, :] = v_ref[ks, :].astype(F32).T.astype(BF16)
            vt[ki, A_V_DIM:V_ROWS, :] = ones_rows
            return c
        lax.fori_loop(0, nk, tr, 0)

    lo_r = lax.broadcasted_iota(jnp.int32, (LANE, 1), 0) < A_HEAD_DIM
    qs = q_ref[...].astype(F32).T
    qbias = qbt_ref[...].astype(F32) * cq_ref[...]
    qa1[...] = jnp.where(lo_r, qs, qbias).astype(BF16)
    qa2[...] = jnp.where(lo_r, qbias, qs).astype(BF16)
    for m_ref in (m1, m2):
        m_ref[...] = jnp.full(m_ref.shape, M_INIT, F32)
    for z_ref in (acc1, acc2):
        z_ref[...] = jnp.zeros(z_ref.shape, F32)

    lo_c = lax.broadcasted_iota(jnp.int32, (1, LANE), 1) < A_HEAD_DIM
    bufs_a = ((s1a, mt1a), (s2a, mt2a))
    bufs_b = ((s1b, mt1b), (s2b, mt2b))

    def raw_scores(ki, sgn):
        ks = pl.ds(pl.multiple_of(ki * tk, tk), tk)
        k = k_ref[ks, :]
        kbias = (kb_ref[ks, :].astype(F32) * (ck_ref[...] * sgn)).astype(BF16)
        s1 = jnp.dot(jnp.where(lo_c, k, kbias), qa1[...], preferred_element_type=F32)
        s2 = jnp.dot(jnp.where(lo_c, kbias, k), qa2[...], preferred_element_type=F32)
        return s1, s2

    def explicit_bias(ki):
        ks = pl.ds(pl.multiple_of(ki * tk, tk), tk)
        return -cb * jnp.abs(posk_ref[ks, :] - posq_ref[...])

    def scores(ki, bufs, general):
        if general:
            s1, s2 = raw_scores(ki, 0.0)
            bias = explicit_bias(ki)
            s1, s2 = s1 + bias, s2 + bias
        else:
            s1, s2 = raw_scores(ki, mode_ref[(b * nq + qi) * nk + ki].astype(F32))
        for s, (s_ref, mt_ref) in zip((s1, s2), bufs):
            s_ref[...] = s
            mt_ref[...] = jnp.max(s, axis=0, keepdims=True)

    def softmax(ki, bufs):
        for (s_ref, mt_ref), m_ref, acc_ref in zip(bufs, (m1, m2), (acc1, acc2)):
            _softmax_update(s_ref[...], mt_ref[...], vt[ki], m_ref, acc_ref)

    fast = fast_ref[b * nq + qi]

    @pl.when(fast == 1)
    def _():
        def tile(j):
            return j - 1 + (j - 1 >= qi).astype(jnp.int32)

        scores(qi, bufs_a, True)

        def pair(j, carry):
            t1, t2 = tile(2 * j + 1), tile(2 * j + 2)
            t0 = jnp.where(j == 0, qi, tile(2 * j))
            scores(t1, bufs_b, False)
            softmax(t0, bufs_a)
            scores(t2, bufs_a, False)
            softmax(t1, bufs_b)
            return carry

        lax.fori_loop(0, nk // 2 - 1, pair, 0)
        scores(tile(nk - 1), bufs_b, False)
        softmax(qi if nk == 2 else tile(nk - 2), bufs_a)
        softmax(tile(nk - 1), bufs_b)

    @pl.when(fast == 0)
    def _():
        def step(ki, carry):
            mode = mode_ref[(b * nq + qi) * nk + ki]

            @pl.when(mode == 0)
            def _():
                scores(ki, bufs_a, True)

            @pl.when(mode != 0)
            def _():
                scores(ki, bufs_a, False)

            softmax(ki, bufs_a)
            return carry

        lax.fori_loop(0, nk, step, 0)

    lamv = lam_ref[...]
    lam = (jnp.exp(jnp.sum(lamv[0:1] * lamv[1:2], keepdims=True))
           - jnp.exp(jnp.sum(lamv[2:3] * lamv[3:4], keepdims=True)) + LAM_INIT)
    o = _normalised(acc1) - lam * _normalised(acc2)
    ms = jnp.mean(o * o, axis=0, keepdims=True)
    on = (o * lax.rsqrt(ms + SUBLN_EPS)).T * subg_ref[...] * (1.0 - LAM_INIT)
    ag = ag_ref[...].astype(F32)
    o_ref[...] = (on * (ag * jax.nn.sigmoid(ag))).astype(BF16)


def _attn_a(mode, fast, cb, proj3, qbias_t, kbias, cq, ck, posq_row, posk_col, lamv, subg, *, tq, tk):
    bsz, s, _ = proj3.shape
    nq, nk = s // tq, s // tk
    kern = functools.partial(_attn_a_kernel, tk=tk, nq=nq, nk=nk)
    score_buf = pltpu.VMEM((tk, tq), F32)
    stat = pltpu.VMEM((1, tq), F32)
    grid_spec = pltpu.PrefetchScalarGridSpec(
        num_scalar_prefetch=3,
        grid=(bsz, A_HEADS, nq),
        in_specs=[
            pl.BlockSpec((None, tq, LANE), lambda b, h, i, *_: (b, i, COL_AQ + h)),
            pl.BlockSpec((None, s, LANE), lambda b, h, i, *_: (b, 0, COL_AK + h)),
            pl.BlockSpec((None, s, LANE), lambda b, h, i, *_: (b, 0, COL_AV + h)),
            pl.BlockSpec((None, tq, LANE), lambda b, h, i, *_: (b, i, COL_AG + h)),
            pl.BlockSpec((None, LANE, tq), lambda b, h, i, *_: (b, 0, i)),
            pl.BlockSpec((None, s, LANE), lambda b, h, i, *_: (b, 0, 0)),
            pl.BlockSpec((None, LANE, 1), lambda b, h, i, *_: (h, 0, 0)),
            pl.BlockSpec((None, 1, LANE), lambda b, h, i, *_: (h, 0, 0)),
            pl.BlockSpec((None, 1, tq), lambda b, h, i, *_: (b, 0, i)),
            pl.BlockSpec((None, s, 1), lambda b, h, i, *_: (b, 0, 0)),
            pl.BlockSpec((4, A_HEAD_DIM), lambda b, h, i, *_: (0, 0)),
            pl.BlockSpec((1, A_V_DIM), lambda b, h, i, *_: (0, 0)),
        ],
        out_specs=pl.BlockSpec((None, tq, LANE), lambda b, h, i, *_: (b, i, h)),
        scratch_shapes=[
            pltpu.VMEM((LANE, tq), BF16), pltpu.VMEM((LANE, tq), BF16),
            pltpu.VMEM((nk, V_ROWS, tk), BF16),
            score_buf, score_buf, score_buf, score_buf,
            stat, stat, stat, stat,
            stat, stat,
            pltpu.VMEM((V_ROWS, tq), F32), pltpu.VMEM((V_ROWS, tq), F32),
        ],
    )
    return pl.pallas_call(
        kern,
        grid_spec=grid_spec,
        out_shape=jax.ShapeDtypeStruct((bsz, s, A_WIDTH), BF16),
        compiler_params=pltpu.CompilerParams(
            dimension_semantics=("arbitrary", "arbitrary", "arbitrary"),
            vmem_limit_bytes=VMEM_LIMIT),
        name="attn_a",
    )(mode, fast, cb, proj3, proj3, proj3, proj3, qbias_t, kbias, cq, ck, posq_row, posk_col, lamv,
      subg)


def _mla_prep_kernel(cq_ref, ckv_ref, kr_ref, posc_ref, posr_ref, invl_ref, invc_ref, qg_ref, kvg_ref,
                     wuqt_ref, wuk_ref, wuvt_ref, vones_ref, qo_ref, ko_ref, vo_ref):
    def norm(ref, g_ref):
        t = ref[...].astype(F32)
        ms = jnp.mean(t * t, axis=-1, keepdims=True)
        return (t * lax.rsqrt(ms + NORM_EPS) * g_ref[...]).astype(BF16)

    half = B_ROPE // 2
    cqn = norm(cq_ref, qg_ref)
    ckvn = norm(ckv_ref, kvg_ref)

    qt = _dot_nt(wuqt_ref[...], cqn)
    ang_t = invc_ref[...] * posr_ref[...]
    cs_t = jnp.cos(ang_t)
    sn_t = jnp.sin(ang_t)
    for h in range(B_HEADS):
        r0 = h * 2 * LANE
        qo_ref[r0:r0 + LANE, :] = qt[r0:r0 + LANE, :].astype(BF16)
        t1 = qt[r0 + LANE:r0 + LANE + half, :]
        t2 = qt[r0 + LANE + half:r0 + LANE + 2 * half, :]
        qo_ref[r0 + LANE:r0 + LANE + half, :] = (t1 * cs_t - t2 * sn_t).astype(BF16)
        qo_ref[r0 + LANE + half:r0 + LANE + 2 * half, :] = (t1 * sn_t + t2 * cs_t).astype(BF16)
        qo_ref[r0 + LANE + 2 * half:r0 + 2 * LANE, :] = jnp.zeros((LANE - 2 * half, qt.shape[1]), BF16)

    lane = lax.broadcasted_iota(jnp.int32, (1, LANE), 1)
    ang = posc_ref[...] * invl_ref[...]
    cs = jnp.cos(ang)
    sn = jnp.sin(ang)
    c_t = jnp.where(lane < B_ROPE, cs, 0.0)
    s1_t = jnp.where(lane < half, -sn, 0.0)
    s2_t = jnp.where((lane >= half) & (lane < B_ROPE), sn, 0.0)
    kr = kr_ref[...].astype(F32)
    krope = (kr * c_t + pltpu.roll(kr, LANE - half, axis=1) * s1_t
             + pltpu.roll(kr, half, axis=1) * s2_t).astype(BF16)
    kf = jnp.dot(ckvn, wuk_ref[...], preferred_element_type=F32)
    for h in range(B_HEADS):
        c0 = h * 2 * LANE
        ko_ref[:, c0:c0 + LANE] = kf[:, h * LANE:(h + 1) * LANE].astype(BF16)
        ko_ref[:, c0 + LANE:c0 + 2 * LANE] = krope

    vo_ref[...] = (_dot_nt(wuvt_ref[...], ckvn) + vones_ref[...]).astype(BF16)


def _mla_prep(proj2, pos_col, pos_row, inv_lane, inv_col, qg, kvg, wuqt, wuk, wuvt, vones, *, tm):
    m = proj2.shape[0]
    wq = B_HEADS * 2 * LANE
    wv = B_HEADS * V_ROWS
    return pl.pallas_call(
        _mla_prep_kernel,
        grid=(m // tm,),
        in_specs=[
            pl.BlockSpec((tm, B_Q_LORA), lambda i: (i, COL_CQ // 4)),
            pl.BlockSpec((tm, B_KV_LORA), lambda i: (i, COL_CKV // 4)),
            pl.BlockSpec((tm, LANE), lambda i: (i, COL_KR)),
            pl.BlockSpec((tm, 1), lambda i: (i, 0)),
            pl.BlockSpec((1, tm), lambda i: (0, i)),
            pl.BlockSpec((1, LANE), lambda i: (0, 0)),
            pl.BlockSpec((B_ROPE // 2, 1), lambda i: (0, 0)),
            pl.BlockSpec((1, B_Q_LORA), lambda i: (0, 0)),
            pl.BlockSpec((1, B_KV_LORA), lambda i: (0, 0)),
            pl.BlockSpec((wq, B_Q_LORA), lambda i: (0, 0)),
            pl.BlockSpec((B_KV_LORA, B_HEADS * B_NOPE), lambda i: (0, 0)),
            pl.BlockSpec((wv, B_KV_LORA), lambda i: (0, 0)),
            pl.BlockSpec((wv, 1), lambda i: (0, 0)),
        ],
        out_specs=[
            pl.BlockSpec((None, wq, tm), lambda i: (i, 0, 0)),
            pl.BlockSpec((tm, wq), lambda i: (i, 0)),
            pl.BlockSpec((None, wv, tm), lambda i: (i, 0, 0)),
        ],
        out_shape=[
            jax.ShapeDtypeStruct((m // tm, wq, tm), BF16),
            jax.ShapeDtypeStruct((m, wq), BF16),
            jax.ShapeDtypeStruct((m // tm, wv, tm), BF16),
        ],
        compiler_params=pltpu.CompilerParams(
            dimension_semantics=("arbitrary",), vmem_limit_bytes=VMEM_LIMIT),
        name="mla_prep",
    )(proj2, proj2, proj2, pos_col, pos_row, inv_lane, inv_col, qg, kvg, wuqt, wuk, wuvt, vones)


def _attn_b_kernel(qt_ref, k_ref, vt_ref, bg_ref, o_ref, s_a, s_b, mt_a, mt_b, m1, acc1, *, tk, nk):
    m1[...] = jnp.full(m1.shape, M_INIT, F32)
    acc1[...] = jnp.zeros(acc1.shape, F32)

    def scores(ki, s_ref, mt_ref):
        ks = pl.ds(pl.multiple_of(ki * tk, tk), tk)
        s = jnp.dot(k_ref[ks, :], qt_ref[...], preferred_element_type=F32)
        s_ref[...] = s
        mt_ref[...] = jnp.max(s, axis=0, keepdims=True)

    def softmax(ki, s_ref, mt_ref):
        _softmax_update(s_ref[...], mt_ref[...], vt_ref[ki], m1, acc1)

    scores(0, s_a, mt_a)

    def pair(j, carry):
        scores(2 * j + 1, s_b, mt_b)
        softmax(2 * j, s_a, mt_a)
        scores(2 * j + 2, s_a, mt_a)
        softmax(2 * j + 1, s_b, mt_b)
        return carry

    lax.fori_loop(0, nk // 2 - 1, pair, 0)
    scores(nk - 1, s_b, mt_b)
    softmax(nk - 2, s_a, mt_a)
    softmax(nk - 1, s_b, mt_b)
    bg = bg_ref[...].astype(F32)
    o_ref[...] = (_normalised(acc1).T * (bg * jax.nn.sigmoid(bg))).astype(BF16)


def _attn_b(qt3, kb3, vt3, proj3, *, tq, tk):
    bsz, s, _ = kb3.shape
    nq, nk = s // tq, s // tk
    return pl.pallas_call(
        functools.partial(_attn_b_kernel, tk=tk, nk=nk),
        grid=(bsz, B_HEADS, nq),
        in_specs=[
            pl.BlockSpec((None, 2 * LANE, tq), lambda b, h, i: (b * nq + i, h, 0)),
            pl.BlockSpec((None, s, 2 * LANE), lambda b, h, i: (b, 0, h)),
            pl.BlockSpec((nk, V_ROWS, tk), lambda b, h, i: (b, h, 0)),
            pl.BlockSpec((None, tq, LANE), lambda b, h, i: (b, i, COL_BG + h)),
        ],
        out_specs=pl.BlockSpec((None, tq, LANE), lambda b, h, i: (b, i, h)),
        out_shape=jax.ShapeDtypeStruct((bsz, s, B_WIDTH), BF16),
        scratch_shapes=[
            pltpu.VMEM((tk, tq), F32), pltpu.VMEM((tk, tq), F32),
            pltpu.VMEM((1, tq), F32), pltpu.VMEM((1, tq), F32),
            pltpu.VMEM((1, tq), F32), pltpu.VMEM((V_ROWS, tq), F32),
        ],
        compiler_params=pltpu.CompilerParams(
            dimension_semantics=("arbitrary", "arbitrary", "arbitrary"),
            vmem_limit_bytes=VMEM_LIMIT),
        name="attn_b",
    )(qt3, kb3, vt3, proj3)


def _post_kernel(za_ref, zb_ref, ga_ref, gb_ref, x_ref, woa_ref, wob_ref, wout_ref, fg_ref, o_ref):
    ya = jnp.dot(za_ref[...], woa_ref[...], preferred_element_type=F32)
    yb = jnp.dot(zb_ref[...], wob_ref[...], preferred_element_type=F32)
    merged = (jax.nn.sigmoid(ga_ref[...].astype(F32)) * ya
              + jax.nn.sigmoid(gb_ref[...].astype(F32)) * yb)
    y = x_ref[...] + jnp.dot(merged.astype(BF16), wout_ref[...], preferred_element_type=F32)
    ms = jnp.mean(y * y, axis=-1, keepdims=True)
    o_ref[...] = y * lax.rsqrt(ms + NORM_EPS) * fg_ref[...]


def _post(za, zb, proj2, x2, woa, wob, wout, fg, *, tm=256):
    m = x2.shape[0]
    const = lambda shape: pl.BlockSpec(shape, lambda i: (0, 0), pipeline_mode=pl.Buffered(1))
    return pl.pallas_call(
        _post_kernel,
        grid=(m // tm,),
        in_specs=[
            pl.BlockSpec((tm, A_WIDTH), lambda i: (i, 0)),
            pl.BlockSpec((tm, B_WIDTH), lambda i: (i, 0)),
            pl.BlockSpec((tm, D_MODEL), lambda i: (i, COL_GA // 16)),
            pl.BlockSpec((tm, D_MODEL), lambda i: (i, COL_GB // 16)),
            pl.BlockSpec((tm, D_MODEL), lambda i: (i, 0)),
            const((A_WIDTH, D_MODEL)),
            const((B_WIDTH, D_MODEL)),
            const((D_MODEL, D_MODEL)),
            const((1, D_MODEL)),
        ],
        out_specs=pl.BlockSpec((tm, D_MODEL), lambda i: (i, 0)),
        out_shape=jax.ShapeDtypeStruct((m, D_MODEL), F32),
        compiler_params=pltpu.CompilerParams(
            dimension_semantics=("arbitrary",), vmem_limit_bytes=VMEM_LIMIT),
        name="post",
    )(za, zb, proj2, proj2, x2, woa, wob, wout, fg)


def _prep_w_in(w):
    aq, ak, av, ag, cq, ckv, kr, bg, mg = jnp.split(w, _SPLITS, axis=1)
    ga, gb = mg[:, :D_MODEL], mg[:, D_MODEL:]
    aq = aq * (A_HEAD_DIM ** -0.5 * LOG2E)
    pad = jnp.zeros((w.shape[0], PROJ_COLS - w.shape[1]), w.dtype)
    return jnp.concatenate([aq, ak, av, ag, ga, gb, bg, cq, ckv, kr, pad], axis=1).astype(BF16)


def _prep_w_uq_t(w):
    w = w.reshape(B_Q_LORA, B_HEADS, B_NOPE + B_ROPE) * ((B_NOPE + B_ROPE) ** -0.5 * LOG2E)
    w = jnp.pad(w, ((0, 0), (0, 0), (0, 2 * LANE - (B_NOPE + B_ROPE))))
    return w.reshape(B_Q_LORA, B_HEADS * 2 * LANE).T.astype(BF16)


def _prep_w_uv_t(w_ukv):
    w = w_ukv.reshape(B_KV_LORA, B_HEADS, B_NOPE + B_V_DIM)[:, :, B_NOPE:]
    w = jnp.pad(w, ((0, 0), (0, 0), (0, V_ROWS - B_V_DIM)))
    return w.reshape(B_KV_LORA, B_HEADS * V_ROWS).T.astype(BF16)


def _split3_bf16(c):
    c1 = c.astype(BF16).astype(F32)
    c2 = (c - c1).astype(BF16).astype(F32)
    c3 = (c - c1 - c2).astype(BF16).astype(F32)
    return c1, c2, c3


def _alibi_operands(positions, slopes, tq, tk):
    bsz, s = positions.shape
    nq, nk = s // tq, s // tk
    pmin = jnp.min(positions, axis=1, keepdims=True)
    rel = positions - pmin
    ok = ((jnp.min(rel, axis=1) >= 0) & (jnp.max(rel, axis=1) < 65536)
          & (jnp.max(jnp.abs(positions), axis=1) < (1 << 24)))
    qmin = jnp.min(positions.reshape(bsz, nq, tq), axis=-1)
    qmax = jnp.max(positions.reshape(bsz, nq, tq), axis=-1)
    kmin = jnp.min(positions.reshape(bsz, nk, tk), axis=-1)
    kmax = jnp.max(positions.reshape(bsz, nk, tk), axis=-1)
    after = qmin[:, :, None] >= kmax[:, None, :]
    before = qmax[:, :, None] <= kmin[:, None, :]
    mode = jnp.where(after, 1, jnp.where(before, -1, 0)) * ok[:, None, None].astype(jnp.int32)
    off_diag = ~jnp.eye(nq, nk, dtype=bool)[None]
    fast = jnp.all((mode != 0) | ~off_diag, axis=-1) & (nq == nk) & (nk >= 2) & (nk % 2 == 0)

    rel = jnp.where(ok[:, None], rel, 0)
    hi = (rel >> 8).astype(F32)
    lo = (rel & 255).astype(F32)
    one = jnp.ones_like(hi)
    pos6 = jnp.stack([hi, lo] * 3, axis=-1)
    one6 = jnp.stack([one] * 6, axis=-1)

    def place(c, axis):
        shape = list(c.shape)
        shape[axis] = A_HEAD_DIM - N_BIAS
        z = jnp.zeros(shape, F32)
        return jnp.concatenate([c, z, c, z], axis=axis)

    qb_t = jnp.swapaxes(place(jnp.concatenate([pos6, one6], axis=-1), -1), 1, 2).astype(BF16)
    kb = place(jnp.concatenate([one6, pos6], axis=-1), -1).astype(BF16)

    cb = slopes * LOG2E
    pieces = _split3_bf16(cb)
    cpair = jnp.stack([v for c in pieces for v in (256.0 * c, c)], axis=-1)
    one_h6 = jnp.ones_like(cpair)
    cq = place(jnp.concatenate([one_h6, cpair], axis=-1), -1)[:, :, None]
    ck = place(jnp.concatenate([-cpair, one_h6], axis=-1), -1)[:, None, :]
    return (mode.reshape(-1).astype(jnp.int32), fast.reshape(-1).astype(jnp.int32), cb.astype(F32),
            qb_t, kb, cq, ck)


def kernel(x, positions, norm_g, w_in, lam_q1, lam_k1, lam_q2, lam_k2, a_subln_g, w_oa, q_norm_g,
           w_uq, kv_norm_g, w_ukv, w_ob, w_out, final_g):
    bsz, s, d = x.shape
    m = bsz * s
    assert norm_g.shape[0] == 1 and d == D_MODEL and s % TQ == 0 and s % TK == 0
    layer = 0
    x2 = x.reshape(m, d)
    pos_f = positions.astype(F32)

    proj2 = _inproj(x2, norm_g[layer][None, :], _prep_w_in(w_in[layer]))
    proj3 = proj2.reshape(bsz, s, PROJ_COLS)

    slopes = jnp.asarray([2.0 ** (-8.0 * (h + 1) / A_HEADS) for h in range(A_HEADS)], dtype=F32)
    mode, fast, cb, qb_t, kb, cq, ck = _alibi_operands(positions, slopes, TQ, TK)
    lamv = jnp.stack([lam_q1[layer], lam_k1[layer], lam_q2[layer], lam_k2[layer]]).astype(F32)
    za = _attn_a(mode, fast, cb, proj3, qb_t, kb, cq, ck, pos_f[:, None, :], pos_f[:, :, None], lamv,
                 a_subln_g[layer][None, :], tq=TQ, tk=TK)

    half = B_ROPE // 2
    inv = ROPE_THETA ** (-jnp.arange(half, dtype=F32) / half)
    inv_lane = jnp.concatenate([inv, inv, jnp.zeros((LANE - B_ROPE,), F32)])[None, :]
    wuk = (w_ukv[layer].reshape(B_KV_LORA, B_HEADS, B_NOPE + B_V_DIM)[:, :, :B_NOPE]
           .reshape(B_KV_LORA, B_HEADS * B_NOPE).astype(BF16))
    vones = (jnp.arange(B_HEADS * V_ROWS) % V_ROWS == B_V_DIM).astype(F32)[:, None]
    qt3, kb2, vt3 = _mla_prep(proj2, pos_f.reshape(m, 1), pos_f.reshape(1, m), inv_lane, inv[:, None],
                              q_norm_g[layer][None, :], kv_norm_g[layer][None, :],
                              _prep_w_uq_t(w_uq[layer]), wuk, _prep_w_uv_t(w_ukv[layer]), vones,
                              tm=TK)
    zb = _attn_b(qt3, kb2.reshape(bsz, s, -1), vt3, proj3, tq=TQ, tk=TK)

    out = _post(za.reshape(m, A_WIDTH), zb.reshape(m, B_WIDTH), proj2, x2, w_oa[layer].astype(BF16),
                w_ob[layer].astype(BF16), w_out[layer].astype(BF16), final_g[None, :])
    return out.reshape(bsz, s, d)
```

```python
import functools
import math

import numpy as np
import jax
import jax.numpy as jnp
from jax import lax
from jax.experimental import pallas as pl
from jax.experimental.pallas import tpu as pltpu

F32 = jnp.float32
BF16 = jnp.bfloat16

D_MODEL = 2048
A_HEADS = 8
A_HEAD_DIM = 64
A_V_DIM = 128
A_WIDTH = 1024
B_HEADS = 8
B_Q_LORA = 512
B_KV_LORA = 512
B_NOPE = 128
B_ROPE = 64
B_V_DIM = 128
B_WIDTH = 1024
ROPE_THETA = 10000.0
NORM_EPS = 1e-6
SUBLN_EPS = 1e-5
LAM_INIT = 0.8 - 0.6 * math.exp(-0.3 * 0)
LOG2E = math.log2(math.e)

_SIZES = [1024, 1024, 1024, 1024, 512, 512, 64, 1024, 4096]
_SPLITS = [int(v) for v in np.cumsum(_SIZES)[:-1]]

LANE = 128
BF16_ROWS = 16
COL_AQ, COL_AK, COL_AV, COL_AG = 0, 8, 16, 24
COL_GA, COL_GB, COL_BG = 32, 48, 64
COL_CQ, COL_CKV, COL_KR = 72, 76, 80
PROJ_COLS = 10752

TQ = 512
TK = 512
V_ROWS = A_V_DIM + BF16_ROWS
N_BIAS = 12
VMEM_LIMIT = 56 * 1024 * 1024
M_INIT = -1e30
SPEC_MARGIN = 60.0


def _dot_nt(a, b):
    return lax.dot_general(a, b, (((1,), (1,)), ((), ())), preferred_element_type=F32)


def _inproj_kernel(x_ref, g_ref, w_ref, o_ref, h_ref, *, tm):
    @pl.when(pl.program_id(1) == 0)
    def _():
        def body(r, c):
            rows = pl.ds(pl.multiple_of(r * 128, 128), 128)
            x = x_ref[rows, :]
            ms = jnp.mean(x * x, axis=-1, keepdims=True)
            h_ref[rows, :] = (x * lax.rsqrt(ms + NORM_EPS) * g_ref[...]).astype(BF16)
            return c
        lax.fori_loop(0, tm // 128, body, 0)

    o_ref[...] = jnp.dot(h_ref[...], w_ref[...], preferred_element_type=F32).astype(BF16)


def _inproj(x2, g, w, *, tm=1024, tn=512):
    m = x2.shape[0]
    return pl.pallas_call(
        functools.partial(_inproj_kernel, tm=tm),
        grid=(m // tm, PROJ_COLS // tn),
        in_specs=[
            pl.BlockSpec((tm, D_MODEL), lambda i, j: (i, 0)),
            pl.BlockSpec((1, D_MODEL), lambda i, j: (0, 0)),
            pl.BlockSpec((D_MODEL, tn), lambda i, j: (0, j)),
        ],
        out_specs=pl.BlockSpec((tm, tn), lambda i, j: (i, j)),
        out_shape=jax.ShapeDtypeStruct((m, PROJ_COLS), BF16),
        scratch_shapes=[pltpu.VMEM((tm, D_MODEL), BF16)],
        compiler_params=pltpu.CompilerParams(
            dimension_semantics=("arbitrary", "arbitrary"), vmem_limit_bytes=VMEM_LIMIT),
        name="inproj",
    )(x2, g, w)


def _softmax_update(s, s_max, vt, m_ref, acc_ref):
    m_old = m_ref[...]
    m_new = jnp.maximum(m_old, s_max)
    p = jnp.exp2(s - m_new)
    alpha = jnp.exp2(m_old - m_new)
    acc_ref[...] = alpha * acc_ref[...] + jnp.dot(vt, p.astype(BF16), preferred_element_type=F32)
    m_ref[...] = m_new


def _normalised(acc_ref):
    return acc_ref[0:A_V_DIM, :] / acc_ref[A_V_DIM:A_V_DIM + 1, :]


def _attn_a_kernel(mode_ref, fast_ref, cb_ref, q_ref, k_ref, v_ref, ag_ref, qbt_ref, kb_ref, cq_ref,
                   ck_ref, posq_ref, posk_ref, lam_ref, subg_ref, o_ref,
                   qa1, qa2, vt, m1, m2, acc1, acc2, flag_ref, *, tk, nq, nk):
    b = pl.program_id(0)
    h = pl.program_id(1)
    qi = pl.program_id(2)
    cb = cb_ref[h]

    @pl.when(qi == 0)
    def _():
        ones_rows = (lax.broadcasted_iota(jnp.int32, (BF16_ROWS, tk), 0) == 0).astype(BF16)

        def tr(ki, c):
            ks = pl.ds(pl.multiple_of(ki * tk, tk), tk)
            vt[ki, 0:A_V_DIM, :] = v_ref[ks, :].astype(F32).T.astype(BF16)
            vt[ki, A_V_DIM:V_ROWS, :] = ones_rows
            return c
        lax.fori_loop(0, nk, tr, 0)

    lo_r = lax.broadcasted_iota(jnp.int32, (LANE, 1), 0) < A_HEAD_DIM
    qs = q_ref[...].astype(F32).T
    qbias = qbt_ref[...].astype(F32) * cq_ref[...]
    qa1[...] = jnp.where(lo_r, qs, qbias).astype(BF16)
    qa2[...] = jnp.where(lo_r, qbias, qs).astype(BF16)
    lo_c = lax.broadcasted_iota(jnp.int32, (1, LANE), 1) < A_HEAD_DIM

    def raw_scores(ki, sgn):
        ks = pl.ds(pl.multiple_of(ki * tk, tk), tk)
        k = k_ref[ks, :]
        kbias = (kb_ref[ks, :].astype(F32) * (ck_ref[...] * sgn)).astype(BF16)
        s1 = jnp.dot(jnp.where(lo_c, k, kbias), qa1[...], preferred_element_type=F32)
        s2 = jnp.dot(jnp.where(lo_c, kbias, k), qa2[...], preferred_element_type=F32)
        return s1, s2

    def explicit_bias(ki):
        ks = pl.ds(pl.multiple_of(ki * tk, tk), tk)
        return -cb * jnp.abs(posk_ref[ks, :] - posq_ref[...])

    def tile_scores(ki, general):
        if general:
            s1, s2 = raw_scores(ki, 0.0)
            bias = explicit_bias(ki)
            return s1 + bias, s2 + bias
        return raw_scores(ki, mode_ref[(b * nq + qi) * nk + ki].astype(F32))

    def col_max(s):
        return jnp.max(s, axis=0, keepdims=True)

    def pv(ki, s, shift):
        return jnp.dot(vt[ki], jnp.exp2(s - shift).astype(BF16), preferred_element_type=F32)

    fast = fast_ref[b * nq + qi]
    flag_ref[0] = 1 - fast

    @pl.when(fast == 1)
    def _():
        s1, s2 = tile_scores(qi, True)
        sh1, sh2 = col_max(s1), col_max(s2)
        top1, top2 = sh1, sh2
        acc1[...] = pv(qi, s1, sh1)
        acc2[...] = pv(qi, s2, sh2)
        for j in range(nk - 1):
            t = j + (j >= qi).astype(jnp.int32)
            s1, s2 = tile_scores(t, False)
            top1, top2 = jnp.maximum(top1, col_max(s1)), jnp.maximum(top2, col_max(s2))
            acc1[...] += pv(t, s1, sh1)
            acc2[...] += pv(t, s2, sh2)
        excess = jnp.max(jnp.maximum(top1 - sh1, top2 - sh2))
        flag_ref[0] = (excess > SPEC_MARGIN).astype(jnp.int32)

    @pl.when(flag_ref[0] == 1)
    def _():
        for m_ref in (m1, m2):
            m_ref[...] = jnp.full(m_ref.shape, M_INIT, F32)
        for z_ref in (acc1, acc2):
            z_ref[...] = jnp.zeros(z_ref.shape, F32)

        def update(ki, general):
            for s, m_ref, acc_ref in zip(tile_scores(ki, general), (m1, m2), (acc1, acc2)):
                _softmax_update(s, col_max(s), vt[ki], m_ref, acc_ref)

        def step(ki, carry):
            mode = mode_ref[(b * nq + qi) * nk + ki]

            @pl.when(mode == 0)
            def _():
                update(ki, True)

            @pl.when(mode != 0)
            def _():
                update(ki, False)

            return carry

        lax.fori_loop(0, nk, step, 0)

    lamv = lam_ref[...]
    lam = (jnp.exp(jnp.sum(lamv[0:1] * lamv[1:2], keepdims=True))
           - jnp.exp(jnp.sum(lamv[2:3] * lamv[3:4], keepdims=True)) + LAM_INIT)
    o = _normalised(acc1) - lam * _normalised(acc2)
    ms = jnp.mean(o * o, axis=0, keepdims=True)
    on = (o * lax.rsqrt(ms + SUBLN_EPS)).T * subg_ref[...] * (1.0 - LAM_INIT)
    ag = ag_ref[...].astype(F32)
    o_ref[...] = (on * (ag * jax.nn.sigmoid(ag))).astype(BF16)


def _attn_a(mode, fast, cb, proj3, qbias_t, kbias, cq, ck, posq_row, posk_col, lamv, subg, *, tq, tk):
    bsz, s, _ = proj3.shape
    nq, nk = s // tq, s // tk
    kern = functools.partial(_attn_a_kernel, tk=tk, nq=nq, nk=nk)
    stat = pltpu.VMEM((1, tq), F32)
    grid_spec = pltpu.PrefetchScalarGridSpec(
        num_scalar_prefetch=3,
        grid=(bsz, A_HEADS, nq),
        in_specs=[
            pl.BlockSpec((None, tq, LANE), lambda b, h, i, *_: (b, i, COL_AQ + h)),
            pl.BlockSpec((None, s, LANE), lambda b, h, i, *_: (b, 0, COL_AK + h)),
            pl.BlockSpec((None, s, LANE), lambda b, h, i, *_: (b, 0, COL_AV + h)),
            pl.BlockSpec((None, tq, LANE), lambda b, h, i, *_: (b, i, COL_AG + h)),
            pl.BlockSpec((None, LANE, tq), lambda b, h, i, *_: (b, 0, i)),
            pl.BlockSpec((None, s, LANE), lambda b, h, i, *_: (b, 0, 0)),
            pl.BlockSpec((None, LANE, 1), lambda b, h, i, *_: (h, 0, 0)),
            pl.BlockSpec((None, 1, LANE), lambda b, h, i, *_: (h, 0, 0)),
            pl.BlockSpec((None, 1, tq), lambda b, h, i, *_: (b, 0, i)),
            pl.BlockSpec((None, s, 1), lambda b, h, i, *_: (b, 0, 0)),
            pl.BlockSpec((4, A_HEAD_DIM), lambda b, h, i, *_: (0, 0)),
            pl.BlockSpec((1, A_V_DIM), lambda b, h, i, *_: (0, 0)),
        ],
        out_specs=pl.BlockSpec((None, tq, LANE), lambda b, h, i, *_: (b, i, h)),
        scratch_shapes=[
            pltpu.VMEM((LANE, tq), BF16), pltpu.VMEM((LANE, tq), BF16),
            pltpu.VMEM((nk, V_ROWS, tk), BF16),
            stat, stat,
            pltpu.VMEM((V_ROWS, tq), F32), pltpu.VMEM((V_ROWS, tq), F32),
            pltpu.SMEM((1,), jnp.int32),
        ],
    )
    return pl.pallas_call(
        kern,
        grid_spec=grid_spec,
        out_shape=jax.ShapeDtypeStruct((bsz, s, A_WIDTH), BF16),
        compiler_params=pltpu.CompilerParams(
            dimension_semantics=("arbitrary", "arbitrary", "arbitrary"),
            vmem_limit_bytes=VMEM_LIMIT),
        name="attn_a",
    )(mode, fast, cb, proj3, proj3, proj3, proj3, qbias_t, kbias, cq, ck, posq_row, posk_col, lamv,
      subg)


def _mla_prep_kernel(cq_ref, ckv_ref, kr_ref, posc_ref, posr_ref, invl_ref, invc_ref, qg_ref, kvg_ref,
                     wuqt_ref, wuk_ref, wuvt_ref, vones_ref, qo_ref, ko_ref, vo_ref):
    def norm(ref, g_ref):
        t = ref[...].astype(F32)
        ms = jnp.mean(t * t, axis=-1, keepdims=True)
        return (t * lax.rsqrt(ms + NORM_EPS) * g_ref[...]).astype(BF16)

    half = B_ROPE // 2
    cqn = norm(cq_ref, qg_ref)
    ckvn = norm(ckv_ref, kvg_ref)

    qt = _dot_nt(wuqt_ref[...], cqn)
    ang_t = invc_ref[...] * posr_ref[...]
    cs_t = jnp.cos(ang_t)
    sn_t = jnp.sin(ang_t)
    for h in range(B_HEADS):
        r0 = h * 2 * LANE
        qo_ref[r0:r0 + LANE, :] = qt[r0:r0 + LANE, :].astype(BF16)
        t1 = qt[r0 + LANE:r0 + LANE + half, :]
        t2 = qt[r0 + LANE + half:r0 + LANE + 2 * half, :]
        qo_ref[r0 + LANE:r0 + LANE + half, :] = (t1 * cs_t - t2 * sn_t).astype(BF16)
        qo_ref[r0 + LANE + half:r0 + LANE + 2 * half, :] = (t1 * sn_t + t2 * cs_t).astype(BF16)
        qo_ref[r0 + LANE + 2 * half:r0 + 2 * LANE, :] = jnp.zeros((LANE - 2 * half, qt.shape[1]), BF16)

    lane = lax.broadcasted_iota(jnp.int32, (1, LANE), 1)
    ang = posc_ref[...] * invl_ref[...]
    cs = jnp.cos(ang)
    sn = jnp.sin(ang)
    c_t = jnp.where(lane < B_ROPE, cs, 0.0)
    s1_t = jnp.where(lane < half, -sn, 0.0)
    s2_t = jnp.where((lane >= half) & (lane < B_ROPE), sn, 0.0)
    kr = kr_ref[...].astype(F32)
    krope = (kr * c_t + pltpu.roll(kr, LANE - half, axis=1) * s1_t
             + pltpu.roll(kr, half, axis=1) * s2_t).astype(BF16)
    kf = jnp.dot(ckvn, wuk_ref[...], preferred_element_type=F32)
    for h in range(B_HEADS):
        c0 = h * 2 * LANE
        ko_ref[:, c0:c0 + LANE] = kf[:, h * LANE:(h + 1) * LANE].astype(BF16)
        ko_ref[:, c0 + LANE:c0 + 2 * LANE] = krope

    vo_ref[...] = (_dot_nt(wuvt_ref[...], ckvn) + vones_ref[...]).astype(BF16)


def _mla_prep(proj2, pos_col, pos_row, inv_lane, inv_col, qg, kvg, wuqt, wuk, wuvt, vones, *, tm):
    m = proj2.shape[0]
    wq = B_HEADS * 2 * LANE
    wv = B_HEADS * V_ROWS
    return pl.pallas_call(
        _mla_prep_kernel,
        grid=(m // tm,),
        in_specs=[
            pl.BlockSpec((tm, B_Q_LORA), lambda i: (i, COL_CQ // 4)),
            pl.BlockSpec((tm, B_KV_LORA), lambda i: (i, COL_CKV // 4)),
            pl.BlockSpec((tm, LANE), lambda i: (i, COL_KR)),
            pl.BlockSpec((tm, 1), lambda i: (i, 0)),
            pl.BlockSpec((1, tm), lambda i: (0, i)),
            pl.BlockSpec((1, LANE), lambda i: (0, 0)),
            pl.BlockSpec((B_ROPE // 2, 1), lambda i: (0, 0)),
            pl.BlockSpec((1, B_Q_LORA), lambda i: (0, 0)),
            pl.BlockSpec((1, B_KV_LORA), lambda i: (0, 0)),
            pl.BlockSpec((wq, B_Q_LORA), lambda i: (0, 0)),
            pl.BlockSpec((B_KV_LORA, B_HEADS * B_NOPE), lambda i: (0, 0)),
            pl.BlockSpec((wv, B_KV_LORA), lambda i: (0, 0)),
            pl.BlockSpec((wv, 1), lambda i: (0, 0)),
        ],
        out_specs=[
            pl.BlockSpec((None, wq, tm), lambda i: (i, 0, 0)),
            pl.BlockSpec((tm, wq), lambda i: (i, 0)),
            pl.BlockSpec((None, wv, tm), lambda i: (i, 0, 0)),
        ],
        out_shape=[
            jax.ShapeDtypeStruct((m // tm, wq, tm), BF16),
            jax.ShapeDtypeStruct((m, wq), BF16),
            jax.ShapeDtypeStruct((m // tm, wv, tm), BF16),
        ],
        compiler_params=pltpu.CompilerParams(
            dimension_semantics=("arbitrary",), vmem_limit_bytes=VMEM_LIMIT),
        name="mla_prep",
    )(proj2, proj2, proj2, pos_col, pos_row, inv_lane, inv_col, qg, kvg, wuqt, wuk, wuvt, vones)


def _attn_b_kernel(qt_ref, k_ref, vt_ref, bg_ref, o_ref, m1, acc1, flag_ref, *, tk, nk):
    def raw_scores(ki):
        ks = pl.ds(pl.multiple_of(ki * tk, tk), tk)
        return jnp.dot(k_ref[ks, :], qt_ref[...], preferred_element_type=F32)

    s = raw_scores(0)
    shift = jnp.max(s, axis=0, keepdims=True)
    top = shift
    acc1[...] = jnp.dot(vt_ref[0], jnp.exp2(s - shift).astype(BF16), preferred_element_type=F32)
    for ki in range(1, nk):
        s = raw_scores(ki)
        top = jnp.maximum(top, jnp.max(s, axis=0, keepdims=True))
        acc1[...] += jnp.dot(vt_ref[ki], jnp.exp2(s - shift).astype(BF16),
                             preferred_element_type=F32)
    flag_ref[0] = (jnp.max(top - shift) > SPEC_MARGIN).astype(jnp.int32)

    @pl.when(flag_ref[0] == 1)
    def _():
        m1[...] = jnp.full(m1.shape, M_INIT, F32)
        acc1[...] = jnp.zeros(acc1.shape, F32)

        def step(ki, carry):
            s = raw_scores(ki)
            _softmax_update(s, jnp.max(s, axis=0, keepdims=True), vt_ref[ki], m1, acc1)
            return carry

        lax.fori_loop(0, nk, step, 0)

    bg = bg_ref[...].astype(F32)
    o_ref[...] = (_normalised(acc1).T * (bg * jax.nn.sigmoid(bg))).astype(BF16)


def _attn_b(qt3, kb3, vt3, proj3, *, tq, tk):
    bsz, s, _ = kb3.shape
    nq, nk = s // tq, s // tk
    return pl.pallas_call(
        functools.partial(_attn_b_kernel, tk=tk, nk=nk),
        grid=(bsz, B_HEADS, nq),
        in_specs=[
            pl.BlockSpec((None, 2 * LANE, tq), lambda b, h, i: (b * nq + i, h, 0)),
            pl.BlockSpec((None, s, 2 * LANE), lambda b, h, i: (b, 0, h)),
            pl.BlockSpec((nk, V_ROWS, tk), lambda b, h, i: (b, h, 0)),
            pl.BlockSpec((None, tq, LANE), lambda b, h, i: (b, i, COL_BG + h)),
        ],
        out_specs=pl.BlockSpec((None, tq, LANE), lambda b, h, i: (b, i, h)),
        out_shape=jax.ShapeDtypeStruct((bsz, s, B_WIDTH), BF16),
        scratch_shapes=[
            pltpu.VMEM((1, tq), F32), pltpu.VMEM((V_ROWS, tq), F32), pltpu.SMEM((1,), jnp.int32),
        ],
        compiler_params=pltpu.CompilerParams(
            dimension_semantics=("arbitrary", "arbitrary", "arbitrary"),
            vmem_limit_bytes=VMEM_LIMIT),
        name="attn_b",
    )(qt3, kb3, vt3, proj3)


def _post_kernel(za_ref, zb_ref, ga_ref, gb_ref, x_ref, woa_ref, wob_ref, wout_ref, fg_ref, o_ref):
    ya = jnp.dot(za_ref[...], woa_ref[...], preferred_element_type=F32)
    yb = jnp.dot(zb_ref[...], wob_ref[...], preferred_element_type=F32)
    merged = (jax.nn.sigmoid(ga_ref[...].astype(F32)) * ya
              + jax.nn.sigmoid(gb_ref[...].astype(F32)) * yb)
    y = x_ref[...] + jnp.dot(merged.astype(BF16), wout_ref[...], preferred_element_type=F32)
    ms = jnp.mean(y * y, axis=-1, keepdims=True)
    o_ref[...] = y * lax.rsqrt(ms + NORM_EPS) * fg_ref[...]


def _post(za, zb, proj2, x2, woa, wob, wout, fg, *, tm=256):
    m = x2.shape[0]
    const = lambda shape: pl.BlockSpec(shape, lambda i: (0, 0), pipeline_mode=pl.Buffered(1))
    return pl.pallas_call(
        _post_kernel,
        grid=(m // tm,),
        in_specs=[
            pl.BlockSpec((tm, A_WIDTH), lambda i: (i, 0)),
            pl.BlockSpec((tm, B_WIDTH), lambda i: (i, 0)),
            pl.BlockSpec((tm, D_MODEL), lambda i: (i, COL_GA // 16)),
            pl.BlockSpec((tm, D_MODEL), lambda i: (i, COL_GB // 16)),
            pl.BlockSpec((tm, D_MODEL), lambda i: (i, 0)),
            const((A_WIDTH, D_MODEL)),
            const((B_WIDTH, D_MODEL)),
            const((D_MODEL, D_MODEL)),
            const((1, D_MODEL)),
        ],
        out_specs=pl.BlockSpec((tm, D_MODEL), lambda i: (i, 0)),
        out_shape=jax.ShapeDtypeStruct((m, D_MODEL), F32),
        compiler_params=pltpu.CompilerParams(
            dimension_semantics=("arbitrary",), vmem_limit_bytes=VMEM_LIMIT),
        name="post",
    )(za, zb, proj2, proj2, x2, woa, wob, wout, fg)


def _prep_w_in(w):
    aq, ak, av, ag, cq, ckv, kr, bg, mg = jnp.split(w, _SPLITS, axis=1)
    ga, gb = mg[:, :D_MODEL], mg[:, D_MODEL:]
    aq = aq * (A_HEAD_DIM ** -0.5 * LOG2E)
    pad = jnp.zeros((w.shape[0], PROJ_COLS - w.shape[1]), w.dtype)
    return jnp.concatenate([aq, ak, av, ag, ga, gb, bg, cq, ckv, kr, pad], axis=1).astype(BF16)


def _prep_w_uq_t(w):
    w = w.reshape(B_Q_LORA, B_HEADS, B_NOPE + B_ROPE) * ((B_NOPE + B_ROPE) ** -0.5 * LOG2E)
    w = jnp.pad(w, ((0, 0), (0, 0), (0, 2 * LANE - (B_NOPE + B_ROPE))))
    return w.reshape(B_Q_LORA, B_HEADS * 2 * LANE).T.astype(BF16)


def _prep_w_uv_t(w_ukv):
    w = w_ukv.reshape(B_KV_LORA, B_HEADS, B_NOPE + B_V_DIM)[:, :, B_NOPE:]
    w = jnp.pad(w, ((0, 0), (0, 0), (0, V_ROWS - B_V_DIM)))
    return w.reshape(B_KV_LORA, B_HEADS * V_ROWS).T.astype(BF16)


def _split3_bf16(c):
    c1 = c.astype(BF16).astype(F32)
    c2 = (c - c1).astype(BF16).astype(F32)
    c3 = (c - c1 - c2).astype(BF16).astype(F32)
    return c1, c2, c3


def _alibi_operands(positions, slopes, tq, tk):
    bsz, s = positions.shape
    nq, nk = s // tq, s // tk
    pmin = jnp.min(positions, axis=1, keepdims=True)
    rel = positions - pmin
    ok = ((jnp.min(rel, axis=1) >= 0) & (jnp.max(rel, axis=1) < 65536)
          & (jnp.max(jnp.abs(positions), axis=1) < (1 << 24)))
    qmin = jnp.min(positions.reshape(bsz, nq, tq), axis=-1)
    qmax = jnp.max(positions.reshape(bsz, nq, tq), axis=-1)
    kmin = jnp.min(positions.reshape(bsz, nk, tk), axis=-1)
    kmax = jnp.max(positions.reshape(bsz, nk, tk), axis=-1)
    after = qmin[:, :, None] >= kmax[:, None, :]
    before = qmax[:, :, None] <= kmin[:, None, :]
    mode = jnp.where(after, 1, jnp.where(before, -1, 0)) * ok[:, None, None].astype(jnp.int32)
    off_diag = ~jnp.eye(nq, nk, dtype=bool)[None]
    fast = jnp.all((mode != 0) | ~off_diag, axis=-1) & (nq == nk)

    rel = jnp.where(ok[:, None], rel, 0)
    hi = (rel >> 8).astype(F32)
    lo = (rel & 255).astype(F32)

    g = np.arange(LANE) % A_HEAD_DIM
    first, second = g < N_BIAS // 2, (g >= N_BIAS // 2) & (g < N_BIAS)
    even, odd = g % 2 == 0, g % 2 == 1
    f32 = lambda mask: jnp.asarray(mask, F32)
    q_hi, q_lo, q_one = f32(first & even), f32(first & odd), f32(second)
    k_hi, k_lo, k_one = f32(second & even), f32(second & odd), f32(first)
    qb_t = (hi[:, None, :] * q_hi[None, :, None] + lo[:, None, :] * q_lo[None, :, None]
            + q_one[None, :, None]).astype(BF16)
    kb = (hi[:, :, None] * k_hi[None, None, :] + lo[:, :, None] * k_lo[None, None, :]
          + k_one[None, None, :]).astype(BF16)

    cb = slopes * LOG2E
    pieces = _split3_bf16(cb)
    cpair = jnp.stack([v for c in pieces for v in (256.0 * c, c)], axis=-1)
    sel = np.zeros((N_BIAS // 2, LANE), np.float32)
    for i in range(N_BIAS // 2):
        sel[i, (g % (N_BIAS // 2) == i) & (g < N_BIAS)] = 1.0
    spread = jnp.dot(cpair, jnp.asarray(sel), precision=lax.Precision.HIGHEST)
    cq = (spread * f32(second) + f32(first))[:, :, None]
    ck = (-spread * f32(first) + f32(second))[:, None, :]
    return (mode.reshape(-1).astype(jnp.int32), fast.reshape(-1).astype(jnp.int32), cb.astype(F32),
            qb_t, kb, cq, ck)


def kernel(x, positions, norm_g, w_in, lam_q1, lam_k1, lam_q2, lam_k2, a_subln_g, w_oa, q_norm_g,
           w_uq, kv_norm_g, w_ukv, w_ob, w_out, final_g):
    bsz, s, d = x.shape
    m = bsz * s
    assert norm_g.shape[0] == 1 and d == D_MODEL and s % TQ == 0 and s % TK == 0
    layer = 0
    x2 = x.reshape(m, d)
    pos_f = positions.astype(F32)

    proj2 = _inproj(x2, norm_g[layer][None, :], _prep_w_in(w_in[layer]))
    proj3 = proj2.reshape(bsz, s, PROJ_COLS)

    slopes = jnp.asarray([2.0 ** (-8.0 * (h + 1) / A_HEADS) for h in range(A_HEADS)], dtype=F32)
    mode, fast, cb, qb_t, kb, cq, ck = _alibi_operands(positions, slopes, TQ, TK)
    lamv = jnp.stack([lam_q1[layer], lam_k1[layer], lam_q2[layer], lam_k2[layer]]).astype(F32)
    za = _attn_a(mode, fast, cb, proj3, qb_t, kb, cq, ck, pos_f[:, None, :], pos_f[:, :, None], lamv,
                 a_subln_g[layer][None, :], tq=TQ, tk=TK)

    half = B_ROPE // 2
    inv = ROPE_THETA ** (-jnp.arange(half, dtype=F32) / half)
    inv_lane = jnp.concatenate([inv, inv, jnp.zeros((LANE - B_ROPE,), F32)])[None, :]
    wuk = (w_ukv[layer].reshape(B_KV_LORA, B_HEADS, B_NOPE + B_V_DIM)[:, :, :B_NOPE]
           .reshape(B_KV_LORA, B_HEADS * B_NOPE).astype(BF16))
    vones = (jnp.arange(B_HEADS * V_ROWS) % V_ROWS == B_V_DIM).astype(F32)[:, None]
    qt3, kb2, vt3 = _mla_prep(proj2, pos_f.reshape(m, 1), pos_f.reshape(1, m), inv_lane, inv[:, None],
                              q_norm_g[layer][None, :], kv_norm_g[layer][None, :],
                              _prep_w_uq_t(w_uq[layer]), wuk, _prep_w_uv_t(w_ukv[layer]), vones,
                              tm=TK)
    zb = _attn_b(qt3, kb2.reshape(bsz, s, -1), vt3, proj3, tq=TQ, tk=TK)

    out = _post(za.reshape(m, A_WIDTH), zb.reshape(m, B_WIDTH), proj2, x2, w_oa[layer].astype(BF16),
                w_ob[layer].astype(BF16), w_out[layer].astype(BF16), final_g[None, :])
    return out.reshape(bsz, s, d)
```

```python
import functools
import math

import numpy as np
import jax
import jax.numpy as jnp
from jax import lax
from jax.experimental import pallas as pl
from jax.experimental.pallas import tpu as pltpu

F32 = jnp.float32
BF16 = jnp.bfloat16

D_MODEL = 2048
A_HEADS = 8
A_HEAD_DIM = 64
A_V_DIM = 128
A_WIDTH = 1024
B_HEADS = 8
B_Q_LORA = 512
B_KV_LORA = 512
B_NOPE = 128
B_ROPE = 64
B_V_DIM = 128
B_WIDTH = 1024
ROPE_THETA = 10000.0
NORM_EPS = 1e-6
SUBLN_EPS = 1e-5
LAM_INIT = 0.8 - 0.6 * math.exp(-0.3 * 0)
LOG2E = math.log2(math.e)

_SIZES = [1024, 1024, 1024, 1024, 512, 512, 64, 1024, 4096]
_SPLITS = [int(v) for v in np.cumsum(_SIZES)[:-1]]

LANE = 128
BF16_ROWS = 16
COL_AQ, COL_AK, COL_AV, COL_AG = 0, 8, 16, 24
COL_GA, COL_GB, COL_BG = 32, 48, 64
COL_CQ, COL_CKV, COL_KR = 72, 76, 80
PROJ_COLS = 10752

TQ = 512
TK = 512
V_ROWS = A_V_DIM + BF16_ROWS
N_BIAS = 12
VMEM_LIMIT = 56 * 1024 * 1024
M_INIT = -1e30
SPEC_MARGIN = 60.0


def _dot_nt(a, b):
    return lax.dot_general(a, b, (((1,), (1,)), ((), ())), preferred_element_type=F32)


def _inproj_kernel(x_ref, g_ref, w_ref, o_ref, h_ref, *, tm):
    @pl.when(pl.program_id(1) == 0)
    def _():
        def body(r, c):
            rows = pl.ds(pl.multiple_of(r * 128, 128), 128)
            x = x_ref[rows, :]
            ms = jnp.mean(x * x, axis=-1, keepdims=True)
            h_ref[rows, :] = (x * lax.rsqrt(ms + NORM_EPS) * g_ref[...]).astype(BF16)
            return c
        lax.fori_loop(0, tm // 128, body, 0)

    o_ref[...] = jnp.dot(h_ref[...], w_ref[...], preferred_element_type=F32).astype(BF16)


def _inproj(x2, g, w, *, tm=1024, tn=512):
    m = x2.shape[0]
    return pl.pallas_call(
        functools.partial(_inproj_kernel, tm=tm),
        grid=(m // tm, PROJ_COLS // tn),
        in_specs=[
            pl.BlockSpec((tm, D_MODEL), lambda i, j: (i, 0)),
            pl.BlockSpec((1, D_MODEL), lambda i, j: (0, 0)),
            pl.BlockSpec((D_MODEL, tn), lambda i, j: (0, j)),
        ],
        out_specs=pl.BlockSpec((tm, tn), lambda i, j: (i, j)),
        out_shape=jax.ShapeDtypeStruct((m, PROJ_COLS), BF16),
        scratch_shapes=[pltpu.VMEM((tm, D_MODEL), BF16)],
        compiler_params=pltpu.CompilerParams(
            dimension_semantics=("arbitrary", "arbitrary"), vmem_limit_bytes=VMEM_LIMIT),
        name="inproj",
    )(x2, g, w)


def _softmax_update(s, s_max, vt, m_ref, acc_ref):
    m_old = m_ref[...]
    m_new = jnp.maximum(m_old, s_max)
    p = jnp.exp2(s - m_new)
    alpha = jnp.exp2(m_old - m_new)
    acc_ref[...] = alpha * acc_ref[...] + jnp.dot(vt, p.astype(BF16), preferred_element_type=F32)
    m_ref[...] = m_new


def _normalised(acc_ref):
    return acc_ref[0:A_V_DIM, :] / acc_ref[A_V_DIM:A_V_DIM + 1, :]


def _attn_a_kernel(mode_ref, fast_ref, cb_ref, q_ref, k_ref, v_ref, ag_ref, qbt_ref, kb_ref, cq_ref,
                   ck_ref, posq_ref, posk_ref, lam_ref, subg_ref, o_ref,
                   qa1, qa2, vt, m1, m2, acc1, acc2, flag_ref, *, tq, tk, nq, nk):
    b = pl.program_id(0)
    cb = cb_ref[pl.program_id(1)]

    ones_rows = (lax.broadcasted_iota(jnp.int32, (BF16_ROWS, tk), 0) == 0).astype(BF16)

    def tr(ki, c):
        ks = pl.ds(pl.multiple_of(ki * tk, tk), tk)
        vt[ki, 0:A_V_DIM, :] = v_ref[ks, :].astype(F32).T.astype(BF16)
        vt[ki, A_V_DIM:V_ROWS, :] = ones_rows
        return c
    lax.fori_loop(0, nk, tr, 0)

    def q_tile(qi, c):
        _attn_a_q_tile(qi, b, cb, mode_ref, fast_ref, q_ref, k_ref, ag_ref, qbt_ref, kb_ref, cq_ref,
                       ck_ref, posq_ref, posk_ref, lam_ref, subg_ref, o_ref,
                       qa1, qa2, vt, m1, m2, acc1, acc2, flag_ref, tq=tq, tk=tk, nq=nq, nk=nk)
        return c
    lax.fori_loop(0, nq, q_tile, 0)


def _attn_a_q_tile(qi, b, cb, mode_ref, fast_ref, q_ref, k_ref, ag_ref, qbt_ref, kb_ref, cq_ref,
                   ck_ref, posq_ref, posk_ref, lam_ref, subg_ref, o_ref,
                   qa1, qa2, vt, m1, m2, acc1, acc2, flag_ref, *, tq, tk, nq, nk):
    rows = pl.ds(pl.multiple_of(qi * tq, tq), tq)
    posq = posq_ref[qi]

    lo_r = lax.broadcasted_iota(jnp.int32, (LANE, 1), 0) < A_HEAD_DIM
    qs = q_ref[rows, :].astype(F32).T
    qbias = qbt_ref[qi].astype(F32) * cq_ref[...]
    qa1[...] = jnp.where(lo_r, qs, qbias).astype(BF16)
    qa2[...] = jnp.where(lo_r, qbias, qs).astype(BF16)
    lo_c = lax.broadcasted_iota(jnp.int32, (1, LANE), 1) < A_HEAD_DIM

    def raw_scores(ki, sgn):
        ks = pl.ds(pl.multiple_of(ki * tk, tk), tk)
        k = k_ref[ks, :]
        kbias = (kb_ref[ks, :].astype(F32) * (ck_ref[...] * sgn)).astype(BF16)
        s1 = jnp.dot(jnp.where(lo_c, k, kbias), qa1[...], preferred_element_type=F32)
        s2 = jnp.dot(jnp.where(lo_c, kbias, k), qa2[...], preferred_element_type=F32)
        return s1, s2

    def explicit_bias(ki):
        ks = pl.ds(pl.multiple_of(ki * tk, tk), tk)
        return -cb * jnp.abs(posk_ref[ks, :] - posq)

    def tile_scores(ki, general):
        if general:
            s1, s2 = raw_scores(ki, 0.0)
            bias = explicit_bias(ki)
            return s1 + bias, s2 + bias
        return raw_scores(ki, mode_ref[(b * nq + qi) * nk + ki].astype(F32))

    def col_max(s):
        return jnp.max(s, axis=0, keepdims=True)

    def pv(ki, s, shift):
        return jnp.dot(vt[ki], jnp.exp2(s - shift).astype(BF16), preferred_element_type=F32)

    fast = fast_ref[b * nq + qi]
    flag_ref[0] = 1 - fast

    @pl.when(fast == 1)
    def _():
        s1, s2 = tile_scores(qi, True)
        sh1, sh2 = col_max(s1), col_max(s2)
        top1, top2 = sh1, sh2
        acc1[...] = pv(qi, s1, sh1)
        acc2[...] = pv(qi, s2, sh2)
        for j in range(nk - 1):
            t = j + (j >= qi).astype(jnp.int32)
            s1, s2 = tile_scores(t, False)
            top1, top2 = jnp.maximum(top1, col_max(s1)), jnp.maximum(top2, col_max(s2))
            acc1[...] += pv(t, s1, sh1)
            acc2[...] += pv(t, s2, sh2)
        excess = jnp.max(jnp.maximum(top1 - sh1, top2 - sh2))
        flag_ref[0] = (excess > SPEC_MARGIN).astype(jnp.int32)

    @pl.when(flag_ref[0] == 1)
    def _():
        for m_ref in (m1, m2):
            m_ref[...] = jnp.full(m_ref.shape, M_INIT, F32)
        for z_ref in (acc1, acc2):
            z_ref[...] = jnp.zeros(z_ref.shape, F32)

        def update(ki, general):
            for s, m_ref, acc_ref in zip(tile_scores(ki, general), (m1, m2), (acc1, acc2)):
                _softmax_update(s, col_max(s), vt[ki], m_ref, acc_ref)

        def step(ki, carry):
            mode = mode_ref[(b * nq + qi) * nk + ki]

            @pl.when(mode == 0)
            def _():
                update(ki, True)

            @pl.when(mode != 0)
            def _():
                update(ki, False)

            return carry

        lax.fori_loop(0, nk, step, 0)

    lamv = lam_ref[...]
    lam = (jnp.exp(jnp.sum(lamv[0:1] * lamv[1:2], keepdims=True))
           - jnp.exp(jnp.sum(lamv[2:3] * lamv[3:4], keepdims=True)) + LAM_INIT)
    o = _normalised(acc1) - lam * _normalised(acc2)
    ms = jnp.mean(o * o, axis=0, keepdims=True)
    on = (o * lax.rsqrt(ms + SUBLN_EPS)).T * subg_ref[...] * (1.0 - LAM_INIT)
    ag = ag_ref[rows, :].astype(F32)
    o_ref[rows, :] = (on * (ag * jax.nn.sigmoid(ag))).astype(BF16)


def _attn_a(mode, fast, cb, proj3, qbias_t, kbias, cq, ck, posq_row, posk_col, lamv, subg, *, tq, tk):
    bsz, s, _ = proj3.shape
    nq, nk = s // tq, s // tk
    kern = functools.partial(_attn_a_kernel, tq=tq, tk=tk, nq=nq, nk=nk)
    stat = pltpu.VMEM((1, tq), F32)
    grid_spec = pltpu.PrefetchScalarGridSpec(
        num_scalar_prefetch=3,
        grid=(bsz, A_HEADS),
        in_specs=[
            pl.BlockSpec((None, s, LANE), lambda b, h, *_: (b, 0, COL_AQ + h)),
            pl.BlockSpec((None, s, LANE), lambda b, h, *_: (b, 0, COL_AK + h)),
            pl.BlockSpec((None, s, LANE), lambda b, h, *_: (b, 0, COL_AV + h)),
            pl.BlockSpec((None, s, LANE), lambda b, h, *_: (b, 0, COL_AG + h)),
            pl.BlockSpec((None, nq, LANE, tq), lambda b, h, *_: (b, 0, 0, 0)),
            pl.BlockSpec((None, s, LANE), lambda b, h, *_: (b, 0, 0)),
            pl.BlockSpec((None, LANE, 1), lambda b, h, *_: (h, 0, 0)),
            pl.BlockSpec((None, 1, LANE), lambda b, h, *_: (h, 0, 0)),
            pl.BlockSpec((None, nq, 1, tq), lambda b, h, *_: (b, 0, 0, 0)),
            pl.BlockSpec((None, s, 1), lambda b, h, *_: (b, 0, 0)),
            pl.BlockSpec((4, A_HEAD_DIM), lambda b, h, *_: (0, 0)),
            pl.BlockSpec((1, A_V_DIM), lambda b, h, *_: (0, 0)),
        ],
        out_specs=pl.BlockSpec((None, s, LANE), lambda b, h, *_: (b, 0, h)),
        scratch_shapes=[
            pltpu.VMEM((LANE, tq), BF16), pltpu.VMEM((LANE, tq), BF16),
            pltpu.VMEM((nk, V_ROWS, tk), BF16),
            stat, stat,
            pltpu.VMEM((V_ROWS, tq), F32), pltpu.VMEM((V_ROWS, tq), F32),
            pltpu.SMEM((1,), jnp.int32),
        ],
    )
    return pl.pallas_call(
        kern,
        grid_spec=grid_spec,
        out_shape=jax.ShapeDtypeStruct((bsz, s, A_WIDTH), BF16),
        compiler_params=pltpu.CompilerParams(
            dimension_semantics=("arbitrary", "arbitrary"), vmem_limit_bytes=VMEM_LIMIT),
        name="attn_a",
    )(mode, fast, cb, proj3, proj3, proj3, proj3, qbias_t, kbias, cq, ck, posq_row, posk_col, lamv,
      subg)


def _mla_prep_kernel(cq_ref, ckv_ref, kr_ref, posc_ref, posr_ref, invl_ref, invc_ref, qg_ref, kvg_ref,
                     wuqt_ref, wuk_ref, wuvt_ref, vones_ref, qo_ref, ko_ref, vo_ref):
    def norm(ref, g_ref):
        t = ref[...].astype(F32)
        ms = jnp.mean(t * t, axis=-1, keepdims=True)
        return (t * lax.rsqrt(ms + NORM_EPS) * g_ref[...]).astype(BF16)

    half = B_ROPE // 2
    cqn = norm(cq_ref, qg_ref)
    ckvn = norm(ckv_ref, kvg_ref)

    qt = _dot_nt(wuqt_ref[...], cqn)
    ang_t = invc_ref[...] * posr_ref[...]
    cs_t = jnp.cos(ang_t)
    sn_t = jnp.sin(ang_t)
    for h in range(B_HEADS):
        r0 = h * 2 * LANE
        qo_ref[r0:r0 + LANE, :] = qt[r0:r0 + LANE, :].astype(BF16)
        t1 = qt[r0 + LANE:r0 + LANE + half, :]
        t2 = qt[r0 + LANE + half:r0 + LANE + 2 * half, :]
        qo_ref[r0 + LANE:r0 + LANE + half, :] = (t1 * cs_t - t2 * sn_t).astype(BF16)
        qo_ref[r0 + LANE + half:r0 + LANE + 2 * half, :] = (t1 * sn_t + t2 * cs_t).astype(BF16)
        qo_ref[r0 + LANE + 2 * half:r0 + 2 * LANE, :] = jnp.zeros((LANE - 2 * half, qt.shape[1]), BF16)

    lane = lax.broadcasted_iota(jnp.int32, (1, LANE), 1)
    ang = posc_ref[...] * invl_ref[...]
    cs = jnp.cos(ang)
    sn = jnp.sin(ang)
    c_t = jnp.where(lane < B_ROPE, cs, 0.0)
    s1_t = jnp.where(lane < half, -sn, 0.0)
    s2_t = jnp.where((lane >= half) & (lane < B_ROPE), sn, 0.0)
    kr = kr_ref[...].astype(F32)
    krope = (kr * c_t + pltpu.roll(kr, LANE - half, axis=1) * s1_t
             + pltpu.roll(kr, half, axis=1) * s2_t).astype(BF16)
    kf = jnp.dot(ckvn, wuk_ref[...], preferred_element_type=F32)
    for h in range(B_HEADS):
        c0 = h * 2 * LANE
        ko_ref[:, c0:c0 + LANE] = kf[:, h * LANE:(h + 1) * LANE].astype(BF16)
        ko_ref[:, c0 + LANE:c0 + 2 * LANE] = krope

    vo_ref[...] = (_dot_nt(wuvt_ref[...], ckvn) + vones_ref[...]).astype(BF16)


def _mla_prep(proj2, pos_col, pos_row, inv_lane, inv_col, qg, kvg, wuqt, wuk, wuvt, vones, *, tm):
    m = proj2.shape[0]
    wq = B_HEADS * 2 * LANE
    wv = B_HEADS * V_ROWS
    return pl.pallas_call(
        _mla_prep_kernel,
        grid=(m // tm,),
        in_specs=[
            pl.BlockSpec((tm, B_Q_LORA), lambda i: (i, COL_CQ // 4)),
            pl.BlockSpec((tm, B_KV_LORA), lambda i: (i, COL_CKV // 4)),
            pl.BlockSpec((tm, LANE), lambda i: (i, COL_KR)),
            pl.BlockSpec((tm, 1), lambda i: (i, 0)),
            pl.BlockSpec((1, tm), lambda i: (0, i)),
            pl.BlockSpec((1, LANE), lambda i: (0, 0)),
            pl.BlockSpec((B_ROPE // 2, 1), lambda i: (0, 0)),
            pl.BlockSpec((1, B_Q_LORA), lambda i: (0, 0)),
            pl.BlockSpec((1, B_KV_LORA), lambda i: (0, 0)),
            pl.BlockSpec((wq, B_Q_LORA), lambda i: (0, 0)),
            pl.BlockSpec((B_KV_LORA, B_HEADS * B_NOPE), lambda i: (0, 0)),
            pl.BlockSpec((wv, B_KV_LORA), lambda i: (0, 0)),
            pl.BlockSpec((wv, 1), lambda i: (0, 0)),
        ],
        out_specs=[
            pl.BlockSpec((None, wq, tm), lambda i: (i, 0, 0)),
            pl.BlockSpec((tm, wq), lambda i: (i, 0)),
            pl.BlockSpec((None, wv, tm), lambda i: (i, 0, 0)),
        ],
        out_shape=[
            jax.ShapeDtypeStruct((m // tm, wq, tm), BF16),
            jax.ShapeDtypeStruct((m, wq), BF16),
            jax.ShapeDtypeStruct((m // tm, wv, tm), BF16),
        ],
        compiler_params=pltpu.CompilerParams(
            dimension_semantics=("arbitrary",), vmem_limit_bytes=VMEM_LIMIT),
        name="mla_prep",
    )(proj2, proj2, proj2, pos_col, pos_row, inv_lane, inv_col, qg, kvg, wuqt, wuk, wuvt, vones)


def _attn_b_kernel(qt_ref, k_ref, vt_ref, bg_ref, o_ref, m1, acc1, flag_ref, *, tq, tk, nq, nk):
    def q_tile(qi, c):
        _attn_b_q_tile(qi, qt_ref, k_ref, vt_ref, bg_ref, o_ref, m1, acc1, flag_ref,
                       tq=tq, tk=tk, nk=nk)
        return c
    lax.fori_loop(0, nq, q_tile, 0)


def _attn_b_q_tile(qi, qt_ref, k_ref, vt_ref, bg_ref, o_ref, m1, acc1, flag_ref, *, tq, tk, nk):
    rows = pl.ds(pl.multiple_of(qi * tq, tq), tq)

    def raw_scores(ki):
        ks = pl.ds(pl.multiple_of(ki * tk, tk), tk)
        return jnp.dot(k_ref[ks, :], qt_ref[qi], preferred_element_type=F32)

    s = raw_scores(0)
    shift = jnp.max(s, axis=0, keepdims=True)
    top = shift
    acc1[...] = jnp.dot(vt_ref[0], jnp.exp2(s - shift).astype(BF16), preferred_element_type=F32)
    for ki in range(1, nk):
        s = raw_scores(ki)
        top = jnp.maximum(top, jnp.max(s, axis=0, keepdims=True))
        acc1[...] += jnp.dot(vt_ref[ki], jnp.exp2(s - shift).astype(BF16),
                             preferred_element_type=F32)
    flag_ref[0] = (jnp.max(top - shift) > SPEC_MARGIN).astype(jnp.int32)

    @pl.when(flag_ref[0] == 1)
    def _():
        m1[...] = jnp.full(m1.shape, M_INIT, F32)
        acc1[...] = jnp.zeros(acc1.shape, F32)

        def step(ki, carry):
            s = raw_scores(ki)
            _softmax_update(s, jnp.max(s, axis=0, keepdims=True), vt_ref[ki], m1, acc1)
            return carry

        lax.fori_loop(0, nk, step, 0)

    bg = bg_ref[rows, :].astype(F32)
    o_ref[rows, :] = (_normalised(acc1).T * (bg * jax.nn.sigmoid(bg))).astype(BF16)


def _attn_b(qt3, kb3, vt3, proj3, *, tq, tk):
    bsz, s, _ = kb3.shape
    nq, nk = s // tq, s // tk
    return pl.pallas_call(
        functools.partial(_attn_b_kernel, tq=tq, tk=tk, nq=nq, nk=nk),
        grid=(bsz, B_HEADS),
        in_specs=[
            pl.BlockSpec((nq, 2 * LANE, tq), lambda b, h: (b, h, 0)),
            pl.BlockSpec((None, s, 2 * LANE), lambda b, h: (b, 0, h)),
            pl.BlockSpec((nk, V_ROWS, tk), lambda b, h: (b, h, 0)),
            pl.BlockSpec((None, s, LANE), lambda b, h: (b, 0, COL_BG + h)),
        ],
        out_specs=pl.BlockSpec((None, s, LANE), lambda b, h: (b, 0, h)),
        out_shape=jax.ShapeDtypeStruct((bsz, s, B_WIDTH), BF16),
        scratch_shapes=[
            pltpu.VMEM((1, tq), F32), pltpu.VMEM((V_ROWS, tq), F32), pltpu.SMEM((1,), jnp.int32),
        ],
        compiler_params=pltpu.CompilerParams(
            dimension_semantics=("arbitrary", "arbitrary"), vmem_limit_bytes=VMEM_LIMIT),
        name="attn_b",
    )(qt3, kb3, vt3, proj3)


def _post_kernel(za_ref, zb_ref, ga_ref, gb_ref, x_ref, woa_ref, wob_ref, wout_ref, fg_ref, o_ref):
    ya = jnp.dot(za_ref[...], woa_ref[...], preferred_element_type=F32)
    yb = jnp.dot(zb_ref[...], wob_ref[...], preferred_element_type=F32)
    merged = (jax.nn.sigmoid(ga_ref[...].astype(F32)) * ya
              + jax.nn.sigmoid(gb_ref[...].astype(F32)) * yb)
    y = x_ref[...] + jnp.dot(merged.astype(BF16), wout_ref[...], preferred_element_type=F32)
    ms = jnp.mean(y * y, axis=-1, keepdims=True)
    o_ref[...] = y * lax.rsqrt(ms + NORM_EPS) * fg_ref[...]


def _post(za, zb, proj2, x2, woa, wob, wout, fg, *, tm=256):
    m = x2.shape[0]
    const = lambda shape: pl.BlockSpec(shape, lambda i: (0, 0), pipeline_mode=pl.Buffered(1))
    return pl.pallas_call(
        _post_kernel,
        grid=(m // tm,),
        in_specs=[
            pl.BlockSpec((tm, A_WIDTH), lambda i: (i, 0)),
            pl.BlockSpec((tm, B_WIDTH), lambda i: (i, 0)),
            pl.BlockSpec((tm, D_MODEL), lambda i: (i, COL_GA // 16)),
            pl.BlockSpec((tm, D_MODEL), lambda i: (i, COL_GB // 16)),
            pl.BlockSpec((tm, D_MODEL), lambda i: (i, 0)),
            const((A_WIDTH, D_MODEL)),
            const((B_WIDTH, D_MODEL)),
            const((D_MODEL, D_MODEL)),
            const((1, D_MODEL)),
        ],
        out_specs=pl.BlockSpec((tm, D_MODEL), lambda i: (i, 0)),
        out_shape=jax.ShapeDtypeStruct((m, D_MODEL), F32),
        compiler_params=pltpu.CompilerParams(
            dimension_semantics=("arbitrary",), vmem_limit_bytes=VMEM_LIMIT),
        name="post",
    )(za, zb, proj2, proj2, x2, woa, wob, wout, fg)


W_TN = 512
HALF_LANE = LANE // 2


def _w_in_plan():
    off = dict(zip(("aq", "ak", "av", "ag", "cq", "ckv", "kr", "bg", "mg"),
                   [0] + _SPLITS))
    a_blk, b_blk, kind, scale = [], [], [], []
    for j in range(PROJ_COLS // W_TN):
        c0 = j * W_TN
        if c0 < COL_GA * LANE:
            src, k = off["aq"] + c0, 0
        elif c0 < COL_BG * LANE:
            src, k = off["mg"] + c0 - COL_GA * LANE, 1
        elif c0 < COL_CQ * LANE:
            src, k = off["bg"] + c0 - COL_BG * LANE, 1
        elif c0 < COL_KR * LANE:
            src, k = off["cq"] + c0 - COL_CQ * LANE, 0
        else:
            src, k = off["kr"], 2
        assert src % W_TN == (HALF_LANE if k == 1 else 0)
        a_blk.append(src // W_TN)
        b_blk.append((src // W_TN + 1) * (W_TN // LANE) if k == 1 else 0)
        kind.append(k)
        scale.append(A_HEAD_DIM ** -0.5 * LOG2E if c0 < COL_AK * LANE else 1.0)
    i32 = lambda v: jnp.asarray(v, jnp.int32)
    return i32(a_blk), i32(b_blk), i32(kind), jnp.asarray(scale, F32)


def _w_prep_kernel(ablk_ref, bblk_ref, kind_ref, scale_ref, a_ref, b_ref, o_ref):
    j = pl.program_id(0)
    kind = kind_ref[j]

    @pl.when(kind == 0)
    def _():
        o_ref[...] = (a_ref[...] * scale_ref[j]).astype(BF16)

    @pl.when(kind == 1)
    def _():
        shifted = pltpu.roll(a_ref[...], W_TN - HALF_LANE, axis=1)
        tail = pltpu.roll(b_ref[...], HALF_LANE, axis=1)
        lane = lax.broadcasted_iota(jnp.int32, (1, LANE), 1)
        o_ref[:, 0:W_TN - LANE] = shifted[:, 0:W_TN - LANE].astype(BF16)
        o_ref[:, W_TN - LANE:W_TN] = jnp.where(lane < HALF_LANE, shifted[:, W_TN - LANE:W_TN],
                                               tail).astype(BF16)

    @pl.when(kind == 2)
    def _():
        lane = lax.broadcasted_iota(jnp.int32, (1, W_TN), 1)
        o_ref[...] = jnp.where(lane < HALF_LANE, a_ref[...], 0.0).astype(BF16)


def _prep_w_in(w):
    rows = w.shape[0]
    grid_spec = pltpu.PrefetchScalarGridSpec(
        num_scalar_prefetch=4,
        grid=(PROJ_COLS // W_TN,),
        in_specs=[
            pl.BlockSpec((rows, W_TN), lambda j, a, b, k, s: (0, a[j])),
            pl.BlockSpec((rows, LANE), lambda j, a, b, k, s: (0, b[j])),
        ],
        out_specs=pl.BlockSpec((rows, W_TN), lambda j, a, b, k, s: (0, j)),
    )
    return pl.pallas_call(
        _w_prep_kernel,
        grid_spec=grid_spec,
        out_shape=jax.ShapeDtypeStruct((rows, PROJ_COLS), BF16),
        compiler_params=pltpu.CompilerParams(
            dimension_semantics=("arbitrary",), vmem_limit_bytes=VMEM_LIMIT),
        name="w_prep",
    )(*_w_in_plan(), w, w)


def _prep_w_uq_t(w):
    w = w.reshape(B_Q_LORA, B_HEADS, B_NOPE + B_ROPE) * ((B_NOPE + B_ROPE) ** -0.5 * LOG2E)
    w = jnp.pad(w, ((0, 0), (0, 0), (0, 2 * LANE - (B_NOPE + B_ROPE))))
    return w.reshape(B_Q_LORA, B_HEADS * 2 * LANE).T.astype(BF16)


def _prep_w_uv_t(w_ukv):
    w = w_ukv.reshape(B_KV_LORA, B_HEADS, B_NOPE + B_V_DIM)[:, :, B_NOPE:]
    w = jnp.pad(w, ((0, 0), (0, 0), (0, V_ROWS - B_V_DIM)))
    return w.reshape(B_KV_LORA, B_HEADS * V_ROWS).T.astype(BF16)


def _split3_bf16(c):
    c1 = c.astype(BF16).astype(F32)
    c2 = (c - c1).astype(BF16).astype(F32)
    c3 = (c - c1 - c2).astype(BF16).astype(F32)
    return c1, c2, c3


def _alibi_operands(positions, slopes, tq, tk):
    bsz, s = positions.shape
    nq, nk = s // tq, s // tk
    pmin = jnp.min(positions, axis=1, keepdims=True)
    rel = positions - pmin
    ok = ((jnp.min(rel, axis=1) >= 0) & (jnp.max(rel, axis=1) < 65536)
          & (jnp.max(jnp.abs(positions), axis=1) < (1 << 24)))
    qmin = jnp.min(positions.reshape(bsz, nq, tq), axis=-1)
    qmax = jnp.max(positions.reshape(bsz, nq, tq), axis=-1)
    kmin = jnp.min(positions.reshape(bsz, nk, tk), axis=-1)
    kmax = jnp.max(positions.reshape(bsz, nk, tk), axis=-1)
    after = qmin[:, :, None] >= kmax[:, None, :]
    before = qmax[:, :, None] <= kmin[:, None, :]
    mode = jnp.where(after, 1, jnp.where(before, -1, 0)) * ok[:, None, None].astype(jnp.int32)
    off_diag = ~jnp.eye(nq, nk, dtype=bool)[None]
    fast = jnp.all((mode != 0) | ~off_diag, axis=-1) & (nq == nk)

    rel = jnp.where(ok[:, None], rel, 0)
    hi = (rel >> 8).astype(F32)
    lo = (rel & 255).astype(F32)

    g = np.arange(LANE) % A_HEAD_DIM
    first, second = g < N_BIAS // 2, (g >= N_BIAS // 2) & (g < N_BIAS)
    even, odd = g % 2 == 0, g % 2 == 1
    f32 = lambda mask: jnp.asarray(mask, F32)
    q_hi, q_lo, q_one = f32(first & even), f32(first & odd), f32(second)
    k_hi, k_lo, k_one = f32(second & even), f32(second & odd), f32(first)
    hi_q, lo_q = hi.reshape(bsz, nq, 1, tq), lo.reshape(bsz, nq, 1, tq)
    col = lambda v: v[None, None, :, None]
    qb_t = (hi_q * col(q_hi) + lo_q * col(q_lo) + col(q_one)).astype(BF16)
    kb = (hi[:, :, None] * k_hi[None, None, :] + lo[:, :, None] * k_lo[None, None, :]
          + k_one[None, None, :]).astype(BF16)

    cb = slopes * LOG2E
    pieces = _split3_bf16(cb)
    cpair = jnp.stack([v for c in pieces for v in (256.0 * c, c)], axis=-1)
    sel = np.zeros((N_BIAS // 2, LANE), np.float32)
    for i in range(N_BIAS // 2):
        sel[i, (g % (N_BIAS // 2) == i) & (g < N_BIAS)] = 1.0
    spread = jnp.dot(cpair, jnp.asarray(sel), precision=lax.Precision.HIGHEST)
    cq = (spread * f32(second) + f32(first))[:, :, None]
    ck = (-spread * f32(first) + f32(second))[:, None, :]
    return (mode.reshape(-1).astype(jnp.int32), fast.reshape(-1).astype(jnp.int32), cb.astype(F32),
            qb_t, kb, cq, ck)


def kernel(x, positions, norm_g, w_in, lam_q1, lam_k1, lam_q2, lam_k2, a_subln_g, w_oa, q_norm_g,
           w_uq, kv_norm_g, w_ukv, w_ob, w_out, final_g):
    bsz, s, d = x.shape
    m = bsz * s
    assert norm_g.shape[0] == 1 and d == D_MODEL and s % TQ == 0 and s % TK == 0
    layer = 0
    x2 = x.reshape(m, d)
    pos_f = positions.astype(F32)

    proj2 = _inproj(x2, norm_g[layer][None, :], _prep_w_in(w_in[layer]))
    proj3 = proj2.reshape(bsz, s, PROJ_COLS)

    slopes = jnp.asarray([2.0 ** (-8.0 * (h + 1) / A_HEADS) for h in range(A_HEADS)], dtype=F32)
    mode, fast, cb, qb_t, kb, cq, ck = _alibi_operands(positions, slopes, TQ, TK)
    lamv = jnp.stack([lam_q1[layer], lam_k1[layer], lam_q2[layer], lam_k2[layer]]).astype(F32)
    za = _attn_a(mode, fast, cb, proj3, qb_t, kb, cq, ck, pos_f.reshape(bsz, s // TQ, 1, TQ),
                 pos_f[:, :, None], lamv,
                 a_subln_g[layer][None, :], tq=TQ, tk=TK)

    half = B_ROPE // 2
    inv = ROPE_THETA ** (-jnp.arange(half, dtype=F32) / half)
    inv_lane = jnp.concatenate([inv, inv, jnp.zeros((LANE - B_ROPE,), F32)])[None, :]
    wuk = (w_ukv[layer].reshape(B_KV_LORA, B_HEADS, B_NOPE + B_V_DIM)[:, :, :B_NOPE]
           .reshape(B_KV_LORA, B_HEADS * B_NOPE).astype(BF16))
    vones = (jnp.arange(B_HEADS * V_ROWS) % V_ROWS == B_V_DIM).astype(F32)[:, None]
    qt3, kb2, vt3 = _mla_prep(proj2, pos_f.reshape(m, 1), pos_f.reshape(1, m), inv_lane, inv[:, None],
                              q_norm_g[layer][None, :], kv_norm_g[layer][None, :],
                              _prep_w_uq_t(w_uq[layer]), wuk, _prep_w_uv_t(w_ukv[layer]), vones,
                              tm=TK)
    zb = _attn_b(qt3, kb2.reshape(bsz, s, -1), vt3, proj3, tq=TQ, tk=TK)

    out = _post(za.reshape(m, A_WIDTH), zb.reshape(m, B_WIDTH), proj2, x2, w_oa[layer].astype(BF16),
                w_ob[layer].astype(BF16), w_out[layer].astype(BF16), final_g[None, :])
    return out.reshape(bsz, s, d)
```

```python
import functools
import math

import numpy as np
import jax
import jax.numpy as jnp
from jax import lax
from jax.experimental import pallas as pl
from jax.experimental.pallas import tpu as pltpu

F32 = jnp.float32
BF16 = jnp.bfloat16

D_MODEL = 2048
A_HEADS = 8
A_HEAD_DIM = 64
A_V_DIM = 128
A_WIDTH = 1024
B_HEADS = 8
B_Q_LORA = 512
B_KV_LORA = 512
B_NOPE = 128
B_ROPE = 64
B_V_DIM = 128
B_WIDTH = 1024
ROPE_THETA = 10000.0
NORM_EPS = 1e-6
SUBLN_EPS = 1e-5
LAM_INIT = 0.8 - 0.6 * math.exp(-0.3 * 0)
LOG2E = math.log2(math.e)

_SIZES = [1024, 1024, 1024, 1024, 512, 512, 64, 1024, 4096]
_SPLITS = [int(v) for v in np.cumsum(_SIZES)[:-1]]

LANE = 128
BF16_ROWS = 16
COL_AQ, COL_AK, COL_AV, COL_AG = 0, 8, 16, 24
COL_GA, COL_GB, COL_BG = 32, 48, 64
COL_CQ, COL_CKV, COL_KR = 72, 76, 80
PROJ_COLS = 10752

TQ = 512
TK = 512
V_ROWS = A_V_DIM + BF16_ROWS
N_BIAS = 12
VMEM_LIMIT = 56 * 1024 * 1024
M_INIT = -1e30
SPEC_MARGIN = 60.0
SKIP_LOG2 = 160.0
SKIP_SLACK = 1.01


def _dot_nt(a, b):
    return lax.dot_general(a, b, (((1,), (1,)), ((), ())), preferred_element_type=F32)


def _inproj_kernel(x_ref, g_ref, w_ref, o_ref, h_ref, *, tm):
    @pl.when(pl.program_id(1) == 0)
    def _():
        def body(r, c):
            rows = pl.ds(pl.multiple_of(r * 128, 128), 128)
            x = x_ref[rows, :]
            ms = jnp.mean(x * x, axis=-1, keepdims=True)
            h_ref[rows, :] = (x * lax.rsqrt(ms + NORM_EPS) * g_ref[...]).astype(BF16)
            return c
        lax.fori_loop(0, tm // 128, body, 0)

    o_ref[...] = jnp.dot(h_ref[...], w_ref[...], preferred_element_type=F32).astype(BF16)


def _inproj(x2, g, w, *, tm=1024, tn=512):
    m = x2.shape[0]
    return pl.pallas_call(
        functools.partial(_inproj_kernel, tm=tm),
        grid=(m // tm, PROJ_COLS // tn),
        in_specs=[
            pl.BlockSpec((tm, D_MODEL), lambda i, j: (i, 0)),
            pl.BlockSpec((1, D_MODEL), lambda i, j: (0, 0)),
            pl.BlockSpec((D_MODEL, tn), lambda i, j: (0, j)),
        ],
        out_specs=pl.BlockSpec((tm, tn), lambda i, j: (i, j)),
        out_shape=jax.ShapeDtypeStruct((m, PROJ_COLS), BF16),
        scratch_shapes=[pltpu.VMEM((tm, D_MODEL), BF16)],
        compiler_params=pltpu.CompilerParams(
            dimension_semantics=("arbitrary", "arbitrary"), vmem_limit_bytes=VMEM_LIMIT),
        name="inproj",
    )(x2, g, w)


def _softmax_update(s, s_max, vt, m_ref, acc_ref):
    m_old = m_ref[...]
    m_new = jnp.maximum(m_old, s_max)
    p = jnp.exp2(s - m_new)
    alpha = jnp.exp2(m_old - m_new)
    acc_ref[...] = alpha * acc_ref[...] + jnp.dot(vt, p.astype(BF16), preferred_element_type=F32)
    m_ref[...] = m_new


def _normalised(acc_ref):
    return acc_ref[0:A_V_DIM, :] / acc_ref[A_V_DIM:A_V_DIM + 1, :]


def _scalar(v):
    return jnp.max(v)


def _attn_a_kernel(mode_ref, fast_ref, cb_ref, dmin_ref, q_ref, k_ref, v_ref, ag_ref, qbt_ref, kb_ref,
                   cq_ref, ck_ref, posq_ref, posk_ref, lam_ref, subg_ref, o_ref,
                   qa1, qa2, vt, m1, m2, acc1, acc2, flag_ref, kn_ref, list_ref, *, tq, tk, nq, nk):
    b = pl.program_id(0)
    cb = cb_ref[pl.program_id(1)]

    ones_rows = (lax.broadcasted_iota(jnp.int32, (BF16_ROWS, tk), 0) == 0).astype(BF16)
    lo_c = lax.broadcasted_iota(jnp.int32, (1, LANE), 1) < A_HEAD_DIM

    def tr(ki, c):
        ks = pl.ds(pl.multiple_of(ki * tk, tk), tk)
        vt[ki, 0:A_V_DIM, :] = v_ref[ks, :].astype(F32).T.astype(BF16)
        vt[ki, A_V_DIM:V_ROWS, :] = ones_rows
        k = k_ref[ks, :].astype(F32)
        sq = k * k
        n1 = jnp.max(jnp.sum(jnp.where(lo_c, sq, 0.0), axis=1, keepdims=True), axis=0, keepdims=True)
        n2 = jnp.max(jnp.sum(jnp.where(lo_c, 0.0, sq), axis=1, keepdims=True), axis=0, keepdims=True)
        kn_ref[2 * ki] = _scalar(jnp.sqrt(n1))
        kn_ref[2 * ki + 1] = _scalar(jnp.sqrt(n2))
        return c
    lax.fori_loop(0, nk, tr, 0)

    def q_tile(qi, c):
        _attn_a_q_tile(qi, b, cb, mode_ref, fast_ref, dmin_ref, q_ref, k_ref, ag_ref, qbt_ref, kb_ref,
                       cq_ref, ck_ref, posq_ref, posk_ref, lam_ref, subg_ref, o_ref,
                       qa1, qa2, vt, m1, m2, acc1, acc2, flag_ref, kn_ref, list_ref,
                       tq=tq, tk=tk, nq=nq, nk=nk)
        return c
    lax.fori_loop(0, nq, q_tile, 0)


def _attn_a_q_tile(qi, b, cb, mode_ref, fast_ref, dmin_ref, q_ref, k_ref, ag_ref, qbt_ref, kb_ref,
                   cq_ref, ck_ref, posq_ref, posk_ref, lam_ref, subg_ref, o_ref,
                   qa1, qa2, vt, m1, m2, acc1, acc2, flag_ref, kn_ref, list_ref, *, tq, tk, nq, nk):
    rows = pl.ds(pl.multiple_of(qi * tq, tq), tq)
    posq = posq_ref[qi]

    lo_r = lax.broadcasted_iota(jnp.int32, (LANE, 1), 0) < A_HEAD_DIM
    qs = q_ref[rows, :].astype(F32).T
    qbias = qbt_ref[qi].astype(F32) * cq_ref[...]
    qa1[...] = jnp.where(lo_r, qs, qbias).astype(BF16)
    qa2[...] = jnp.where(lo_r, qbias, qs).astype(BF16)
    lo_c = lax.broadcasted_iota(jnp.int32, (1, LANE), 1) < A_HEAD_DIM
    qsq = qs * qs
    qn1 = _scalar(jnp.sqrt(jnp.sum(jnp.where(lo_r, qsq, 0.0), axis=0, keepdims=True)))
    qn2 = _scalar(jnp.sqrt(jnp.sum(jnp.where(lo_r, 0.0, qsq), axis=0, keepdims=True)))

    def raw_scores(ki, sgn):
        ks = pl.ds(pl.multiple_of(ki * tk, tk), tk)
        k = k_ref[ks, :]
        kbias = (kb_ref[ks, :].astype(F32) * (ck_ref[...] * sgn)).astype(BF16)
        s1 = jnp.dot(jnp.where(lo_c, k, kbias), qa1[...], preferred_element_type=F32)
        s2 = jnp.dot(jnp.where(lo_c, kbias, k), qa2[...], preferred_element_type=F32)
        return s1, s2

    def explicit_bias(ki):
        ks = pl.ds(pl.multiple_of(ki * tk, tk), tk)
        return -cb * jnp.abs(posk_ref[ks, :] - posq)

    def tile_scores(ki, general):
        if general:
            s1, s2 = raw_scores(ki, 0.0)
            bias = explicit_bias(ki)
            return s1 + bias, s2 + bias
        return raw_scores(ki, mode_ref[(b * nq + qi) * nk + ki].astype(F32))

    def col_max(s):
        return jnp.max(s, axis=0, keepdims=True)

    def pv(ki, s, shift):
        return jnp.dot(vt[ki], jnp.exp2(s - shift).astype(BF16), preferred_element_type=F32)

    fast = fast_ref[b * nq + qi]
    flag_ref[0] = 1 - fast

    @pl.when(fast == 1)
    def _():
        s1, s2 = tile_scores(qi, True)
        sh1, sh2 = col_max(s1), col_max(s2)
        m1[...] = sh1
        m2[...] = sh2
        acc1[...] = pv(qi, s1, sh1)
        acc2[...] = pv(qi, s2, sh2)

        def add_tile(t, off):
            s1, s2 = tile_scores(t, False)
            m1[...] = jnp.maximum(m1[...], col_max(s1) - off)
            m2[...] = jnp.maximum(m2[...], col_max(s2) - off)
            acc1[...] += pv(t, s1, sh1 + off)
            acc2[...] += pv(t, s2, sh2 + off)

        lo1, lo2 = -_scalar(-sh1), -_scalar(-sh2)
        n = jnp.int32(0)
        for t in range(nk):
            gap = cb * dmin_ref[(b * nq + qi) * nk + t]
            ub1 = SKIP_SLACK * qn1 * kn_ref[2 * t] + 1.0 - gap - lo1
            ub2 = SKIP_SLACK * qn2 * kn_ref[2 * t + 1] + 1.0 - gap - lo2
            keep = (qi != t) & ((ub1 > -SKIP_LOG2) | (ub2 > -SKIP_LOG2))
            list_ref[n] = jnp.int32(t)
            n = n + keep.astype(jnp.int32)
        list_ref[n] = qi

        short = n <= nk - 4

        @pl.when(short)
        def _():
            def pair(p, c):
                for e in range(2):
                    idx = 2 * p + e
                    add_tile(list_ref[idx], jnp.where(idx < n, 0.0, -M_INIT))
                return c
            lax.fori_loop(0, lax.shift_right_logical(n + 1, 1), pair, 0)

        @pl.when(jnp.logical_not(short))
        def _():
            for j in range(nk - 1):
                add_tile(j + (j >= qi).astype(jnp.int32), 0.0)

        excess = jnp.max(jnp.maximum(m1[...] - sh1, m2[...] - sh2))
        flag_ref[0] = (excess > SPEC_MARGIN).astype(jnp.int32)

    @pl.when(flag_ref[0] == 1)
    def _():
        for m_ref in (m1, m2):
            m_ref[...] = jnp.full(m_ref.shape, M_INIT, F32)
        for z_ref in (acc1, acc2):
            z_ref[...] = jnp.zeros(z_ref.shape, F32)

        def update(ki, general):
            for s, m_ref, acc_ref in zip(tile_scores(ki, general), (m1, m2), (acc1, acc2)):
                _softmax_update(s, col_max(s), vt[ki], m_ref, acc_ref)

        def step(ki, carry):
            mode = mode_ref[(b * nq + qi) * nk + ki]

            @pl.when(mode == 0)
            def _():
                update(ki, True)

            @pl.when(mode != 0)
            def _():
                update(ki, False)

            return carry

        lax.fori_loop(0, nk, step, 0)

    lamv = lam_ref[...]
    lam = (jnp.exp(jnp.sum(lamv[0:1] * lamv[1:2], keepdims=True))
           - jnp.exp(jnp.sum(lamv[2:3] * lamv[3:4], keepdims=True)) + LAM_INIT)
    o = _normalised(acc1) - lam * _normalised(acc2)
    ms = jnp.mean(o * o, axis=0, keepdims=True)
    on = (o * lax.rsqrt(ms + SUBLN_EPS)).T * subg_ref[...] * (1.0 - LAM_INIT)
    ag = ag_ref[rows, :].astype(F32)
    o_ref[rows, :] = (on * (ag * jax.nn.sigmoid(ag))).astype(BF16)


def _attn_a(mode, fast, cb, dmin, proj3, qbias_t, kbias, cq, ck, posq_row, posk_col, lamv, subg,
            *, tq, tk):
    bsz, s, _ = proj3.shape
    nq, nk = s // tq, s // tk
    kern = functools.partial(_attn_a_kernel, tq=tq, tk=tk, nq=nq, nk=nk)
    stat = pltpu.VMEM((1, tq), F32)
    grid_spec = pltpu.PrefetchScalarGridSpec(
        num_scalar_prefetch=4,
        grid=(bsz, A_HEADS),
        in_specs=[
            pl.BlockSpec((None, s, LANE), lambda b, h, *_: (b, 0, COL_AQ + h)),
            pl.BlockSpec((None, s, LANE), lambda b, h, *_: (b, 0, COL_AK + h)),
            pl.BlockSpec((None, s, LANE), lambda b, h, *_: (b, 0, COL_AV + h)),
            pl.BlockSpec((None, s, LANE), lambda b, h, *_: (b, 0, COL_AG + h)),
            pl.BlockSpec((None, nq, LANE, tq), lambda b, h, *_: (b, 0, 0, 0)),
            pl.BlockSpec((None, s, LANE), lambda b, h, *_: (b, 0, 0)),
            pl.BlockSpec((None, LANE, 1), lambda b, h, *_: (h, 0, 0)),
            pl.BlockSpec((None, 1, LANE), lambda b, h, *_: (h, 0, 0)),
            pl.BlockSpec((None, nq, 1, tq), lambda b, h, *_: (b, 0, 0, 0)),
            pl.BlockSpec((None, s, 1), lambda b, h, *_: (b, 0, 0)),
            pl.BlockSpec((4, A_HEAD_DIM), lambda b, h, *_: (0, 0)),
            pl.BlockSpec((1, A_V_DIM), lambda b, h, *_: (0, 0)),
        ],
        out_specs=pl.BlockSpec((None, s, LANE), lambda b, h, *_: (b, 0, h)),
        scratch_shapes=[
            pltpu.VMEM((LANE, tq), BF16), pltpu.VMEM((LANE, tq), BF16),
            pltpu.VMEM((nk, V_ROWS, tk), BF16),
            stat, stat,
            pltpu.VMEM((V_ROWS, tq), F32), pltpu.VMEM((V_ROWS, tq), F32),
            pltpu.SMEM((1,), jnp.int32), pltpu.SMEM((2 * nk,), F32), pltpu.SMEM((nk + 1,), jnp.int32),
        ],
    )
    return pl.pallas_call(
        kern,
        grid_spec=grid_spec,
        out_shape=jax.ShapeDtypeStruct((bsz, s, A_WIDTH), BF16),
        compiler_params=pltpu.CompilerParams(
            dimension_semantics=("arbitrary", "arbitrary"), vmem_limit_bytes=VMEM_LIMIT),
        name="attn_a",
    )(mode, fast, cb, dmin, proj3, proj3, proj3, proj3, qbias_t, kbias, cq, ck, posq_row, posk_col,
      lamv, subg)


def _mla_prep_kernel(cq_ref, ckv_ref, kr_ref, posc_ref, posr_ref, invl_ref, invc_ref, qg_ref, kvg_ref,
                     wuqt_ref, wuk_ref, wuvt_ref, vones_ref, qo_ref, ko_ref, vo_ref):
    def norm(ref, g_ref):
        t = ref[...].astype(F32)
        ms = jnp.mean(t * t, axis=-1, keepdims=True)
        return (t * lax.rsqrt(ms + NORM_EPS) * g_ref[...]).astype(BF16)

    half = B_ROPE // 2
    cqn = norm(cq_ref, qg_ref)
    ckvn = norm(ckv_ref, kvg_ref)

    qt = _dot_nt(wuqt_ref[...], cqn)
    ang_t = invc_ref[...] * posr_ref[...]
    cs_t = jnp.cos(ang_t)
    sn_t = jnp.sin(ang_t)
    for h in range(B_HEADS):
        r0 = h * 2 * LANE
        qo_ref[r0:r0 + LANE, :] = qt[r0:r0 + LANE, :].astype(BF16)
        t1 = qt[r0 + LANE:r0 + LANE + half, :]
        t2 = qt[r0 + LANE + half:r0 + LANE + 2 * half, :]
        qo_ref[r0 + LANE:r0 + LANE + half, :] = (t1 * cs_t - t2 * sn_t).astype(BF16)
        qo_ref[r0 + LANE + half:r0 + LANE + 2 * half, :] = (t1 * sn_t + t2 * cs_t).astype(BF16)
        qo_ref[r0 + LANE + 2 * half:r0 + 2 * LANE, :] = jnp.zeros((LANE - 2 * half, qt.shape[1]), BF16)

    lane = lax.broadcasted_iota(jnp.int32, (1, LANE), 1)
    ang = posc_ref[...] * invl_ref[...]
    cs = jnp.cos(ang)
    sn = jnp.sin(ang)
    c_t = jnp.where(lane < B_ROPE, cs, 0.0)
    s1_t = jnp.where(lane < half, -sn, 0.0)
    s2_t = jnp.where((lane >= half) & (lane < B_ROPE), sn, 0.0)
    kr = kr_ref[...].astype(F32)
    krope = (kr * c_t + pltpu.roll(kr, LANE - half, axis=1) * s1_t
             + pltpu.roll(kr, half, axis=1) * s2_t).astype(BF16)
    kf = jnp.dot(ckvn, wuk_ref[...], preferred_element_type=F32)
    for h in range(B_HEADS):
        c0 = h * 2 * LANE
        ko_ref[:, c0:c0 + LANE] = kf[:, h * LANE:(h + 1) * LANE].astype(BF16)
        ko_ref[:, c0 + LANE:c0 + 2 * LANE] = krope

    vo_ref[...] = (_dot_nt(wuvt_ref[...], ckvn) + vones_ref[...]).astype(BF16)


def _mla_prep(proj2, pos_col, pos_row, inv_lane, inv_col, qg, kvg, wuqt, wuk, wuvt, vones, *, tm):
    m = proj2.shape[0]
    wq = B_HEADS * 2 * LANE
    wv = B_HEADS * V_ROWS
    return pl.pallas_call(
        _mla_prep_kernel,
        grid=(m // tm,),
        in_specs=[
            pl.BlockSpec((tm, B_Q_LORA), lambda i: (i, COL_CQ // 4)),
            pl.BlockSpec((tm, B_KV_LORA), lambda i: (i, COL_CKV // 4)),
            pl.BlockSpec((tm, LANE), lambda i: (i, COL_KR)),
            pl.BlockSpec((tm, 1), lambda i: (i, 0)),
            pl.BlockSpec((1, tm), lambda i: (0, i)),
            pl.BlockSpec((1, LANE), lambda i: (0, 0)),
            pl.BlockSpec((B_ROPE // 2, 1), lambda i: (0, 0)),
            pl.BlockSpec((1, B_Q_LORA), lambda i: (0, 0)),
            pl.BlockSpec((1, B_KV_LORA), lambda i: (0, 0)),
            pl.BlockSpec((wq, B_Q_LORA), lambda i: (0, 0)),
            pl.BlockSpec((B_KV_LORA, B_HEADS * B_NOPE), lambda i: (0, 0)),
            pl.BlockSpec((wv, B_KV_LORA), lambda i: (0, 0)),
            pl.BlockSpec((wv, 1), lambda i: (0, 0)),
        ],
        out_specs=[
            pl.BlockSpec((None, wq, tm), lambda i: (i, 0, 0)),
            pl.BlockSpec((tm, wq), lambda i: (i, 0)),
            pl.BlockSpec((None, wv, tm), lambda i: (i, 0, 0)),
        ],
        out_shape=[
            jax.ShapeDtypeStruct((m // tm, wq, tm), BF16),
            jax.ShapeDtypeStruct((m, wq), BF16),
            jax.ShapeDtypeStruct((m // tm, wv, tm), BF16),
        ],
        compiler_params=pltpu.CompilerParams(
            dimension_semantics=("arbitrary",), vmem_limit_bytes=VMEM_LIMIT),
        name="mla_prep",
    )(proj2, proj2, proj2, pos_col, pos_row, inv_lane, inv_col, qg, kvg, wuqt, wuk, wuvt, vones)


def _attn_b_kernel(qt_ref, k_ref, vt_ref, bg_ref, o_ref, m1, acc1, flag_ref, *, tq, tk, nq, nk):
    def q_tile(qi, c):
        _attn_b_q_tile(qi, qt_ref, k_ref, vt_ref, bg_ref, o_ref, m1, acc1, flag_ref,
                       tq=tq, tk=tk, nk=nk)
        return c
    lax.fori_loop(0, nq, q_tile, 0)


def _attn_b_q_tile(qi, qt_ref, k_ref, vt_ref, bg_ref, o_ref, m1, acc1, flag_ref, *, tq, tk, nk):
    rows = pl.ds(pl.multiple_of(qi * tq, tq), tq)

    def raw_scores(ki):
        ks = pl.ds(pl.multiple_of(ki * tk, tk), tk)
        return jnp.dot(k_ref[ks, :], qt_ref[qi], preferred_element_type=F32)

    s = raw_scores(0)
    shift = jnp.max(s, axis=0, keepdims=True)
    top = shift
    acc1[...] = jnp.dot(vt_ref[0], jnp.exp2(s - shift).astype(BF16), preferred_element_type=F32)
    for ki in range(1, nk):
        s = raw_scores(ki)
        top = jnp.maximum(top, jnp.max(s, axis=0, keepdims=True))
        acc1[...] += jnp.dot(vt_ref[ki], jnp.exp2(s - shift).astype(BF16),
                             preferred_element_type=F32)
    flag_ref[0] = (jnp.max(top - shift) > SPEC_MARGIN).astype(jnp.int32)

    @pl.when(flag_ref[0] == 1)
    def _():
        m1[...] = jnp.full(m1.shape, M_INIT, F32)
        acc1[...] = jnp.zeros(acc1.shape, F32)

        def step(ki, carry):
            s = raw_scores(ki)
            _softmax_update(s, jnp.max(s, axis=0, keepdims=True), vt_ref[ki], m1, acc1)
            return carry

        lax.fori_loop(0, nk, step, 0)

    bg = bg_ref[rows, :].astype(F32)
    o_ref[rows, :] = (_normalised(acc1).T * (bg * jax.nn.sigmoid(bg))).astype(BF16)


def _attn_b(qt3, kb3, vt3, proj3, *, tq, tk):
    bsz, s, _ = kb3.shape
    nq, nk = s // tq, s // tk
    return pl.pallas_call(
        functools.partial(_attn_b_kernel, tq=tq, tk=tk, nq=nq, nk=nk),
        grid=(bsz, B_HEADS),
        in_specs=[
            pl.BlockSpec((nq, 2 * LANE, tq), lambda b, h: (b, h, 0)),
            pl.BlockSpec((None, s, 2 * LANE), lambda b, h: (b, 0, h)),
            pl.BlockSpec((nk, V_ROWS, tk), lambda b, h: (b, h, 0)),
            pl.BlockSpec((None, s, LANE), lambda b, h: (b, 0, COL_BG + h)),
        ],
        out_specs=pl.BlockSpec((None, s, LANE), lambda b, h: (b, 0, h)),
        out_shape=jax.ShapeDtypeStruct((bsz, s, B_WIDTH), BF16),
        scratch_shapes=[
            pltpu.VMEM((1, tq), F32), pltpu.VMEM((V_ROWS, tq), F32), pltpu.SMEM((1,), jnp.int32),
        ],
        compiler_params=pltpu.CompilerParams(
            dimension_semantics=("arbitrary", "arbitrary"), vmem_limit_bytes=VMEM_LIMIT),
        name="attn_b",
    )(qt3, kb3, vt3, proj3)


def _post_kernel(za_ref, zb_ref, ga_ref, gb_ref, x_ref, woa_ref, wob_ref, wout_ref, fg_ref, o_ref):
    ya = jnp.dot(za_ref[...], woa_ref[...], preferred_element_type=F32)
    yb = jnp.dot(zb_ref[...], wob_ref[...], preferred_element_type=F32)
    merged = (jax.nn.sigmoid(ga_ref[...].astype(F32)) * ya
              + jax.nn.sigmoid(gb_ref[...].astype(F32)) * yb)
    y = x_ref[...] + jnp.dot(merged.astype(BF16), wout_ref[...], preferred_element_type=F32)
    ms = jnp.mean(y * y, axis=-1, keepdims=True)
    o_ref[...] = y * lax.rsqrt(ms + NORM_EPS) * fg_ref[...]


def _post(za, zb, proj2, x2, woa, wob, wout, fg, *, tm=256):
    m = x2.shape[0]
    const = lambda shape: pl.BlockSpec(shape, lambda i: (0, 0), pipeline_mode=pl.Buffered(1))
    return pl.pallas_call(
        _post_kernel,
        grid=(m // tm,),
        in_specs=[
            pl.BlockSpec((tm, A_WIDTH), lambda i: (i, 0)),
            pl.BlockSpec((tm, B_WIDTH), lambda i: (i, 0)),
            pl.BlockSpec((tm, D_MODEL), lambda i: (i, COL_GA // 16)),
            pl.BlockSpec((tm, D_MODEL), lambda i: (i, COL_GB // 16)),
            pl.BlockSpec((tm, D_MODEL), lambda i: (i, 0)),
            const((A_WIDTH, D_MODEL)),
            const((B_WIDTH, D_MODEL)),
            const((D_MODEL, D_MODEL)),
            const((1, D_MODEL)),
        ],
        out_specs=pl.BlockSpec((tm, D_MODEL), lambda i: (i, 0)),
        out_shape=jax.ShapeDtypeStruct((m, D_MODEL), F32),
        compiler_params=pltpu.CompilerParams(
            dimension_semantics=("arbitrary",), vmem_limit_bytes=VMEM_LIMIT),
        name="post",
    )(za, zb, proj2, proj2, x2, woa, wob, wout, fg)


W_TN = 512
KR_COLS = _SIZES[6]


def _w_in_plan():
    off = dict(zip(("aq", "ak", "av", "ag", "cq", "ckv", "kr", "bg", "mg"),
                   [0] + _SPLITS))
    a_blk, b_blk, kind, scale = [], [], [], []
    for j in range(PROJ_COLS // W_TN):
        c0 = j * W_TN
        if c0 < COL_GA * LANE:
            src, k = off["aq"] + c0, 0
        elif c0 < COL_BG * LANE:
            src, k = off["mg"] + c0 - COL_GA * LANE, 1
        elif c0 < COL_CQ * LANE:
            src, k = off["bg"] + c0 - COL_BG * LANE, 1
        elif c0 < COL_KR * LANE:
            src, k = off["cq"] + c0 - COL_CQ * LANE, 0
        else:
            src, k = off["kr"], 2
        assert src % W_TN == (KR_COLS if k == 1 else 0)
        a_blk.append(src // W_TN)
        b_blk.append((src // W_TN + 1) * (W_TN // KR_COLS) if k == 1 else 0)
        kind.append(k)
        scale.append(A_HEAD_DIM ** -0.5 * LOG2E if c0 < COL_AK * LANE else 1.0)
    i32 = lambda v: jnp.asarray(v, jnp.int32)
    return i32(a_blk), i32(b_blk), i32(kind), jnp.asarray(scale, F32)


def _w_prep_kernel(ablk_ref, bblk_ref, kind_ref, scale_ref, a_ref, b_ref, o_ref):
    j = pl.program_id(0)
    kind = kind_ref[j]

    @pl.when(kind == 0)
    def _():
        o_ref[...] = (a_ref[...] * scale_ref[j]).T.astype(BF16)

    @pl.when(kind == 1)
    def _():
        t = jnp.concatenate([a_ref[KR_COLS:W_TN, :], b_ref[...]], axis=0)
        o_ref[...] = t.T.astype(BF16)

    @pl.when(kind == 2)
    def _():
        row = lax.broadcasted_iota(jnp.int32, (W_TN, 1), 0)
        o_ref[...] = jnp.where(row < KR_COLS, a_ref[...], 0.0).T.astype(BF16)


def _prep_w_in(w_t):
    d = w_t.shape[1]
    grid_spec = pltpu.PrefetchScalarGridSpec(
        num_scalar_prefetch=4,
        grid=(PROJ_COLS // W_TN,),
        in_specs=[
            pl.BlockSpec((W_TN, d), lambda j, a, b, k, s: (a[j], 0)),
            pl.BlockSpec((KR_COLS, d), lambda j, a, b, k, s: (b[j], 0)),
        ],
        out_specs=pl.BlockSpec((d, W_TN), lambda j, a, b, k, s: (0, j)),
    )
    return pl.pallas_call(
        _w_prep_kernel,
        grid_spec=grid_spec,
        out_shape=jax.ShapeDtypeStruct((d, PROJ_COLS), BF16),
        compiler_params=pltpu.CompilerParams(
            dimension_semantics=("arbitrary",), vmem_limit_bytes=VMEM_LIMIT),
        name="w_prep",
    )(*_w_in_plan(), w_t, w_t)


def _prep_w_uq_t(w):
    w = w.reshape(B_Q_LORA, B_HEADS, B_NOPE + B_ROPE) * ((B_NOPE + B_ROPE) ** -0.5 * LOG2E)
    w = jnp.pad(w, ((0, 0), (0, 0), (0, 2 * LANE - (B_NOPE + B_ROPE))))
    return w.reshape(B_Q_LORA, B_HEADS * 2 * LANE).T.astype(BF16)


def _prep_w_uv_t(w_ukv):
    w = w_ukv.reshape(B_KV_LORA, B_HEADS, B_NOPE + B_V_DIM)[:, :, B_NOPE:]
    w = jnp.pad(w, ((0, 0), (0, 0), (0, V_ROWS - B_V_DIM)))
    return w.reshape(B_KV_LORA, B_HEADS * V_ROWS).T.astype(BF16)


def _split3_bf16(c):
    c1 = c.astype(BF16).astype(F32)
    c2 = (c - c1).astype(BF16).astype(F32)
    c3 = (c - c1 - c2).astype(BF16).astype(F32)
    return c1, c2, c3


def _alibi_operands(positions, slopes, tq, tk):
    bsz, s = positions.shape
    nq, nk = s // tq, s // tk
    pmin = jnp.min(positions, axis=1, keepdims=True)
    rel = positions - pmin
    ok = ((jnp.min(rel, axis=1) >= 0) & (jnp.max(rel, axis=1) < 65536)
          & (jnp.max(jnp.abs(positions), axis=1) < (1 << 24)))
    qmin = jnp.min(positions.reshape(bsz, nq, tq), axis=-1)
    qmax = jnp.max(positions.reshape(bsz, nq, tq), axis=-1)
    kmin = jnp.min(positions.reshape(bsz, nk, tk), axis=-1)
    kmax = jnp.max(positions.reshape(bsz, nk, tk), axis=-1)
    after = qmin[:, :, None] >= kmax[:, None, :]
    before = qmax[:, :, None] <= kmin[:, None, :]
    mode = jnp.where(after, 1, jnp.where(before, -1, 0)) * ok[:, None, None].astype(jnp.int32)
    gap = jnp.where(after, qmin[:, :, None] - kmax[:, None, :], kmin[:, None, :] - qmax[:, :, None])
    dmin = jnp.where(mode != 0, gap, 0).astype(F32)
    off_diag = ~jnp.eye(nq, nk, dtype=bool)[None]
    fast = jnp.all((mode != 0) | ~off_diag, axis=-1) & (nq == nk)

    rel = jnp.where(ok[:, None], rel, 0)
    hi = (rel >> 8).astype(F32)
    lo = (rel & 255).astype(F32)

    g = np.arange(LANE) % A_HEAD_DIM
    first, second = g < N_BIAS // 2, (g >= N_BIAS // 2) & (g < N_BIAS)
    even, odd = g % 2 == 0, g % 2 == 1
    f32 = lambda mask: jnp.asarray(mask, F32)
    q_hi, q_lo, q_one = f32(first & even), f32(first & odd), f32(second)
    k_hi, k_lo, k_one = f32(second & even), f32(second & odd), f32(first)
    hi_q, lo_q = hi.reshape(bsz, nq, 1, tq), lo.reshape(bsz, nq, 1, tq)
    col = lambda v: v[None, None, :, None]
    qb_t = (hi_q * col(q_hi) + lo_q * col(q_lo) + col(q_one)).astype(BF16)
    kb = (hi[:, :, None] * k_hi[None, None, :] + lo[:, :, None] * k_lo[None, None, :]
          + k_one[None, None, :]).astype(BF16)

    cb = slopes * LOG2E
    pieces = _split3_bf16(cb)
    cpair = jnp.stack([v for c in pieces for v in (256.0 * c, c)], axis=-1)
    sel = np.zeros((N_BIAS // 2, LANE), np.float32)
    for i in range(N_BIAS // 2):
        sel[i, (g % (N_BIAS // 2) == i) & (g < N_BIAS)] = 1.0
    spread = jnp.dot(cpair, jnp.asarray(sel), precision=lax.Precision.HIGHEST)
    cq = (spread * f32(second) + f32(first))[:, :, None]
    ck = (-spread * f32(first) + f32(second))[:, None, :]
    return (mode.reshape(-1).astype(jnp.int32), fast.reshape(-1).astype(jnp.int32), cb.astype(F32),
            dmin.reshape(-1),
            qb_t, kb, cq, ck)


def kernel(x, positions, norm_g, w_in, lam_q1, lam_k1, lam_q2, lam_k2, a_subln_g, w_oa, q_norm_g,
           w_uq, kv_norm_g, w_ukv, w_ob, w_out, final_g):
    bsz, s, d = x.shape
    m = bsz * s
    assert norm_g.shape[0] == 1 and d == D_MODEL and s % TQ == 0 and s % TK == 0
    layer = 0
    x2 = x.reshape(m, d)
    pos_f = positions.astype(F32)

    proj2 = _inproj(x2, norm_g[layer][None, :], _prep_w_in(w_in[layer].T))
    proj3 = proj2.reshape(bsz, s, PROJ_COLS)

    slopes = jnp.asarray([2.0 ** (-8.0 * (h + 1) / A_HEADS) for h in range(A_HEADS)], dtype=F32)
    mode, fast, cb, dmin, qb_t, kb, cq, ck = _alibi_operands(positions, slopes, TQ, TK)
    lamv = jnp.stack([lam_q1[layer], lam_k1[layer], lam_q2[layer], lam_k2[layer]]).astype(F32)
    za = _attn_a(mode, fast, cb, dmin, proj3, qb_t, kb, cq, ck, pos_f.reshape(bsz, s // TQ, 1, TQ),
                 pos_f[:, :, None], lamv,
                 a_subln_g[layer][None, :], tq=TQ, tk=TK)

    half = B_ROPE // 2
    inv = ROPE_THETA ** (-jnp.arange(half, dtype=F32) / half)
    inv_lane = jnp.concatenate([inv, inv, jnp.zeros((LANE - B_ROPE,), F32)])[None, :]
    wuk = (w_ukv[layer].reshape(B_KV_LORA, B_HEADS, B_NOPE + B_V_DIM)[:, :, :B_NOPE]
           .reshape(B_KV_LORA, B_HEADS * B_NOPE).astype(BF16))
    vones = (jnp.arange(B_HEADS * V_ROWS) % V_ROWS == B_V_DIM).astype(F32)[:, None]
    qt3, kb2, vt3 = _mla_prep(proj2, pos_f.reshape(m, 1), pos_f.reshape(1, m), inv_lane, inv[:, None],
                              q_norm_g[layer][None, :], kv_norm_g[layer][None, :],
                              _prep_w_uq_t(w_uq[layer]), wuk, _prep_w_uv_t(w_ukv[layer]), vones,
                              tm=TK)
    zb = _attn_b(qt3, kb2.reshape(bsz, s, -1), vt3, proj3, tq=TQ, tk=TK)

    out = _post(za.reshape(m, A_WIDTH), zb.reshape(m, B_WIDTH), proj2, x2, w_oa[layer].astype(BF16),
                w_ob[layer].astype(BF16), w_out[layer].astype(BF16), final_g[None, :])
    return out.reshape(bsz, s, d)
```

```python
import functools
import math

import numpy as np
import jax
import jax.numpy as jnp
from jax import lax
from jax.experimental import pallas as pl
from jax.experimental.pallas import tpu as pltpu

F32 = jnp.float32
BF16 = jnp.bfloat16

D_MODEL = 2048
A_HEADS = 8
A_HEAD_DIM = 64
A_V_DIM = 128
A_WIDTH = 1024
B_HEADS = 8
B_Q_LORA = 512
B_KV_LORA = 512
B_NOPE = 128
B_ROPE = 64
B_V_DIM = 128
B_WIDTH = 1024
ROPE_THETA = 10000.0
NORM_EPS = 1e-6
SUBLN_EPS = 1e-5
LAM_INIT = 0.8 - 0.6 * math.exp(-0.3 * 0)
LOG2E = math.log2(math.e)

_SIZES = [1024, 1024, 1024, 1024, 512, 512, 64, 1024, 4096]
_SPLITS = [int(v) for v in np.cumsum(_SIZES)[:-1]]

LANE = 128
BF16_ROWS = 16
COL_AQ, COL_AK, COL_AV, COL_AG = 0, 8, 16, 24
COL_GA, COL_GB, COL_BG = 32, 48, 64
COL_CQ, COL_CKV, COL_KR = 72, 76, 80
PROJ_COLS = 10752

TQ = 512
TK = 512
TK_B = 512
V_ROWS = A_V_DIM + BF16_ROWS
N_BIAS = 12
VMEM_LIMIT = 56 * 1024 * 1024
M_INIT = -1e30
SPEC_MARGIN = 60.0
SKIP_LOG2 = 160.0
SKIP_SLACK = 1.01


def _dot_nt(a, b):
    return lax.dot_general(a, b, (((1,), (1,)), ((), ())), preferred_element_type=F32)


def _inproj_kernel(x_ref, g_ref, w_ref, o_ref, h_ref, *, tm):
    @pl.when(pl.program_id(1) == 0)
    def _():
        def body(r, c):
            rows = pl.ds(pl.multiple_of(r * 128, 128), 128)
            x = x_ref[rows, :]
            ms = jnp.mean(x * x, axis=-1, keepdims=True)
            h_ref[rows, :] = (x * lax.rsqrt(ms + NORM_EPS) * g_ref[...]).astype(BF16)
            return c
        lax.fori_loop(0, tm // 128, body, 0)

    o_ref[...] = jnp.dot(h_ref[...], w_ref[...], preferred_element_type=F32).astype(BF16)


def _inproj(x2, g, w, *, tm=1024, tn=1536):
    m = x2.shape[0]
    return pl.pallas_call(
        functools.partial(_inproj_kernel, tm=tm),
        grid=(m // tm, PROJ_COLS // tn),
        in_specs=[
            pl.BlockSpec((tm, D_MODEL), lambda i, j: (i, 0)),
            pl.BlockSpec((1, D_MODEL), lambda i, j: (0, 0)),
            pl.BlockSpec((D_MODEL, tn), lambda i, j: (0, j)),
        ],
        out_specs=pl.BlockSpec((tm, tn), lambda i, j: (i, j)),
        out_shape=jax.ShapeDtypeStruct((m, PROJ_COLS), BF16),
        scratch_shapes=[pltpu.VMEM((tm, D_MODEL), BF16)],
        compiler_params=pltpu.CompilerParams(
            dimension_semantics=("arbitrary", "arbitrary"), vmem_limit_bytes=VMEM_LIMIT),
        name="inproj",
    )(x2, g, w)


def _softmax_update(s, s_max, vt, m_ref, acc_ref):
    m_old = m_ref[...]
    m_new = jnp.maximum(m_old, s_max)
    p = jnp.exp2(s - m_new)
    alpha = jnp.exp2(m_old - m_new)
    acc_ref[...] = alpha * acc_ref[...] + jnp.dot(vt, p.astype(BF16), preferred_element_type=F32)
    m_ref[...] = m_new


def _normalised(acc_ref):
    return acc_ref[0:A_V_DIM, :] / acc_ref[A_V_DIM:A_V_DIM + 1, :]


def _attn_a_kernel(mode_ref, fast_ref, cb_ref, q_ref, k_ref, v_ref, ag_ref, qbt_ref, kb_ref,
                   cq_ref, ck_ref, posq_ref, posk_ref, dmin_ref, lam_ref, subg_ref, o_ref,
                   qa1, qa2, vt, m1, m2, acc1, acc2, flag_ref, kn_ref, list_ref, *, tq, tk, nq, nk):
    b = pl.program_id(0)
    cb = cb_ref[pl.program_id(1)]

    ones_rows = (lax.broadcasted_iota(jnp.int32, (BF16_ROWS, tk), 0) == 0).astype(BF16)
    feat = lax.broadcasted_iota(jnp.int32, (LANE, LANE), 0)
    col = lax.broadcasted_iota(jnp.int32, (LANE, LANE), 1)
    half_sum = (col == jnp.where(feat < A_HEAD_DIM, 0, 1)).astype(BF16)
    lane = lax.broadcasted_iota(jnp.int32, (1, LANE), 1)
    kn = [jnp.zeros((1, LANE), F32), jnp.zeros((1, LANE), F32)]
    for ki in range(nk):
        vt[ki, 0:A_V_DIM, :] = v_ref[ki * tk:(ki + 1) * tk, :].astype(F32).T.astype(BF16)
        vt[ki, A_V_DIM:V_ROWS, :] = ones_rows
        k = k_ref[ki * tk:(ki + 1) * tk, :].astype(F32)
        nsq = jnp.dot((k * k).astype(BF16), half_sum, preferred_element_type=F32)
        nmax = jnp.sqrt(jnp.max(nsq, axis=0, keepdims=True))
        for m in range(2):
            norm_m = jnp.max(jnp.where(lane == m, nmax, 0.0), axis=1, keepdims=True)
            kn[m] = jnp.where(lane == ki, norm_m, kn[m])
    kn_ref[0:1, :] = kn[0]
    kn_ref[1:2, :] = kn[1]

    def q_tile(qi, c):
        _attn_a_q_tile(qi, b, cb, mode_ref, fast_ref, dmin_ref, q_ref, k_ref, ag_ref, qbt_ref, kb_ref,
                       cq_ref, ck_ref, posq_ref, posk_ref, lam_ref, subg_ref, o_ref,
                       qa1, qa2, vt, m1, m2, acc1, acc2, flag_ref, kn_ref, list_ref,
                       tq=tq, tk=tk, nq=nq, nk=nk)
        return c
    lax.fori_loop(0, nq, q_tile, 0)


def _attn_a_q_tile(qi, b, cb, mode_ref, fast_ref, dmin_ref, q_ref, k_ref, ag_ref, qbt_ref, kb_ref,
                   cq_ref, ck_ref, posq_ref, posk_ref, lam_ref, subg_ref, o_ref,
                   qa1, qa2, vt, m1, m2, acc1, acc2, flag_ref, kn_ref, list_ref, *, tq, tk, nq, nk):
    rows = pl.ds(pl.multiple_of(qi * tq, tq), tq)
    posq = posq_ref[qi]

    lo_r = lax.broadcasted_iota(jnp.int32, (LANE, 1), 0) < A_HEAD_DIM
    qs = q_ref[rows, :].astype(F32).T
    qbias = qbt_ref[qi].astype(F32) * cq_ref[...]
    qa1[...] = jnp.where(lo_r, qs, qbias).astype(BF16)
    qa2[...] = jnp.where(lo_r, qbias, qs).astype(BF16)
    lo_c = lax.broadcasted_iota(jnp.int32, (1, LANE), 1) < A_HEAD_DIM
    qsq = qs * qs
    qn1 = jnp.max(jnp.sqrt(jnp.sum(jnp.where(lo_r, qsq, 0.0), axis=0, keepdims=True)),
                  axis=1, keepdims=True)
    qn2 = jnp.max(jnp.sqrt(jnp.sum(jnp.where(lo_r, 0.0, qsq), axis=0, keepdims=True)),
                  axis=1, keepdims=True)

    def raw_scores(ki, sgn):
        ks = pl.ds(pl.multiple_of(ki * tk, tk), tk)
        k = k_ref[ks, :]
        kbias = (kb_ref[ks, :].astype(F32) * (ck_ref[...] * sgn)).astype(BF16)
        s1 = jnp.dot(jnp.where(lo_c, k, kbias), qa1[...], preferred_element_type=F32)
        s2 = jnp.dot(jnp.where(lo_c, kbias, k), qa2[...], preferred_element_type=F32)
        return s1, s2

    def explicit_bias(ki):
        ks = pl.ds(pl.multiple_of(ki * tk, tk), tk)
        return -cb * jnp.abs(posk_ref[ks, :] - posq)

    def tile_scores(ki, general):
        if general:
            s1, s2 = raw_scores(ki, 0.0)
            bias = explicit_bias(ki)
            return s1 + bias, s2 + bias
        return raw_scores(ki, mode_ref[(b * nq + qi) * nk + ki].astype(F32))

    def col_max(s):
        return jnp.max(s, axis=0, keepdims=True)

    def pv(ki, s, shift):
        return jnp.dot(vt[ki], jnp.exp2(s - shift).astype(BF16), preferred_element_type=F32)

    fast = fast_ref[b * nq + qi]
    flag_ref[0] = 1 - fast

    @pl.when(fast == 1)
    def _():
        s1, s2 = tile_scores(qi, True)
        sh1, sh2 = col_max(s1), col_max(s2)
        m1[...] = sh1
        m2[...] = sh2
        acc1[...] = pv(qi, s1, sh1)
        acc2[...] = pv(qi, s2, sh2)

        def add_tile(t, off):
            s1, s2 = tile_scores(t, False)
            m1[...] = jnp.maximum(m1[...], col_max(s1) - off)
            m2[...] = jnp.maximum(m2[...], col_max(s2) - off)
            acc1[...] += pv(t, s1, sh1 + off)
            acc2[...] += pv(t, s2, sh2 + off)

        lo1 = jnp.min(sh1, axis=1, keepdims=True)
        lo2 = jnp.min(sh2, axis=1, keepdims=True)
        gap = cb * dmin_ref[qi]
        ub1 = SKIP_SLACK * qn1 * kn_ref[0:1, :] + 1.0 - gap - lo1
        ub2 = SKIP_SLACK * qn2 * kn_ref[1:2, :] + 1.0 - gap - lo2
        lane = lax.broadcasted_iota(jnp.int32, (1, LANE), 1)
        keep = (lane != qi) & (lane < nk) & ((ub1 > -SKIP_LOG2) | (ub2 > -SKIP_LOG2))
        bits = jnp.sum(jnp.where(keep, jnp.exp2(lane.astype(F32)), 0.0)).astype(jnp.int32)
        n = jnp.int32(0)
        for t in range(nk):
            list_ref[n] = jnp.int32(t)
            n = n + (lax.shift_right_logical(bits, t) & 1)
        list_ref[n] = qi

        short = n <= nk - 4

        @pl.when(short)
        def _():
            def pair(p, c):
                for e in range(2):
                    idx = 2 * p + e
                    add_tile(list_ref[idx], jnp.where(idx < n, 0.0, -M_INIT))
                return c
            lax.fori_loop(0, lax.shift_right_logical(n + 1, 1), pair, 0)

        @pl.when(jnp.logical_not(short))
        def _():
            for j in range(nk - 1):
                add_tile(j + jnp.where(j >= qi, 1, 0), 0.0)

        excess = jnp.max(jnp.maximum(m1[...] - sh1, m2[...] - sh2))
        flag_ref[0] = (excess > SPEC_MARGIN).astype(jnp.int32)

    @pl.when(flag_ref[0] == 1)
    def _():
        for m_ref in (m1, m2):
            m_ref[...] = jnp.full(m_ref.shape, M_INIT, F32)
        for z_ref in (acc1, acc2):
            z_ref[...] = jnp.zeros(z_ref.shape, F32)

        def update(ki, general):
            for s, m_ref, acc_ref in zip(tile_scores(ki, general), (m1, m2), (acc1, acc2)):
                _softmax_update(s, col_max(s), vt[ki], m_ref, acc_ref)

        def step(ki, carry):
            mode = mode_ref[(b * nq + qi) * nk + ki]

            @pl.when(mode == 0)
            def _():
                update(ki, True)

            @pl.when(mode != 0)
            def _():
                update(ki, False)

            return carry

        lax.fori_loop(0, nk, step, 0)

    lamv = lam_ref[...]
    lam = (jnp.exp(jnp.sum(lamv[0:1] * lamv[1:2], keepdims=True))
           - jnp.exp(jnp.sum(lamv[2:3] * lamv[3:4], keepdims=True)) + LAM_INIT)
    o = _normalised(acc1) - lam * _normalised(acc2)
    ms = jnp.mean(o * o, axis=0, keepdims=True)
    on = (o * lax.rsqrt(ms + SUBLN_EPS)).T * subg_ref[...] * (1.0 - LAM_INIT)
    ag = ag_ref[rows, :].astype(F32)
    o_ref[rows, :] = (on * (ag * jax.nn.sigmoid(ag))).astype(BF16)


def _attn_a(mode, fast, cb, proj3, qbias_t, kbias, cq, ck, posq_row, posk_col, dmin, lamv, subg,
            *, tq, tk):
    bsz, s, _ = proj3.shape
    nq, nk = s // tq, s // tk
    assert nk <= 24
    kern = functools.partial(_attn_a_kernel, tq=tq, tk=tk, nq=nq, nk=nk)
    stat = pltpu.VMEM((1, tq), F32)
    grid_spec = pltpu.PrefetchScalarGridSpec(
        num_scalar_prefetch=3,
        grid=(bsz, A_HEADS),
        in_specs=[
            pl.BlockSpec((None, s, LANE), lambda b, h, *_: (b, 0, COL_AQ + h)),
            pl.BlockSpec((None, s, LANE), lambda b, h, *_: (b, 0, COL_AK + h)),
            pl.BlockSpec((None, s, LANE), lambda b, h, *_: (b, 0, COL_AV + h)),
            pl.BlockSpec((None, s, LANE), lambda b, h, *_: (b, 0, COL_AG + h)),
            pl.BlockSpec((None, nq, LANE, tq), lambda b, h, *_: (b, 0, 0, 0)),
            pl.BlockSpec((None, s, LANE), lambda b, h, *_: (b, 0, 0)),
            pl.BlockSpec((None, LANE, 1), lambda b, h, *_: (h, 0, 0)),
            pl.BlockSpec((None, 1, LANE), lambda b, h, *_: (h, 0, 0)),
            pl.BlockSpec((None, nq, 1, tq), lambda b, h, *_: (b, 0, 0, 0)),
            pl.BlockSpec((None, s, 1), lambda b, h, *_: (b, 0, 0)),
            pl.BlockSpec((None, nq, 1, LANE), lambda b, h, *_: (b, 0, 0, 0)),
            pl.BlockSpec((4, A_HEAD_DIM), lambda b, h, *_: (0, 0)),
            pl.BlockSpec((1, A_V_DIM), lambda b, h, *_: (0, 0)),
        ],
        out_specs=pl.BlockSpec((None, s, LANE), lambda b, h, *_: (b, 0, h)),
        scratch_shapes=[
            pltpu.VMEM((LANE, tq), BF16), pltpu.VMEM((LANE, tq), BF16),
            pltpu.VMEM((nk, V_ROWS, tk), BF16),
            stat, stat,
            pltpu.VMEM((V_ROWS, tq), F32), pltpu.VMEM((V_ROWS, tq), F32),
            pltpu.SMEM((1,), jnp.int32), pltpu.VMEM((2, LANE), F32), pltpu.SMEM((nk + 1,), jnp.int32),
        ],
    )
    return pl.pallas_call(
        kern,
        grid_spec=grid_spec,
        out_shape=jax.ShapeDtypeStruct((bsz, s, A_WIDTH), BF16),
        compiler_params=pltpu.CompilerParams(
            dimension_semantics=("arbitrary", "arbitrary"), vmem_limit_bytes=VMEM_LIMIT),
        name="attn_a",
    )(mode, fast, cb, proj3, proj3, proj3, proj3, qbias_t, kbias, cq, ck, posq_row, posk_col, dmin,
      lamv, subg)


def _mla_prep_kernel(cq_ref, ckv_ref, kr_ref, posc_ref, posr_ref, invl_ref, invc_ref, qg_ref, kvg_ref,
                     wuqt_ref, wuk_ref, wuvt_ref, vones_ref, qo_ref, ko_ref, vo_ref):
    def norm(ref, g_ref):
        t = ref[...].astype(F32)
        ms = jnp.mean(t * t, axis=-1, keepdims=True)
        return (t * lax.rsqrt(ms + NORM_EPS) * g_ref[...]).astype(BF16)

    half = B_ROPE // 2
    cqn = norm(cq_ref, qg_ref)
    ckvn = norm(ckv_ref, kvg_ref)

    qt = _dot_nt(wuqt_ref[...], cqn)
    ang_t = invc_ref[...] * posr_ref[...]
    cs_t = jnp.cos(ang_t)
    sn_t = jnp.sin(ang_t)
    for h in range(B_HEADS):
        r0 = h * 2 * LANE
        qo_ref[r0:r0 + LANE, :] = qt[r0:r0 + LANE, :].astype(BF16)
        t1 = qt[r0 + LANE:r0 + LANE + half, :]
        t2 = qt[r0 + LANE + half:r0 + LANE + 2 * half, :]
        qo_ref[r0 + LANE:r0 + LANE + half, :] = (t1 * cs_t - t2 * sn_t).astype(BF16)
        qo_ref[r0 + LANE + half:r0 + LANE + 2 * half, :] = (t1 * sn_t + t2 * cs_t).astype(BF16)
        qo_ref[r0 + LANE + 2 * half:r0 + 2 * LANE, :] = jnp.zeros((LANE - 2 * half, qt.shape[1]), BF16)

    lane = lax.broadcasted_iota(jnp.int32, (1, LANE), 1)
    ang = posc_ref[...] * invl_ref[...]
    cs = jnp.cos(ang)
    sn = jnp.sin(ang)
    c_t = jnp.where(lane < B_ROPE, cs, 0.0)
    s1_t = jnp.where(lane < half, -sn, 0.0)
    s2_t = jnp.where((lane >= half) & (lane < B_ROPE), sn, 0.0)
    kr = kr_ref[...].astype(F32)
    krope = (kr * c_t + pltpu.roll(kr, LANE - half, axis=1) * s1_t
             + pltpu.roll(kr, half, axis=1) * s2_t).astype(BF16)
    kf = jnp.dot(ckvn, wuk_ref[...], preferred_element_type=F32)
    for h in range(B_HEADS):
        c0 = h * 2 * LANE
        ko_ref[:, c0:c0 + LANE] = kf[:, h * LANE:(h + 1) * LANE].astype(BF16)
        ko_ref[:, c0 + LANE:c0 + 2 * LANE] = krope

    vt = (_dot_nt(wuvt_ref[...], ckvn) + vones_ref[...]).astype(BF16)
    tkv = vo_ref.shape[2]
    for c in range(vo_ref.shape[0]):
        vo_ref[c] = vt[:, c * tkv:(c + 1) * tkv]


def _mla_prep(proj2, pos_col, pos_row, inv_lane, inv_col, qg, kvg, wuqt, wuk, wuvt, vones, *, tm, tkv):
    m = proj2.shape[0]
    wq = B_HEADS * 2 * LANE
    wv = B_HEADS * V_ROWS
    return pl.pallas_call(
        _mla_prep_kernel,
        grid=(m // tm,),
        in_specs=[
            pl.BlockSpec((tm, B_Q_LORA), lambda i: (i, COL_CQ // 4)),
            pl.BlockSpec((tm, B_KV_LORA), lambda i: (i, COL_CKV // 4)),
            pl.BlockSpec((tm, LANE), lambda i: (i, COL_KR)),
            pl.BlockSpec((tm, 1), lambda i: (i, 0)),
            pl.BlockSpec((1, tm), lambda i: (0, i)),
            pl.BlockSpec((1, LANE), lambda i: (0, 0)),
            pl.BlockSpec((B_ROPE // 2, 1), lambda i: (0, 0)),
            pl.BlockSpec((1, B_Q_LORA), lambda i: (0, 0)),
            pl.BlockSpec((1, B_KV_LORA), lambda i: (0, 0)),
            pl.BlockSpec((wq, B_Q_LORA), lambda i: (0, 0)),
            pl.BlockSpec((B_KV_LORA, B_HEADS * B_NOPE), lambda i: (0, 0)),
            pl.BlockSpec((wv, B_KV_LORA), lambda i: (0, 0)),
            pl.BlockSpec((wv, 1), lambda i: (0, 0)),
        ],
        out_specs=[
            pl.BlockSpec((None, wq, tm), lambda i: (i, 0, 0)),
            pl.BlockSpec((tm, wq), lambda i: (i, 0)),
            pl.BlockSpec((tm // tkv, wv, tkv), lambda i: (i, 0, 0)),
        ],
        out_shape=[
            jax.ShapeDtypeStruct((m // tm, wq, tm), BF16),
            jax.ShapeDtypeStruct((m, wq), BF16),
            jax.ShapeDtypeStruct((m // tkv, wv, tkv), BF16),
        ],
        compiler_params=pltpu.CompilerParams(
            dimension_semantics=("arbitrary",), vmem_limit_bytes=VMEM_LIMIT),
        name="mla_prep",
    )(proj2, proj2, proj2, pos_col, pos_row, inv_lane, inv_col, qg, kvg, wuqt, wuk, wuvt, vones)


def _attn_b_kernel(qt_ref, k_ref, vt_ref, bg_ref, o_ref, m1, acc1, flag_ref, *, tq, tk, nq, nk):
    def q_tile(qi, c):
        _attn_b_q_tile(qi, qt_ref, k_ref, vt_ref, bg_ref, o_ref, m1, acc1, flag_ref,
                       tq=tq, tk=tk, nk=nk)
        return c
    lax.fori_loop(0, nq, q_tile, 0)


def _attn_b_q_tile(qi, qt_ref, k_ref, vt_ref, bg_ref, o_ref, m1, acc1, flag_ref, *, tq, tk, nk):
    rows = pl.ds(pl.multiple_of(qi * tq, tq), tq)

    def raw_scores(ki):
        ks = pl.ds(pl.multiple_of(ki * tk, tk), tk)
        return jnp.dot(k_ref[ks, :], qt_ref[qi], preferred_element_type=F32)

    s = raw_scores(0)
    shift = jnp.max(s, axis=0, keepdims=True)
    top = shift
    acc1[...] = jnp.dot(vt_ref[0], jnp.exp2(s - shift).astype(BF16), preferred_element_type=F32)
    for ki in range(1, nk):
        s = raw_scores(ki)
        top = jnp.maximum(top, jnp.max(s, axis=0, keepdims=True))
        acc1[...] += jnp.dot(vt_ref[ki], jnp.exp2(s - shift).astype(BF16),
                             preferred_element_type=F32)
    flag_ref[0] = (jnp.max(top - shift) > SPEC_MARGIN).astype(jnp.int32)

    @pl.when(flag_ref[0] == 1)
    def _():
        m1[...] = jnp.full(m1.shape, M_INIT, F32)
        acc1[...] = jnp.zeros(acc1.shape, F32)

        def step(ki, carry):
            s = raw_scores(ki)
            _softmax_update(s, jnp.max(s, axis=0, keepdims=True), vt_ref[ki], m1, acc1)
            return carry

        lax.fori_loop(0, nk, step, 0)

    bg = bg_ref[rows, :].astype(F32)
    o_ref[rows, :] = (_normalised(acc1).T * (bg * jax.nn.sigmoid(bg))).astype(BF16)


def _attn_b(qt3, kb3, vt3, proj3, *, tq, tk):
    bsz, s, _ = kb3.shape
    nq, nk = s // tq, s // tk
    return pl.pallas_call(
        functools.partial(_attn_b_kernel, tq=tq, tk=tk, nq=nq, nk=nk),
        grid=(bsz, B_HEADS),
        in_specs=[
            pl.BlockSpec((nq, 2 * LANE, tq), lambda b, h: (b, h, 0)),
            pl.BlockSpec((None, s, 2 * LANE), lambda b, h: (b, 0, h)),
            pl.BlockSpec((nk, V_ROWS, tk), lambda b, h: (b, h, 0)),
            pl.BlockSpec((None, s, LANE), lambda b, h: (b, 0, COL_BG + h)),
        ],
        out_specs=pl.BlockSpec((None, s, LANE), lambda b, h: (b, 0, h)),
        out_shape=jax.ShapeDtypeStruct((bsz, s, B_WIDTH), BF16),
        scratch_shapes=[
            pltpu.VMEM((1, tq), F32), pltpu.VMEM((V_ROWS, tq), F32), pltpu.SMEM((1,), jnp.int32),
        ],
        compiler_params=pltpu.CompilerParams(
            dimension_semantics=("arbitrary", "arbitrary"), vmem_limit_bytes=VMEM_LIMIT),
        name="attn_b",
    )(qt3, kb3, vt3, proj3)


def _post_kernel(za_ref, zb_ref, ga_ref, gb_ref, x_ref, woa_ref, wob_ref, wout_ref, fg_ref, o_ref):
    ya = jnp.dot(za_ref[...], woa_ref[...], preferred_element_type=F32)
    yb = jnp.dot(zb_ref[...], wob_ref[...], preferred_element_type=F32)
    merged = (jax.nn.sigmoid(ga_ref[...].astype(F32)) * ya
              + jax.nn.sigmoid(gb_ref[...].astype(F32)) * yb)
    y = x_ref[...] + jnp.dot(merged.astype(BF16), wout_ref[...], preferred_element_type=F32)
    ms = jnp.mean(y * y, axis=-1, keepdims=True)
    o_ref[...] = y * lax.rsqrt(ms + NORM_EPS) * fg_ref[...]


def _post(za, zb, proj2, x2, woa, wob, wout, fg, *, tm=256):
    m = x2.shape[0]
    const = lambda shape: pl.BlockSpec(shape, lambda i: (0, 0), pipeline_mode=pl.Buffered(1))
    return pl.pallas_call(
        _post_kernel,
        grid=(m // tm,),
        in_specs=[
            pl.BlockSpec((tm, A_WIDTH), lambda i: (i, 0)),
            pl.BlockSpec((tm, B_WIDTH), lambda i: (i, 0)),
            pl.BlockSpec((tm, D_MODEL), lambda i: (i, COL_GA // 16)),
            pl.BlockSpec((tm, D_MODEL), lambda i: (i, COL_GB // 16)),
            pl.BlockSpec((tm, D_MODEL), lambda i: (i, 0)),
            const((A_WIDTH, D_MODEL)),
            const((B_WIDTH, D_MODEL)),
            const((D_MODEL, D_MODEL)),
            const((1, D_MODEL)),
        ],
        out_specs=pl.BlockSpec((tm, D_MODEL), lambda i: (i, 0)),
        out_shape=jax.ShapeDtypeStruct((m, D_MODEL), F32),
        compiler_params=pltpu.CompilerParams(
            dimension_semantics=("arbitrary",), vmem_limit_bytes=VMEM_LIMIT),
        name="post",
    )(za, zb, proj2, proj2, x2, woa, wob, wout, fg)


W_TN = 512
KR_COLS = _SIZES[6]


def _w_in_plan():
    off = dict(zip(("aq", "ak", "av", "ag", "cq", "ckv", "kr", "bg", "mg"),
                   [0] + _SPLITS))
    a_blk, b_blk, kind, scale = [], [], [], []
    for j in range(PROJ_COLS // W_TN):
        c0 = j * W_TN
        if c0 < COL_GA * LANE:
            src, k = off["aq"] + c0, 0
        elif c0 < COL_BG * LANE:
            src, k = off["mg"] + c0 - COL_GA * LANE, 1
        elif c0 < COL_CQ * LANE:
            src, k = off["bg"] + c0 - COL_BG * LANE, 1
        elif c0 < COL_KR * LANE:
            src, k = off["cq"] + c0 - COL_CQ * LANE, 0
        else:
            src, k = off["kr"], 2
        assert src % W_TN == (KR_COLS if k == 1 else 0)
        a_blk.append(src // W_TN)
        b_blk.append((src // W_TN + 1) * (W_TN // KR_COLS) if k == 1 else 0)
        kind.append(k)
        scale.append(A_HEAD_DIM ** -0.5 * LOG2E if c0 < COL_AK * LANE else 1.0)
    i32 = lambda v: jnp.asarray(v, jnp.int32)
    return i32(a_blk), i32(b_blk), i32(kind), jnp.asarray(scale, F32)


def _w_prep_kernel(ablk_ref, bblk_ref, kind_ref, scale_ref, a_ref, b_ref, o_ref):
    j = pl.program_id(0)
    kind = kind_ref[j]

    @pl.when(kind == 0)
    def _():
        o_ref[...] = (a_ref[...] * scale_ref[j]).T.astype(BF16)

    @pl.when(kind == 1)
    def _():
        t = jnp.concatenate([a_ref[KR_COLS:W_TN, :], b_ref[...]], axis=0)
        o_ref[...] = t.T.astype(BF16)

    @pl.when(kind == 2)
    def _():
        row = lax.broadcasted_iota(jnp.int32, (W_TN, 1), 0)
        o_ref[...] = jnp.where(row < KR_COLS, a_ref[...], 0.0).T.astype(BF16)


def _prep_w_in(w_t):
    d = w_t.shape[1]
    grid_spec = pltpu.PrefetchScalarGridSpec(
        num_scalar_prefetch=4,
        grid=(PROJ_COLS // W_TN,),
        in_specs=[
            pl.BlockSpec((W_TN, d), lambda j, a, b, k, s: (a[j], 0)),
            pl.BlockSpec((KR_COLS, d), lambda j, a, b, k, s: (b[j], 0)),
        ],
        out_specs=pl.BlockSpec((d, W_TN), lambda j, a, b, k, s: (0, j)),
    )
    return pl.pallas_call(
        _w_prep_kernel,
        grid_spec=grid_spec,
        out_shape=jax.ShapeDtypeStruct((d, PROJ_COLS), BF16),
        compiler_params=pltpu.CompilerParams(
            dimension_semantics=("arbitrary",), vmem_limit_bytes=VMEM_LIMIT),
        name="w_prep",
    )(*_w_in_plan(), w_t, w_t)


def _prep_w_uq_t(w):
    w = w.reshape(B_Q_LORA, B_HEADS, B_NOPE + B_ROPE) * ((B_NOPE + B_ROPE) ** -0.5 * LOG2E)
    w = jnp.pad(w, ((0, 0), (0, 0), (0, 2 * LANE - (B_NOPE + B_ROPE))))
    return w.reshape(B_Q_LORA, B_HEADS * 2 * LANE).T.astype(BF16)


def _prep_w_uv_t(w_ukv):
    w = w_ukv.reshape(B_KV_LORA, B_HEADS, B_NOPE + B_V_DIM)[:, :, B_NOPE:]
    w = jnp.pad(w, ((0, 0), (0, 0), (0, V_ROWS - B_V_DIM)))
    return w.reshape(B_KV_LORA, B_HEADS * V_ROWS).T.astype(BF16)


def _split3_bf16(c):
    c1 = c.astype(BF16).astype(F32)
    c2 = (c - c1).astype(BF16).astype(F32)
    c3 = (c - c1 - c2).astype(BF16).astype(F32)
    return c1, c2, c3


def _alibi_operands(positions, slopes, tq, tk):
    bsz, s = positions.shape
    nq, nk = s // tq, s // tk
    pmin = jnp.min(positions, axis=1, keepdims=True)
    rel = positions - pmin
    ok = ((jnp.min(rel, axis=1) >= 0) & (jnp.max(rel, axis=1) < 65536)
          & (jnp.max(jnp.abs(positions), axis=1) < (1 << 24)))
    qmin = jnp.min(positions.reshape(bsz, nq, tq), axis=-1)
    qmax = jnp.max(positions.reshape(bsz, nq, tq), axis=-1)
    kmin = jnp.min(positions.reshape(bsz, nk, tk), axis=-1)
    kmax = jnp.max(positions.reshape(bsz, nk, tk), axis=-1)
    after = qmin[:, :, None] >= kmax[:, None, :]
    before = qmax[:, :, None] <= kmin[:, None, :]
    mode = jnp.where(after, 1, jnp.where(before, -1, 0)) * ok[:, None, None].astype(jnp.int32)
    gap = jnp.where(after, qmin[:, :, None] - kmax[:, None, :], kmin[:, None, :] - qmax[:, :, None])
    dmin = jnp.where(mode != 0, gap, 0).astype(F32)
    off_diag = ~jnp.eye(nq, nk, dtype=bool)[None]
    fast = jnp.all((mode != 0) | ~off_diag, axis=-1) & (nq == nk)

    rel = jnp.where(ok[:, None], rel, 0)
    hi = (rel >> 8).astype(F32)
    lo = (rel & 255).astype(F32)

    g = np.arange(LANE) % A_HEAD_DIM
    first, second = g < N_BIAS // 2, (g >= N_BIAS // 2) & (g < N_BIAS)
    even, odd = g % 2 == 0, g % 2 == 1
    f32 = lambda mask: jnp.asarray(mask, F32)
    q_hi, q_lo, q_one = f32(first & even), f32(first & odd), f32(second)
    k_hi, k_lo, k_one = f32(second & even), f32(second & odd), f32(first)
    hi_q, lo_q = hi.reshape(bsz, nq, 1, tq), lo.reshape(bsz, nq, 1, tq)
    col = lambda v: v[None, None, :, None]
    qb_t = (hi_q * col(q_hi) + lo_q * col(q_lo) + col(q_one)).astype(BF16)
    kb = (hi[:, :, None] * k_hi[None, None, :] + lo[:, :, None] * k_lo[None, None, :]
          + k_one[None, None, :]).astype(BF16)

    cb = slopes * LOG2E
    pieces = _split3_bf16(cb)
    cpair = jnp.stack([v for c in pieces for v in (256.0 * c, c)], axis=-1)
    sel = np.zeros((N_BIAS // 2, LANE), np.float32)
    for i in range(N_BIAS // 2):
        sel[i, (g % (N_BIAS // 2) == i) & (g < N_BIAS)] = 1.0
    spread = jnp.dot(cpair, jnp.asarray(sel), precision=lax.Precision.HIGHEST)
    cq = (spread * f32(second) + f32(first))[:, :, None]
    ck = (-spread * f32(first) + f32(second))[:, None, :]
    return (mode.reshape(-1).astype(jnp.int32), fast.reshape(-1).astype(jnp.int32), cb.astype(F32),
            jnp.pad(dmin, ((0, 0), (0, 0), (0, LANE - nk)))[:, :, None, :],
            qb_t, kb, cq, ck)


def kernel(x, positions, norm_g, w_in, lam_q1, lam_k1, lam_q2, lam_k2, a_subln_g, w_oa, q_norm_g,
           w_uq, kv_norm_g, w_ukv, w_ob, w_out, final_g):
    bsz, s, d = x.shape
    m = bsz * s
    assert norm_g.shape[0] == 1 and d == D_MODEL and s % TQ == 0 and s % TK == 0
    layer = 0
    x2 = x.reshape(m, d)
    pos_f = positions.astype(F32)

    proj2 = _inproj(x2, norm_g[layer][None, :], _prep_w_in(w_in[layer].T))
    proj3 = proj2.reshape(bsz, s, PROJ_COLS)

    slopes = jnp.asarray([2.0 ** (-8.0 * (h + 1) / A_HEADS) for h in range(A_HEADS)], dtype=F32)
    mode, fast, cb, dmin, qb_t, kb, cq, ck = _alibi_operands(positions, slopes, TQ, TK)
    lamv = jnp.stack([lam_q1[layer], lam_k1[layer], lam_q2[layer], lam_k2[layer]]).astype(F32)
    za = _attn_a(mode, fast, cb, proj3, qb_t, kb, cq, ck, pos_f.reshape(bsz, s // TQ, 1, TQ),
                 pos_f[:, :, None], dmin, lamv, a_subln_g[layer][None, :], tq=TQ, tk=TK)

    half = B_ROPE // 2
    inv = ROPE_THETA ** (-jnp.arange(half, dtype=F32) / half)
    inv_lane = jnp.concatenate([inv, inv, jnp.zeros((LANE - B_ROPE,), F32)])[None, :]
    wuk = (w_ukv[layer].reshape(B_KV_LORA, B_HEADS, B_NOPE + B_V_DIM)[:, :, :B_NOPE]
           .reshape(B_KV_LORA, B_HEADS * B_NOPE).astype(BF16))
    vones = (jnp.arange(B_HEADS * V_ROWS) % V_ROWS == B_V_DIM).astype(F32)[:, None]
    qt3, kb2, vt3 = _mla_prep(proj2, pos_f.reshape(m, 1), pos_f.reshape(1, m), inv_lane, inv[:, None],
                              q_norm_g[layer][None, :], kv_norm_g[layer][None, :],
                              _prep_w_uq_t(w_uq[layer]), wuk, _prep_w_uv_t(w_ukv[layer]), vones,
                              tm=TQ, tkv=TK_B)
    zb = _attn_b(qt3, kb2.reshape(bsz, s, -1), vt3, proj3, tq=TQ, tk=TK_B)

    out = _post(za.reshape(m, A_WIDTH), zb.reshape(m, B_WIDTH), proj2, x2, w_oa[layer].astype(BF16),
                w_ob[layer].astype(BF16), w_out[layer].astype(BF16), final_g[None, :])
    return out.reshape(bsz, s, d)
```

```python
import functools
import math

import numpy as np
import jax
import jax.numpy as jnp
from jax import lax
from jax.experimental import pallas as pl
from jax.experimental.pallas import tpu as pltpu

F32 = jnp.float32
BF16 = jnp.bfloat16

D_MODEL = 2048
A_HEADS = 8
A_HEAD_DIM = 64
A_V_DIM = 128
A_WIDTH = 1024
B_HEADS = 8
B_Q_LORA = 512
B_KV_LORA = 512
B_NOPE = 128
B_ROPE = 64
B_V_DIM = 128
B_WIDTH = 1024
ROPE_THETA = 10000.0
NORM_EPS = 1e-6
SUBLN_EPS = 1e-5
LAM_INIT = 0.8 - 0.6 * math.exp(-0.3 * 0)
LOG2E = math.log2(math.e)

_SIZES = [1024, 1024, 1024, 1024, 512, 512, 64, 1024, 4096]
_SPLITS = [int(v) for v in np.cumsum(_SIZES)[:-1]]

LANE = 128
BF16_ROWS = 16
COL_AQ, COL_AK, COL_AV, COL_AG = 0, 8, 16, 24
COL_GA, COL_GB, COL_BG = 32, 48, 64
COL_CQ, COL_CKV, COL_KR = 72, 76, 80
PROJ_COLS = 10752

TQ = 512
TK = 512
TK_B = 512
SCORE_BUFS = 2
V_ROWS = A_V_DIM + BF16_ROWS
N_BIAS = 12
VMEM_LIMIT = 56 * 1024 * 1024
M_INIT = -1e30
SPEC_MARGIN = 60.0
SKIP_LOG2 = 160.0
SKIP_SLACK = 1.01


def _dot_nt(a, b):
    return lax.dot_general(a, b, (((1,), (1,)), ((), ())), preferred_element_type=F32)


def _inproj_kernel(x_ref, g_ref, w_ref, o_ref, h_ref, *, tm):
    @pl.when(pl.program_id(1) == 0)
    def _():
        def body(r, c):
            rows = pl.ds(pl.multiple_of(r * 128, 128), 128)
            x = x_ref[rows, :]
            ms = jnp.mean(x * x, axis=-1, keepdims=True)
            h_ref[rows, :] = (x * lax.rsqrt(ms + NORM_EPS) * g_ref[...]).astype(BF16)
            return c
        lax.fori_loop(0, tm // 128, body, 0)

    o_ref[...] = jnp.dot(h_ref[...], w_ref[...], preferred_element_type=F32).astype(BF16)


def _inproj(x2, g, w, *, tm=1024, tn=1536):
    m = x2.shape[0]
    return pl.pallas_call(
        functools.partial(_inproj_kernel, tm=tm),
        grid=(m // tm, PROJ_COLS // tn),
        in_specs=[
            pl.BlockSpec((tm, D_MODEL), lambda i, j: (i, 0)),
            pl.BlockSpec((1, D_MODEL), lambda i, j: (0, 0)),
            pl.BlockSpec((D_MODEL, tn), lambda i, j: (0, j)),
        ],
        out_specs=pl.BlockSpec((tm, tn), lambda i, j: (i, j)),
        out_shape=jax.ShapeDtypeStruct((m, PROJ_COLS), BF16),
        scratch_shapes=[pltpu.VMEM((tm, D_MODEL), BF16)],
        compiler_params=pltpu.CompilerParams(
            dimension_semantics=("arbitrary", "arbitrary"), vmem_limit_bytes=VMEM_LIMIT),
        name="inproj",
    )(x2, g, w)


def _softmax_update(s, s_max, vt, m_ref, acc_ref):
    m_old = m_ref[...]
    m_new = jnp.maximum(m_old, s_max)
    p = jnp.exp2(s - m_new)
    alpha = jnp.exp2(m_old - m_new)
    acc_ref[...] = alpha * acc_ref[...] + jnp.dot(vt, p.astype(BF16), preferred_element_type=F32)
    m_ref[...] = m_new


def _normalised(acc_ref):
    return acc_ref[0:A_V_DIM, :] / acc_ref[A_V_DIM:A_V_DIM + 1, :]


def _attn_a_kernel(mode_ref, fast_ref, cb_ref, q_ref, k_ref, v_ref, ag_ref, qbt_ref, kb_ref,
                   cq_ref, ck_ref, posq_ref, posk_ref, dmin_ref, lam_ref, subg_ref, o_ref,
                   qa1, qa2, vt, m1, m2, acc1, acc2, flag_ref, kn_ref, list_ref, *, tq, tk, nq, nk):
    b = pl.program_id(0)
    cb = cb_ref[pl.program_id(1)]

    ones_rows = (lax.broadcasted_iota(jnp.int32, (BF16_ROWS, tk), 0) == 0).astype(BF16)
    feat = lax.broadcasted_iota(jnp.int32, (LANE, LANE), 0)
    col = lax.broadcasted_iota(jnp.int32, (LANE, LANE), 1)
    half_sum = (col == jnp.where(feat < A_HEAD_DIM, 0, 1)).astype(BF16)
    lane = lax.broadcasted_iota(jnp.int32, (1, LANE), 1)
    kn = [jnp.zeros((1, LANE), F32), jnp.zeros((1, LANE), F32)]
    for ki in range(nk):
        vt[ki, 0:A_V_DIM, :] = v_ref[ki * tk:(ki + 1) * tk, :].astype(F32).T.astype(BF16)
        vt[ki, A_V_DIM:V_ROWS, :] = ones_rows
        k = k_ref[ki * tk:(ki + 1) * tk, :].astype(F32)
        nsq = jnp.dot((k * k).astype(BF16), half_sum, preferred_element_type=F32)
        nmax = jnp.sqrt(jnp.max(nsq, axis=0, keepdims=True))
        for m in range(2):
            norm_m = jnp.max(jnp.where(lane == m, nmax, 0.0), axis=1, keepdims=True)
            kn[m] = jnp.where(lane == ki, norm_m, kn[m])
    kn_ref[0:1, :] = kn[0]
    kn_ref[1:2, :] = kn[1]

    def q_tile(qi, c):
        _attn_a_q_tile(qi, b, cb, mode_ref, fast_ref, dmin_ref, q_ref, k_ref, ag_ref, qbt_ref, kb_ref,
                       cq_ref, ck_ref, posq_ref, posk_ref, lam_ref, subg_ref, o_ref,
                       qa1, qa2, vt, m1, m2, acc1, acc2, flag_ref, kn_ref, list_ref,
                       tq=tq, tk=tk, nq=nq, nk=nk)
        return c
    lax.fori_loop(0, nq, q_tile, 0)


def _attn_a_q_tile(qi, b, cb, mode_ref, fast_ref, dmin_ref, q_ref, k_ref, ag_ref, qbt_ref, kb_ref,
                   cq_ref, ck_ref, posq_ref, posk_ref, lam_ref, subg_ref, o_ref,
                   qa1, qa2, vt, m1, m2, acc1, acc2, flag_ref, kn_ref, list_ref, *, tq, tk, nq, nk):
    rows = pl.ds(pl.multiple_of(qi * tq, tq), tq)
    posq = posq_ref[qi]

    lo_r = lax.broadcasted_iota(jnp.int32, (LANE, 1), 0) < A_HEAD_DIM
    qs = q_ref[rows, :].astype(F32).T
    qbias = qbt_ref[qi].astype(F32) * cq_ref[...]
    qa1[...] = jnp.where(lo_r, qs, qbias).astype(BF16)
    qa2[...] = jnp.where(lo_r, qbias, qs).astype(BF16)
    lo_c = lax.broadcasted_iota(jnp.int32, (1, LANE), 1) < A_HEAD_DIM
    qsq = qs * qs
    qn1 = jnp.max(jnp.sqrt(jnp.sum(jnp.where(lo_r, qsq, 0.0), axis=0, keepdims=True)),
                  axis=1, keepdims=True)
    qn2 = jnp.max(jnp.sqrt(jnp.sum(jnp.where(lo_r, 0.0, qsq), axis=0, keepdims=True)),
                  axis=1, keepdims=True)

    def raw_scores(ki, sgn):
        ks = pl.ds(pl.multiple_of(ki * tk, tk), tk)
        k = k_ref[ks, :]
        kbias = (kb_ref[ks, :].astype(F32) * (ck_ref[...] * sgn)).astype(BF16)
        s1 = jnp.dot(jnp.where(lo_c, k, kbias), qa1[...], preferred_element_type=F32)
        s2 = jnp.dot(jnp.where(lo_c, kbias, k), qa2[...], preferred_element_type=F32)
        return s1, s2

    def explicit_bias(ki):
        ks = pl.ds(pl.multiple_of(ki * tk, tk), tk)
        return -cb * jnp.abs(posk_ref[ks, :] - posq)

    def tile_scores(ki, general):
        if general:
            s1, s2 = raw_scores(ki, 0.0)
            bias = explicit_bias(ki)
            return s1 + bias, s2 + bias
        return raw_scores(ki, mode_ref[(b * nq + qi) * nk + ki].astype(F32))

    def col_max(s):
        return jnp.max(s, axis=0, keepdims=True)

    def pv(ki, s, shift):
        return jnp.dot(vt[ki], jnp.exp2(s - shift).astype(BF16), preferred_element_type=F32)

    fast = fast_ref[b * nq + qi]
    flag_ref[0] = 1 - fast

    @pl.when(fast == 1)
    def _():
        s1, s2 = tile_scores(qi, True)
        sh1, sh2 = col_max(s1), col_max(s2)
        m1[...] = sh1
        m2[...] = sh2
        acc1[...] = pv(qi, s1, sh1)
        acc2[...] = pv(qi, s2, sh2)

        def add_tile(t, off):
            s1, s2 = tile_scores(t, False)
            m1[...] = jnp.maximum(m1[...], col_max(s1) - off)
            m2[...] = jnp.maximum(m2[...], col_max(s2) - off)
            acc1[...] += pv(t, s1, sh1 + off)
            acc2[...] += pv(t, s2, sh2 + off)

        lo1 = jnp.min(sh1, axis=1, keepdims=True)
        lo2 = jnp.min(sh2, axis=1, keepdims=True)
        gap = cb * dmin_ref[qi]
        ub1 = SKIP_SLACK * qn1 * kn_ref[0:1, :] + 1.0 - gap - lo1
        ub2 = SKIP_SLACK * qn2 * kn_ref[1:2, :] + 1.0 - gap - lo2
        lane = lax.broadcasted_iota(jnp.int32, (1, LANE), 1)
        keep = (lane != qi) & (lane < nk) & ((ub1 > -SKIP_LOG2) | (ub2 > -SKIP_LOG2))
        bits = jnp.sum(jnp.where(keep, jnp.exp2(lane.astype(F32)), 0.0)).astype(jnp.int32)
        n = jnp.int32(0)
        for t in range(nk):
            list_ref[n] = jnp.int32(t)
            n = n + (lax.shift_right_logical(bits, t) & 1)
        list_ref[n] = qi

        short = n <= nk - 4

        @pl.when(short)
        def _():
            def pair(p, c):
                for e in range(2):
                    idx = 2 * p + e
                    add_tile(list_ref[idx], jnp.where(idx < n, 0.0, -M_INIT))
                return c
            lax.fori_loop(0, lax.shift_right_logical(n + 1, 1), pair, 0)

        @pl.when(jnp.logical_not(short))
        def _():
            for j in range(nk - 1):
                add_tile(j + jnp.where(j >= qi, 1, 0), 0.0)

        excess = jnp.max(jnp.maximum(m1[...] - sh1, m2[...] - sh2))
        flag_ref[0] = (excess > SPEC_MARGIN).astype(jnp.int32)

    @pl.when(flag_ref[0] == 1)
    def _():
        for m_ref in (m1, m2):
            m_ref[...] = jnp.full(m_ref.shape, M_INIT, F32)
        for z_ref in (acc1, acc2):
            z_ref[...] = jnp.zeros(z_ref.shape, F32)

        def update(ki, general):
            for s, m_ref, acc_ref in zip(tile_scores(ki, general), (m1, m2), (acc1, acc2)):
                _softmax_update(s, col_max(s), vt[ki], m_ref, acc_ref)

        def step(ki, carry):
            mode = mode_ref[(b * nq + qi) * nk + ki]

            @pl.when(mode == 0)
            def _():
                update(ki, True)

            @pl.when(mode != 0)
            def _():
                update(ki, False)

            return carry

        lax.fori_loop(0, nk, step, 0)

    lamv = lam_ref[...]
    lam = (jnp.exp(jnp.sum(lamv[0:1] * lamv[1:2], keepdims=True))
           - jnp.exp(jnp.sum(lamv[2:3] * lamv[3:4], keepdims=True)) + LAM_INIT)
    o = _normalised(acc1) - lam * _normalised(acc2)
    ms = jnp.mean(o * o, axis=0, keepdims=True)
    on = (o * lax.rsqrt(ms + SUBLN_EPS)).T * subg_ref[...] * (1.0 - LAM_INIT)
    ag = ag_ref[rows, :].astype(F32)
    o_ref[rows, :] = (on * (ag * jax.nn.sigmoid(ag))).astype(BF16)


def _attn_a(mode, fast, cb, proj3, qbias_t, kbias, cq, ck, posq_row, posk_col, dmin, lamv, subg,
            *, tq, tk):
    bsz, s, _ = proj3.shape
    nq, nk = s // tq, s // tk
    assert nk <= 24
    kern = functools.partial(_attn_a_kernel, tq=tq, tk=tk, nq=nq, nk=nk)
    stat = pltpu.VMEM((1, tq), F32)
    grid_spec = pltpu.PrefetchScalarGridSpec(
        num_scalar_prefetch=3,
        grid=(bsz, A_HEADS),
        in_specs=[
            pl.BlockSpec((None, s, LANE), lambda b, h, *_: (b, 0, COL_AQ + h)),
            pl.BlockSpec((None, s, LANE), lambda b, h, *_: (b, 0, COL_AK + h)),
            pl.BlockSpec((None, s, LANE), lambda b, h, *_: (b, 0, COL_AV + h)),
            pl.BlockSpec((None, s, LANE), lambda b, h, *_: (b, 0, COL_AG + h)),
            pl.BlockSpec((None, nq, LANE, tq), lambda b, h, *_: (b, 0, 0, 0)),
            pl.BlockSpec((None, s, LANE), lambda b, h, *_: (b, 0, 0)),
            pl.BlockSpec((None, LANE, 1), lambda b, h, *_: (h, 0, 0)),
            pl.BlockSpec((None, 1, LANE), lambda b, h, *_: (h, 0, 0)),
            pl.BlockSpec((None, nq, 1, tq), lambda b, h, *_: (b, 0, 0, 0)),
            pl.BlockSpec((None, s, 1), lambda b, h, *_: (b, 0, 0)),
            pl.BlockSpec((None, nq, 1, LANE), lambda b, h, *_: (b, 0, 0, 0)),
            pl.BlockSpec((4, A_HEAD_DIM), lambda b, h, *_: (0, 0)),
            pl.BlockSpec((1, A_V_DIM), lambda b, h, *_: (0, 0)),
        ],
        out_specs=pl.BlockSpec((None, s, LANE), lambda b, h, *_: (b, 0, h)),
        scratch_shapes=[
            pltpu.VMEM((LANE, tq), BF16), pltpu.VMEM((LANE, tq), BF16),
            pltpu.VMEM((nk, V_ROWS, tk), BF16),
            stat, stat,
            pltpu.VMEM((V_ROWS, tq), F32), pltpu.VMEM((V_ROWS, tq), F32),
            pltpu.SMEM((1,), jnp.int32), pltpu.VMEM((2, LANE), F32), pltpu.SMEM((nk + 1,), jnp.int32),
        ],
    )
    return pl.pallas_call(
        kern,
        grid_spec=grid_spec,
        out_shape=jax.ShapeDtypeStruct((bsz, s, A_WIDTH), BF16),
        compiler_params=pltpu.CompilerParams(
            dimension_semantics=("arbitrary", "arbitrary"), vmem_limit_bytes=VMEM_LIMIT),
        name="attn_a",
    )(mode, fast, cb, proj3, proj3, proj3, proj3, qbias_t, kbias, cq, ck, posq_row, posk_col, dmin,
      lamv, subg)


def _mla_prep_kernel(cq_ref, ckv_ref, kr_ref, posc_ref, posr_ref, invl_ref, invc_ref, qg_ref, kvg_ref,
                     wuqt_ref, wuk_ref, wuvt_ref, vones_ref, qo_ref, ko_ref, vo_ref):
    def norm(ref, g_ref):
        t = ref[...].astype(F32)
        ms = jnp.mean(t * t, axis=-1, keepdims=True)
        return (t * lax.rsqrt(ms + NORM_EPS) * g_ref[...]).astype(BF16)

    half = B_ROPE // 2
    cqn = norm(cq_ref, qg_ref)
    ckvn = norm(ckv_ref, kvg_ref)

    qt = _dot_nt(wuqt_ref[...], cqn)
    ang_t = invc_ref[...] * posr_ref[...]
    cs_t = jnp.cos(ang_t)
    sn_t = jnp.sin(ang_t)
    for h in range(B_HEADS):
        r0 = h * 2 * LANE
        qo_ref[r0:r0 + LANE, :] = qt[r0:r0 + LANE, :].astype(BF16)
        t1 = qt[r0 + LANE:r0 + LANE + half, :]
        t2 = qt[r0 + LANE + half:r0 + LANE + 2 * half, :]
        qo_ref[r0 + LANE:r0 + LANE + half, :] = (t1 * cs_t - t2 * sn_t).astype(BF16)
        qo_ref[r0 + LANE + half:r0 + LANE + 2 * half, :] = (t1 * sn_t + t2 * cs_t).astype(BF16)
        qo_ref[r0 + LANE + 2 * half:r0 + 2 * LANE, :] = jnp.zeros((LANE - 2 * half, qt.shape[1]), BF16)

    lane = lax.broadcasted_iota(jnp.int32, (1, LANE), 1)
    ang = posc_ref[...] * invl_ref[...]
    cs = jnp.cos(ang)
    sn = jnp.sin(ang)
    c_t = jnp.where(lane < B_ROPE, cs, 0.0)
    s1_t = jnp.where(lane < half, -sn, 0.0)
    s2_t = jnp.where((lane >= half) & (lane < B_ROPE), sn, 0.0)
    kr = kr_ref[...].astype(F32)
    krope = (kr * c_t + pltpu.roll(kr, LANE - half, axis=1) * s1_t
             + pltpu.roll(kr, half, axis=1) * s2_t).astype(BF16)
    kf = jnp.dot(ckvn, wuk_ref[...], preferred_element_type=F32)
    for h in range(B_HEADS):
        c0 = h * 2 * LANE
        ko_ref[:, c0:c0 + LANE] = kf[:, h * LANE:(h + 1) * LANE].astype(BF16)
        ko_ref[:, c0 + LANE:c0 + 2 * LANE] = krope

    vt = (_dot_nt(wuvt_ref[...], ckvn) + vones_ref[...]).astype(BF16)
    tkv = vo_ref.shape[2]
    for c in range(vo_ref.shape[0]):
        vo_ref[c] = vt[:, c * tkv:(c + 1) * tkv]


def _mla_prep(proj2, pos_col, pos_row, inv_lane, inv_col, qg, kvg, wuqt, wuk, wuvt, vones, *, tm, tkv):
    m = proj2.shape[0]
    wq = B_HEADS * 2 * LANE
    wv = B_HEADS * V_ROWS
    return pl.pallas_call(
        _mla_prep_kernel,
        grid=(m // tm,),
        in_specs=[
            pl.BlockSpec((tm, B_Q_LORA), lambda i: (i, COL_CQ // 4)),
            pl.BlockSpec((tm, B_KV_LORA), lambda i: (i, COL_CKV // 4)),
            pl.BlockSpec((tm, LANE), lambda i: (i, COL_KR)),
            pl.BlockSpec((tm, 1), lambda i: (i, 0)),
            pl.BlockSpec((1, tm), lambda i: (0, i)),
            pl.BlockSpec((1, LANE), lambda i: (0, 0)),
            pl.BlockSpec((B_ROPE // 2, 1), lambda i: (0, 0)),
            pl.BlockSpec((1, B_Q_LORA), lambda i: (0, 0)),
            pl.BlockSpec((1, B_KV_LORA), lambda i: (0, 0)),
            pl.BlockSpec((wq, B_Q_LORA), lambda i: (0, 0)),
            pl.BlockSpec((B_KV_LORA, B_HEADS * B_NOPE), lambda i: (0, 0)),
            pl.BlockSpec((wv, B_KV_LORA), lambda i: (0, 0)),
            pl.BlockSpec((wv, 1), lambda i: (0, 0)),
        ],
        out_specs=[
            pl.BlockSpec((None, wq, tm), lambda i: (i, 0, 0)),
            pl.BlockSpec((tm, wq), lambda i: (i, 0)),
            pl.BlockSpec((tm // tkv, wv, tkv), lambda i: (i, 0, 0)),
        ],
        out_shape=[
            jax.ShapeDtypeStruct((m // tm, wq, tm), BF16),
            jax.ShapeDtypeStruct((m, wq), BF16),
            jax.ShapeDtypeStruct((m // tkv, wv, tkv), BF16),
        ],
        compiler_params=pltpu.CompilerParams(
            dimension_semantics=("arbitrary",), vmem_limit_bytes=VMEM_LIMIT),
        name="mla_prep",
    )(proj2, proj2, proj2, pos_col, pos_row, inv_lane, inv_col, qg, kvg, wuqt, wuk, wuvt, vones)


def _attn_b_kernel(qt_ref, k_ref, vt_ref, bg_ref, o_ref, bufs, m1, acc1, flag_ref, *, tq, tk, nq, nk):
    def q_tile(qi, c):
        _attn_b_q_tile(qi, qt_ref, k_ref, vt_ref, bg_ref, o_ref, bufs, m1, acc1, flag_ref,
                       tq=tq, tk=tk, nk=nk)
        return c
    lax.fori_loop(0, nq, q_tile, 0)


def _attn_b_q_tile(qi, qt_ref, k_ref, vt_ref, bg_ref, o_ref, bufs, m1, acc1, flag_ref, *, tq, tk, nk):
    rows = pl.ds(pl.multiple_of(qi * tq, tq), tq)

    def raw_scores(ki):
        ks = pl.ds(pl.multiple_of(ki * tk, tk), tk)
        return jnp.dot(k_ref[ks, :], qt_ref[qi], preferred_element_type=F32)

    depth = len(bufs)

    def scores(ki):
        s = raw_scores(ki)
        bufs[ki % depth][...] = s
        return jnp.max(s, axis=0, keepdims=True)

    shift = scores(0)
    top = shift
    for ki in range(1, min(depth - 1, nk)):
        top = jnp.maximum(top, scores(ki))
    for ki in range(nk):
        if ki + depth - 1 < nk:
            top = jnp.maximum(top, scores(ki + depth - 1))
        pv = jnp.dot(vt_ref[ki], jnp.exp2(bufs[ki % depth][...] - shift).astype(BF16),
                     preferred_element_type=F32)
        acc1[...] = pv if ki == 0 else acc1[...] + pv
    flag_ref[0] = (jnp.max(top - shift) > SPEC_MARGIN).astype(jnp.int32)

    @pl.when(flag_ref[0] == 1)
    def _():
        m1[...] = jnp.full(m1.shape, M_INIT, F32)
        acc1[...] = jnp.zeros(acc1.shape, F32)

        def step(ki, carry):
            s = raw_scores(ki)
            _softmax_update(s, jnp.max(s, axis=0, keepdims=True), vt_ref[ki], m1, acc1)
            return carry

        lax.fori_loop(0, nk, step, 0)

    bg = bg_ref[rows, :].astype(F32)
    o_ref[rows, :] = (_normalised(acc1).T * (bg * jax.nn.sigmoid(bg))).astype(BF16)


def _attn_b(qt3, kb3, vt3, proj3, *, tq, tk):
    bsz, s, _ = kb3.shape
    nq, nk = s // tq, s // tk
    return pl.pallas_call(
        functools.partial(_attn_b_kernel, tq=tq, tk=tk, nq=nq, nk=nk),
        grid=(bsz, B_HEADS),
        in_specs=[
            pl.BlockSpec((nq, 2 * LANE, tq), lambda b, h: (b, h, 0)),
            pl.BlockSpec((None, s, 2 * LANE), lambda b, h: (b, 0, h)),
            pl.BlockSpec((nk, V_ROWS, tk), lambda b, h: (b, h, 0)),
            pl.BlockSpec((None, s, LANE), lambda b, h: (b, 0, COL_BG + h)),
        ],
        out_specs=pl.BlockSpec((None, s, LANE), lambda b, h: (b, 0, h)),
        out_shape=jax.ShapeDtypeStruct((bsz, s, B_WIDTH), BF16),
        scratch_shapes=[
            tuple(pltpu.VMEM((tk, tq), F32) for _ in range(SCORE_BUFS)),
            pltpu.VMEM((1, tq), F32), pltpu.VMEM((V_ROWS, tq), F32), pltpu.SMEM((1,), jnp.int32),
        ],
        compiler_params=pltpu.CompilerParams(
            dimension_semantics=("arbitrary", "arbitrary"), vmem_limit_bytes=VMEM_LIMIT),
        name="attn_b",
    )(qt3, kb3, vt3, proj3)


def _post_kernel(za_ref, zb_ref, ga_ref, gb_ref, x_ref, woa_ref, wob_ref, wout_ref, fg_ref, o_ref):
    ya = jnp.dot(za_ref[...], woa_ref[...], preferred_element_type=F32)
    yb = jnp.dot(zb_ref[...], wob_ref[...], preferred_element_type=F32)
    merged = (jax.nn.sigmoid(ga_ref[...].astype(F32)) * ya
              + jax.nn.sigmoid(gb_ref[...].astype(F32)) * yb)
    y = x_ref[...] + jnp.dot(merged.astype(BF16), wout_ref[...], preferred_element_type=F32)
    ms = jnp.mean(y * y, axis=-1, keepdims=True)
    o_ref[...] = y * lax.rsqrt(ms + NORM_EPS) * fg_ref[...]


def _post(za, zb, proj2, x2, woa, wob, wout, fg, *, tm=512):
    m = x2.shape[0]
    const = lambda shape: pl.BlockSpec(shape, lambda i: (0, 0), pipeline_mode=pl.Buffered(1))
    return pl.pallas_call(
        _post_kernel,
        grid=(m // tm,),
        in_specs=[
            pl.BlockSpec((tm, A_WIDTH), lambda i: (i, 0)),
            pl.BlockSpec((tm, B_WIDTH), lambda i: (i, 0)),
            pl.BlockSpec((tm, D_MODEL), lambda i: (i, COL_GA // 16)),
            pl.BlockSpec((tm, D_MODEL), lambda i: (i, COL_GB // 16)),
            pl.BlockSpec((tm, D_MODEL), lambda i: (i, 0)),
            const((A_WIDTH, D_MODEL)),
            const((B_WIDTH, D_MODEL)),
            const((D_MODEL, D_MODEL)),
            const((1, D_MODEL)),
        ],
        out_specs=pl.BlockSpec((tm, D_MODEL), lambda i: (i, 0)),
        out_shape=jax.ShapeDtypeStruct((m, D_MODEL), F32),
        compiler_params=pltpu.CompilerParams(
            dimension_semantics=("arbitrary",), vmem_limit_bytes=VMEM_LIMIT),
        name="post",
    )(za, zb, proj2, proj2, x2, woa, wob, wout, fg)


W_TN = 512
KR_COLS = _SIZES[6]


def _w_in_plan():
    off = dict(zip(("aq", "ak", "av", "ag", "cq", "ckv", "kr", "bg", "mg"),
                   [0] + _SPLITS))
    a_blk, b_blk, kind, scale = [], [], [], []
    for j in range(PROJ_COLS // W_TN):
        c0 = j * W_TN
        if c0 < COL_GA * LANE:
            src, k = off["aq"] + c0, 0
        elif c0 < COL_BG * LANE:
            src, k = off["mg"] + c0 - COL_GA * LANE, 1
        elif c0 < COL_CQ * LANE:
            src, k = off["bg"] + c0 - COL_BG * LANE, 1
        elif c0 < COL_KR * LANE:
            src, k = off["cq"] + c0 - COL_CQ * LANE, 0
        else:
            src, k = off["kr"], 2
        assert src % W_TN == (KR_COLS if k == 1 else 0)
        a_blk.append(src // W_TN)
        b_blk.append((src // W_TN + 1) * (W_TN // KR_COLS) if k == 1 else 0)
        kind.append(k)
        scale.append(A_HEAD_DIM ** -0.5 * LOG2E if c0 < COL_AK * LANE else 1.0)
    i32 = lambda v: jnp.asarray(v, jnp.int32)
    return i32(a_blk), i32(b_blk), i32(kind), jnp.asarray(scale, F32)


def _w_prep_kernel(ablk_ref, bblk_ref, kind_ref, scale_ref, a_ref, b_ref, o_ref):
    j = pl.program_id(0)
    kind = kind_ref[j]

    @pl.when(kind == 0)
    def _():
        o_ref[...] = (a_ref[...] * scale_ref[j]).T.astype(BF16)

    @pl.when(kind == 1)
    def _():
        t = jnp.concatenate([a_ref[KR_COLS:W_TN, :], b_ref[...]], axis=0)
        o_ref[...] = t.T.astype(BF16)

    @pl.when(kind == 2)
    def _():
        row = lax.broadcasted_iota(jnp.int32, (W_TN, 1), 0)
        o_ref[...] = jnp.where(row < KR_COLS, a_ref[...], 0.0).T.astype(BF16)


def _prep_w_in(w_t):
    d = w_t.shape[1]
    grid_spec = pltpu.PrefetchScalarGridSpec(
        num_scalar_prefetch=4,
        grid=(PROJ_COLS // W_TN,),
        in_specs=[
            pl.BlockSpec((W_TN, d), lambda j, a, b, k, s: (a[j], 0)),
            pl.BlockSpec((KR_COLS, d), lambda j, a, b, k, s: (b[j], 0)),
        ],
        out_specs=pl.BlockSpec((d, W_TN), lambda j, a, b, k, s: (0, j)),
    )
    return pl.pallas_call(
        _w_prep_kernel,
        grid_spec=grid_spec,
        out_shape=jax.ShapeDtypeStruct((d, PROJ_COLS), BF16),
        compiler_params=pltpu.CompilerParams(
            dimension_semantics=("arbitrary",), vmem_limit_bytes=VMEM_LIMIT),
        name="w_prep",
    )(*_w_in_plan(), w_t, w_t)


def _prep_w_uq_t(w):
    w = w.reshape(B_Q_LORA, B_HEADS, B_NOPE + B_ROPE) * ((B_NOPE + B_ROPE) ** -0.5 * LOG2E)
    w = jnp.pad(w, ((0, 0), (0, 0), (0, 2 * LANE - (B_NOPE + B_ROPE))))
    return w.reshape(B_Q_LORA, B_HEADS * 2 * LANE).T.astype(BF16)


def _prep_w_uv_t(w_ukv):
    w = w_ukv.reshape(B_KV_LORA, B_HEADS, B_NOPE + B_V_DIM)[:, :, B_NOPE:]
    w = jnp.pad(w, ((0, 0), (0, 0), (0, V_ROWS - B_V_DIM)))
    return w.reshape(B_KV_LORA, B_HEADS * V_ROWS).T.astype(BF16)


def _split3_bf16(c):
    c1 = c.astype(BF16).astype(F32)
    c2 = (c - c1).astype(BF16).astype(F32)
    c3 = (c - c1 - c2).astype(BF16).astype(F32)
    return c1, c2, c3


def _alibi_operands(positions, slopes, tq, tk):
    bsz, s = positions.shape
    nq, nk = s // tq, s // tk
    pmin = jnp.min(positions, axis=1, keepdims=True)
    rel = positions - pmin
    ok = ((jnp.min(rel, axis=1) >= 0) & (jnp.max(rel, axis=1) < 65536)
          & (jnp.max(jnp.abs(positions), axis=1) < (1 << 24)))
    qmin = jnp.min(positions.reshape(bsz, nq, tq), axis=-1)
    qmax = jnp.max(positions.reshape(bsz, nq, tq), axis=-1)
    kmin = jnp.min(positions.reshape(bsz, nk, tk), axis=-1)
    kmax = jnp.max(positions.reshape(bsz, nk, tk), axis=-1)
    after = qmin[:, :, None] >= kmax[:, None, :]
    before = qmax[:, :, None] <= kmin[:, None, :]
    mode = jnp.where(after, 1, jnp.where(before, -1, 0)) * ok[:, None, None].astype(jnp.int32)
    gap = jnp.where(after, qmin[:, :, None] - kmax[:, None, :], kmin[:, None, :] - qmax[:, :, None])
    dmin = jnp.where(mode != 0, gap, 0).astype(F32)
    off_diag = ~jnp.eye(nq, nk, dtype=bool)[None]
    fast = jnp.all((mode != 0) | ~off_diag, axis=-1) & (nq == nk)

    rel = jnp.where(ok[:, None], rel, 0)
    hi = (rel >> 8).astype(F32)
    lo = (rel & 255).astype(F32)

    g = np.arange(LANE) % A_HEAD_DIM
    first, second = g < N_BIAS // 2, (g >= N_BIAS // 2) & (g < N_BIAS)
    even, odd = g % 2 == 0, g % 2 == 1
    f32 = lambda mask: jnp.asarray(mask, F32)
    q_hi, q_lo, q_one = f32(first & even), f32(first & odd), f32(second)
    k_hi, k_lo, k_one = f32(second & even), f32(second & odd), f32(first)
    hi_q, lo_q = hi.reshape(bsz, nq, 1, tq), lo.reshape(bsz, nq, 1, tq)
    col = lambda v: v[None, None, :, None]
    qb_t = (hi_q * col(q_hi) + lo_q * col(q_lo) + col(q_one)).astype(BF16)
    kb = (hi[:, :, None] * k_hi[None, None, :] + lo[:, :, None] * k_lo[None, None, :]
          + k_one[None, None, :]).astype(BF16)

    cb = slopes * LOG2E
    pieces = _split3_bf16(cb)
    cpair = jnp.stack([v for c in pieces for v in (256.0 * c, c)], axis=-1)
    sel = np.zeros((N_BIAS // 2, LANE), np.float32)
    for i in range(N_BIAS // 2):
        sel[i, (g % (N_BIAS // 2) == i) & (g < N_BIAS)] = 1.0
    spread = jnp.dot(cpair, jnp.asarray(sel), precision=lax.Precision.HIGHEST)
    cq = (spread * f32(second) + f32(first))[:, :, None]
    ck = (-spread * f32(first) + f32(second))[:, None, :]
    return (mode.reshape(-1).astype(jnp.int32), fast.reshape(-1).astype(jnp.int32), cb.astype(F32),
            jnp.pad(dmin, ((0, 0), (0, 0), (0, LANE - nk)))[:, :, None, :],
            qb_t, kb, cq, ck)


def kernel(x, positions, norm_g, w_in, lam_q1, lam_k1, lam_q2, lam_k2, a_subln_g, w_oa, q_norm_g,
           w_uq, kv_norm_g, w_ukv, w_ob, w_out, final_g):
    bsz, s, d = x.shape
    m = bsz * s
    assert norm_g.shape[0] == 1 and d == D_MODEL and s % TQ == 0 and s % TK == 0
    layer = 0
    x2 = x.reshape(m, d)
    pos_f = positions.astype(F32)

    proj2 = _inproj(x2, norm_g[layer][None, :], _prep_w_in(w_in[layer].T))
    proj3 = proj2.reshape(bsz, s, PROJ_COLS)

    slopes = jnp.asarray([2.0 ** (-8.0 * (h + 1) / A_HEADS) for h in range(A_HEADS)], dtype=F32)
    mode, fast, cb, dmin, qb_t, kb, cq, ck = _alibi_operands(positions, slopes, TQ, TK)
    lamv = jnp.stack([lam_q1[layer], lam_k1[layer], lam_q2[layer], lam_k2[layer]]).astype(F32)
    za = _attn_a(mode, fast, cb, proj3, qb_t, kb, cq, ck, pos_f.reshape(bsz, s // TQ, 1, TQ),
                 pos_f[:, :, None], dmin, lamv, a_subln_g[layer][None, :], tq=TQ, tk=TK)

    half = B_ROPE // 2
    inv = ROPE_THETA ** (-jnp.arange(half, dtype=F32) / half)
    inv_lane = jnp.concatenate([inv, inv, jnp.zeros((LANE - B_ROPE,), F32)])[None, :]
    wuk = (w_ukv[layer].reshape(B_KV_LORA, B_HEADS, B_NOPE + B_V_DIM)[:, :, :B_NOPE]
           .reshape(B_KV_LORA, B_HEADS * B_NOPE).astype(BF16))
    vones = (jnp.arange(B_HEADS * V_ROWS) % V_ROWS == B_V_DIM).astype(F32)[:, None]
    qt3, kb2, vt3 = _mla_prep(proj2, pos_f.reshape(m, 1), pos_f.reshape(1, m), inv_lane, inv[:, None],
                              q_norm_g[layer][None, :], kv_norm_g[layer][None, :],
                              _prep_w_uq_t(w_uq[layer]), wuk, _prep_w_uv_t(w_ukv[layer]), vones,
                              tm=TQ, tkv=TK_B)
    zb = _attn_b(qt3, kb2.reshape(bsz, s, -1), vt3, proj3, tq=TQ, tk=TK_B)

    out = _post(za.reshape(m, A_WIDTH), zb.reshape(m, B_WIDTH), proj2, x2, w_oa[layer].astype(BF16),
                w_ob[layer].astype(BF16), w_out[layer].astype(BF16), final_g[None, :])
    return out.reshape(bsz, s, d)
```

```python
import functools
import math

import numpy as np
import jax
import jax.numpy as jnp
from jax import lax
from jax.experimental import pallas as pl
from jax.experimental.pallas import tpu as pltpu

F32 = jnp.float32
BF16 = jnp.bfloat16

D_MODEL = 2048
A_HEADS = 8
A_HEAD_DIM = 64
A_V_DIM = 128
A_WIDTH = 1024
B_HEADS = 8
B_Q_LORA = 512
B_KV_LORA = 512
B_NOPE = 128
B_ROPE = 64
B_V_DIM = 128
B_WIDTH = 1024
ROPE_THETA = 10000.0
NORM_EPS = 1e-6
SUBLN_EPS = 1e-5
LAM_INIT = 0.8 - 0.6 * math.exp(-0.3 * 0)
LOG2E = math.log2(math.e)

_SIZES = [1024, 1024, 1024, 1024, 512, 512, 64, 1024, 4096]
_SPLITS = [int(v) for v in np.cumsum(_SIZES)[:-1]]

LANE = 128
BF16_ROWS = 16
COL_AQ, COL_AK, COL_AV, COL_AG = 0, 8, 16, 24
COL_GA, COL_GB, COL_BG = 32, 48, 64
COL_CQ, COL_CKV, COL_KR = 72, 76, 80
PROJ_COLS = 10752

TQ = 512
TK = 512
TK_B = 512
SCORE_BUFS = 2
V_ROWS = A_V_DIM + BF16_ROWS
N_BIAS = 12
VMEM_LIMIT = 56 * 1024 * 1024
M_INIT = -1e30
SPEC_MARGIN = 60.0
SKIP_LOG2 = 160.0
SKIP_SLACK = 1.01


def _dot_nt(a, b):
    return lax.dot_general(a, b, (((1,), (1,)), ((), ())), preferred_element_type=F32)


def _inproj_kernel(x_ref, g_ref, w_ref, o_ref, h_ref, *, tm):
    @pl.when(pl.program_id(1) == 0)
    def _():
        def body(r, c):
            rows = pl.ds(pl.multiple_of(r * 128, 128), 128)
            x = x_ref[rows, :]
            ms = jnp.mean(x * x, axis=-1, keepdims=True)
            h_ref[rows, :] = (x * lax.rsqrt(ms + NORM_EPS) * g_ref[...]).astype(BF16)
            return c
        lax.fori_loop(0, tm // 128, body, 0)

    o_ref[...] = jnp.dot(h_ref[...], w_ref[...], preferred_element_type=F32).astype(BF16)


def _inproj(x2, g, w, *, tm=1024, tn=1536):
    m = x2.shape[0]
    return pl.pallas_call(
        functools.partial(_inproj_kernel, tm=tm),
        grid=(m // tm, PROJ_COLS // tn),
        in_specs=[
            pl.BlockSpec((tm, D_MODEL), lambda i, j: (i, 0)),
            pl.BlockSpec((1, D_MODEL), lambda i, j: (0, 0)),
            pl.BlockSpec((D_MODEL, tn), lambda i, j: (0, j)),
        ],
        out_specs=pl.BlockSpec((tm, tn), lambda i, j: (i, j)),
        out_shape=jax.ShapeDtypeStruct((m, PROJ_COLS), BF16),
        scratch_shapes=[pltpu.VMEM((tm, D_MODEL), BF16)],
        compiler_params=pltpu.CompilerParams(
            dimension_semantics=("arbitrary", "arbitrary"), vmem_limit_bytes=VMEM_LIMIT),
        name="inproj",
    )(x2, g, w)


def _softmax_update(s, s_max, vt, m_ref, acc_ref):
    m_old = m_ref[...]
    m_new = jnp.maximum(m_old, s_max)
    p = jnp.exp2(s - m_new)
    alpha = jnp.exp2(m_old - m_new)
    acc_ref[...] = alpha * acc_ref[...] + jnp.dot(vt, p.astype(BF16), preferred_element_type=F32)
    m_ref[...] = m_new


def _normalised(acc_ref):
    return acc_ref[0:A_V_DIM, :] / acc_ref[A_V_DIM:A_V_DIM + 1, :]


def _attn_a_kernel(mode_ref, fast_ref, cb_ref, q_ref, k_ref, v_ref, ag_ref, qbt_ref, kb_ref,
                   cq_ref, ck_ref, posq_ref, posk_ref, dmin_ref, lam_ref, subg_ref, o_ref,
                   qa1, qa2, vt, m1, m2, acc1, acc2, flag_ref, kn_ref, list_ref, *, tq, tk, nq, nk):
    b = pl.program_id(0)
    cb = cb_ref[pl.program_id(1)]

    ones_rows = (lax.broadcasted_iota(jnp.int32, (BF16_ROWS, tk), 0) == 0).astype(BF16)
    feat = lax.broadcasted_iota(jnp.int32, (LANE, LANE), 0)
    col = lax.broadcasted_iota(jnp.int32, (LANE, LANE), 1)
    half_sum = (col == jnp.where(feat < A_HEAD_DIM, 0, 1)).astype(BF16)
    lane = lax.broadcasted_iota(jnp.int32, (1, LANE), 1)
    kn = [jnp.zeros((1, LANE), F32), jnp.zeros((1, LANE), F32)]
    for ki in range(nk):
        vt[ki, 0:A_V_DIM, :] = v_ref[ki * tk:(ki + 1) * tk, :].astype(F32).T.astype(BF16)
        vt[ki, A_V_DIM:V_ROWS, :] = ones_rows
        k = k_ref[ki * tk:(ki + 1) * tk, :].astype(F32)
        nsq = jnp.dot((k * k).astype(BF16), half_sum, preferred_element_type=F32)
        nmax = jnp.sqrt(jnp.max(nsq, axis=0, keepdims=True))
        for m in range(2):
            norm_m = jnp.max(jnp.where(lane == m, nmax, 0.0), axis=1, keepdims=True)
            kn[m] = jnp.where(lane == ki, norm_m, kn[m])
    kn_ref[0:1, :] = kn[0]
    kn_ref[1:2, :] = kn[1]

    def q_tile(qi, c):
        _attn_a_q_tile(qi, b, cb, mode_ref, fast_ref, dmin_ref, q_ref, k_ref, ag_ref, qbt_ref, kb_ref,
                       cq_ref, ck_ref, posq_ref, posk_ref, lam_ref, subg_ref, o_ref,
                       qa1, qa2, vt, m1, m2, acc1, acc2, flag_ref, kn_ref, list_ref,
                       tq=tq, tk=tk, nq=nq, nk=nk)
        return c
    lax.fori_loop(0, nq, q_tile, 0)


def _attn_a_q_tile(qi, b, cb, mode_ref, fast_ref, dmin_ref, q_ref, k_ref, ag_ref, qbt_ref, kb_ref,
                   cq_ref, ck_ref, posq_ref, posk_ref, lam_ref, subg_ref, o_ref,
                   qa1, qa2, vt, m1, m2, acc1, acc2, flag_ref, kn_ref, list_ref, *, tq, tk, nq, nk):
    rows = pl.ds(pl.multiple_of(qi * tq, tq), tq)
    posq = posq_ref[qi]

    lo_r = lax.broadcasted_iota(jnp.int32, (LANE, 1), 0) < A_HEAD_DIM
    qs = q_ref[rows, :].astype(F32).T
    qbias = qbt_ref[qi].astype(F32) * cq_ref[...]
    qa1[...] = jnp.where(lo_r, qs, qbias).astype(BF16)
    qa2[...] = jnp.where(lo_r, qbias, qs).astype(BF16)
    lo_c = lax.broadcasted_iota(jnp.int32, (1, LANE), 1) < A_HEAD_DIM
    qsq = qs * qs
    qn1 = jnp.max(jnp.sqrt(jnp.sum(jnp.where(lo_r, qsq, 0.0), axis=0, keepdims=True)),
                  axis=1, keepdims=True)
    qn2 = jnp.max(jnp.sqrt(jnp.sum(jnp.where(lo_r, 0.0, qsq), axis=0, keepdims=True)),
                  axis=1, keepdims=True)

    def raw_scores(ki, sgn):
        ks = pl.ds(pl.multiple_of(ki * tk, tk), tk)
        k = k_ref[ks, :]
        kbias = (kb_ref[ks, :].astype(F32) * (ck_ref[...] * sgn)).astype(BF16)
        s1 = jnp.dot(jnp.where(lo_c, k, kbias), qa1[...], preferred_element_type=F32)
        s2 = jnp.dot(jnp.where(lo_c, kbias, k), qa2[...], preferred_element_type=F32)
        return s1, s2

    def explicit_bias(ki):
        ks = pl.ds(pl.multiple_of(ki * tk, tk), tk)
        return -cb * jnp.abs(posk_ref[ks, :] - posq)

    def tile_scores(ki, general):
        if general:
            s1, s2 = raw_scores(ki, 0.0)
            bias = explicit_bias(ki)
            return s1 + bias, s2 + bias
        return raw_scores(ki, mode_ref[(b * nq + qi) * nk + ki].astype(F32))

    def col_max(s):
        return jnp.max(s, axis=0, keepdims=True)

    def pv(ki, s, shift):
        return jnp.dot(vt[ki], jnp.exp2(s - shift).astype(BF16), preferred_element_type=F32)

    fast = fast_ref[b * nq + qi]
    flag_ref[0] = 1 - fast

    @pl.when(fast == 1)
    def _():
        s1, s2 = tile_scores(qi, True)
        sh1, sh2 = col_max(s1), col_max(s2)
        m1[...] = sh1
        m2[...] = sh2
        acc1[...] = pv(qi, s1, sh1)
        acc2[...] = pv(qi, s2, sh2)

        def add_tile(t, off):
            s1, s2 = tile_scores(t, False)
            m1[...] = jnp.maximum(m1[...], col_max(s1) - off)
            m2[...] = jnp.maximum(m2[...], col_max(s2) - off)
            acc1[...] += pv(t, s1, sh1 + off)
            acc2[...] += pv(t, s2, sh2 + off)

        lo1 = jnp.min(sh1, axis=1, keepdims=True)
        lo2 = jnp.min(sh2, axis=1, keepdims=True)
        gap = cb * dmin_ref[qi]
        ub1 = SKIP_SLACK * qn1 * kn_ref[0:1, :] + 1.0 - gap - lo1
        ub2 = SKIP_SLACK * qn2 * kn_ref[1:2, :] + 1.0 - gap - lo2
        near_a = jnp.where(qi == 0, 2, qi - 1)
        near_b = jnp.where(qi == nk - 1, nk - 3, qi + 1)
        lane = lax.broadcasted_iota(jnp.int32, (1, LANE), 1)
        rest = (lane != qi) & (lane != near_a) & (lane != near_b) & (lane < nk)
        keep = rest & ((ub1 > -SKIP_LOG2) | (ub2 > -SKIP_LOG2))
        bits = jnp.sum(jnp.where(keep, jnp.exp2(lane.astype(F32)), 0.0)).astype(jnp.int32)
        add_tile(near_a, 0.0)
        add_tile(near_b, 0.0)
        n = jnp.int32(0)
        for t in range(nk):
            list_ref[n] = jnp.int32(t)
            n = n + (lax.shift_right_logical(bits, t) & 1)
        list_ref[n] = qi

        n_rest = nk - 3
        short = n < n_rest

        @pl.when(short)
        def _():
            def pair(p, c):
                for e in range(2):
                    idx = 2 * p + e
                    add_tile(list_ref[idx], jnp.where(idx < n, 0.0, -M_INIT))
                return c
            lax.fori_loop(0, lax.shift_right_logical(n + 1, 1), pair, 0)

        @pl.when(jnp.logical_not(short))
        def _():
            for j in range(n_rest):
                add_tile(list_ref[j], 0.0)

        excess = jnp.max(jnp.maximum(m1[...] - sh1, m2[...] - sh2))
        flag_ref[0] = (excess > SPEC_MARGIN).astype(jnp.int32)

    @pl.when(flag_ref[0] == 1)
    def _():
        for m_ref in (m1, m2):
            m_ref[...] = jnp.full(m_ref.shape, M_INIT, F32)
        for z_ref in (acc1, acc2):
            z_ref[...] = jnp.zeros(z_ref.shape, F32)

        def update(ki, general):
            for s, m_ref, acc_ref in zip(tile_scores(ki, general), (m1, m2), (acc1, acc2)):
                _softmax_update(s, col_max(s), vt[ki], m_ref, acc_ref)

        def step(ki, carry):
            mode = mode_ref[(b * nq + qi) * nk + ki]

            @pl.when(mode == 0)
            def _():
                update(ki, True)

            @pl.when(mode != 0)
            def _():
                update(ki, False)

            return carry

        lax.fori_loop(0, nk, step, 0)

    lamv = lam_ref[...]
    lam = (jnp.exp(jnp.sum(lamv[0:1] * lamv[1:2], keepdims=True))
           - jnp.exp(jnp.sum(lamv[2:3] * lamv[3:4], keepdims=True)) + LAM_INIT)
    o = _normalised(acc1) - lam * _normalised(acc2)
    ms = jnp.mean(o * o, axis=0, keepdims=True)
    on = (o * lax.rsqrt(ms + SUBLN_EPS)).T * subg_ref[...] * (1.0 - LAM_INIT)
    ag = ag_ref[rows, :].astype(F32)
    o_ref[rows, :] = (on * (ag * jax.nn.sigmoid(ag))).astype(BF16)


def _attn_a(mode, fast, cb, proj3, qbias_t, kbias, cq, ck, posq_row, posk_col, dmin, lamv, subg,
            *, tq, tk):
    bsz, s, _ = proj3.shape
    nq, nk = s // tq, s // tk
    assert nk <= 24
    kern = functools.partial(_attn_a_kernel, tq=tq, tk=tk, nq=nq, nk=nk)
    stat = pltpu.VMEM((1, tq), F32)
    grid_spec = pltpu.PrefetchScalarGridSpec(
        num_scalar_prefetch=3,
        grid=(bsz, A_HEADS),
        in_specs=[
            pl.BlockSpec((None, s, LANE), lambda b, h, *_: (b, 0, COL_AQ + h)),
            pl.BlockSpec((None, s, LANE), lambda b, h, *_: (b, 0, COL_AK + h)),
            pl.BlockSpec((None, s, LANE), lambda b, h, *_: (b, 0, COL_AV + h)),
            pl.BlockSpec((None, s, LANE), lambda b, h, *_: (b, 0, COL_AG + h)),
            pl.BlockSpec((None, nq, LANE, tq), lambda b, h, *_: (b, 0, 0, 0)),
            pl.BlockSpec((None, s, LANE), lambda b, h, *_: (b, 0, 0)),
            pl.BlockSpec((None, LANE, 1), lambda b, h, *_: (h, 0, 0)),
            pl.BlockSpec((None, 1, LANE), lambda b, h, *_: (h, 0, 0)),
            pl.BlockSpec((None, nq, 1, tq), lambda b, h, *_: (b, 0, 0, 0)),
            pl.BlockSpec((None, s, 1), lambda b, h, *_: (b, 0, 0)),
            pl.BlockSpec((None, nq, 1, LANE), lambda b, h, *_: (b, 0, 0, 0)),
            pl.BlockSpec((4, A_HEAD_DIM), lambda b, h, *_: (0, 0)),
            pl.BlockSpec((1, A_V_DIM), lambda b, h, *_: (0, 0)),
        ],
        out_specs=pl.BlockSpec((None, s, LANE), lambda b, h, *_: (b, 0, h)),
        scratch_shapes=[
            pltpu.VMEM((LANE, tq), BF16), pltpu.VMEM((LANE, tq), BF16),
            pltpu.VMEM((nk, V_ROWS, tk), BF16),
            stat, stat,
            pltpu.VMEM((V_ROWS, tq), F32), pltpu.VMEM((V_ROWS, tq), F32),
            pltpu.SMEM((1,), jnp.int32), pltpu.VMEM((2, LANE), F32), pltpu.SMEM((nk + 1,), jnp.int32),
        ],
    )
    return pl.pallas_call(
        kern,
        grid_spec=grid_spec,
        out_shape=jax.ShapeDtypeStruct((bsz, s, A_WIDTH), BF16),
        compiler_params=pltpu.CompilerParams(
            dimension_semantics=("arbitrary", "arbitrary"), vmem_limit_bytes=VMEM_LIMIT),
        name="attn_a",
    )(mode, fast, cb, proj3, proj3, proj3, proj3, qbias_t, kbias, cq, ck, posq_row, posk_col, dmin,
      lamv, subg)


def _mla_prep_kernel(cq_ref, ckv_ref, kr_ref, posr_ref, invc_ref, qg_ref, kvg_ref,
                     wuqt_ref, wuk_ref, wuvt_ref, vones_ref, qo_ref, ko_ref, vo_ref):
    def norm(ref, g_ref):
        t = ref[...].astype(F32)
        ms = jnp.mean(t * t, axis=-1, keepdims=True)
        return (t * lax.rsqrt(ms + NORM_EPS) * g_ref[...]).astype(BF16)

    half = B_ROPE // 2
    cqn = norm(cq_ref, qg_ref)
    ckvn = norm(ckv_ref, kvg_ref)

    qt = _dot_nt(wuqt_ref[...], cqn)
    ang_t = invc_ref[...] * posr_ref[...]
    cs_t = jnp.cos(ang_t)
    sn_t = jnp.sin(ang_t)
    for h in range(B_HEADS):
        r0 = h * 2 * LANE
        qo_ref[r0:r0 + LANE, :] = qt[r0:r0 + LANE, :].astype(BF16)
        t1 = qt[r0 + LANE:r0 + LANE + half, :]
        t2 = qt[r0 + LANE + half:r0 + LANE + 2 * half, :]
        qo_ref[r0 + LANE:r0 + LANE + half, :] = (t1 * cs_t - t2 * sn_t).astype(BF16)
        qo_ref[r0 + LANE + half:r0 + LANE + 2 * half, :] = (t1 * sn_t + t2 * cs_t).astype(BF16)
        qo_ref[r0 + LANE + 2 * half:r0 + 2 * LANE, :] = jnp.zeros((LANE - 2 * half, qt.shape[1]), BF16)

    kr_t = kr_ref[...].astype(F32).T
    t1, t2 = kr_t[0:half, :], kr_t[half:2 * half, :]
    krope = jnp.concatenate(
        [t1 * cs_t - t2 * sn_t, t1 * sn_t + t2 * cs_t, kr_t[2 * half:LANE, :]], axis=0
    ).T.astype(BF16)
    kf = jnp.dot(ckvn, wuk_ref[...], preferred_element_type=F32)
    for h in range(B_HEADS):
        c0 = h * 2 * LANE
        ko_ref[:, c0:c0 + LANE] = kf[:, h * LANE:(h + 1) * LANE].astype(BF16)
        ko_ref[:, c0 + LANE:c0 + 2 * LANE] = krope

    vt = (_dot_nt(wuvt_ref[...], ckvn) + vones_ref[...]).astype(BF16)
    tkv = vo_ref.shape[2]
    for c in range(vo_ref.shape[0]):
        vo_ref[c] = vt[:, c * tkv:(c + 1) * tkv]


def _mla_prep(proj2, pos_row, inv_col, qg, kvg, wuqt, wuk, wuvt, vones, *, tm, tkv):
    m = proj2.shape[0]
    wq = B_HEADS * 2 * LANE
    wv = B_HEADS * V_ROWS
    return pl.pallas_call(
        _mla_prep_kernel,
        grid=(m // tm,),
        in_specs=[
            pl.BlockSpec((tm, B_Q_LORA), lambda i: (i, COL_CQ // 4)),
            pl.BlockSpec((tm, B_KV_LORA), lambda i: (i, COL_CKV // 4)),
            pl.BlockSpec((tm, LANE), lambda i: (i, COL_KR)),
            pl.BlockSpec((1, tm), lambda i: (0, i)),
            pl.BlockSpec((B_ROPE // 2, 1), lambda i: (0, 0)),
            pl.BlockSpec((1, B_Q_LORA), lambda i: (0, 0)),
            pl.BlockSpec((1, B_KV_LORA), lambda i: (0, 0)),
            pl.BlockSpec((wq, B_Q_LORA), lambda i: (0, 0)),
            pl.BlockSpec((B_KV_LORA, B_HEADS * B_NOPE), lambda i: (0, 0)),
            pl.BlockSpec((wv, B_KV_LORA), lambda i: (0, 0)),
            pl.BlockSpec((wv, 1), lambda i: (0, 0)),
        ],
        out_specs=[
            pl.BlockSpec((None, wq, tm), lambda i: (i, 0, 0)),
            pl.BlockSpec((tm, wq), lambda i: (i, 0)),
            pl.BlockSpec((tm // tkv, wv, tkv), lambda i: (i, 0, 0)),
        ],
        out_shape=[
            jax.ShapeDtypeStruct((m // tm, wq, tm), BF16),
            jax.ShapeDtypeStruct((m, wq), BF16),
            jax.ShapeDtypeStruct((m // tkv, wv, tkv), BF16),
        ],
        compiler_params=pltpu.CompilerParams(
            dimension_semantics=("arbitrary",), vmem_limit_bytes=VMEM_LIMIT),
        name="mla_prep",
    )(proj2, proj2, proj2, pos_row, inv_col, qg, kvg, wuqt, wuk, wuvt, vones)


def _attn_b_kernel(qt_ref, k_ref, vt_ref, bg_ref, o_ref, bufs, m1, acc1, flag_ref, *, tq, tk, nq, nk):
    def q_tile(qi, c):
        _attn_b_q_tile(qi, qt_ref, k_ref, vt_ref, bg_ref, o_ref, bufs, m1, acc1, flag_ref,
                       tq=tq, tk=tk, nk=nk)
        return c
    lax.fori_loop(0, nq, q_tile, 0)


def _attn_b_q_tile(qi, qt_ref, k_ref, vt_ref, bg_ref, o_ref, bufs, m1, acc1, flag_ref, *, tq, tk, nk):
    rows = pl.ds(pl.multiple_of(qi * tq, tq), tq)

    def raw_scores(ki):
        ks = pl.ds(pl.multiple_of(ki * tk, tk), tk)
        return jnp.dot(k_ref[ks, :], qt_ref[qi], preferred_element_type=F32)

    depth = len(bufs)

    def scores(ki):
        s = raw_scores(ki)
        bufs[ki % depth][...] = s
        return jnp.max(s, axis=0, keepdims=True)

    shift = scores(0)
    top = shift
    for ki in range(1, min(depth - 1, nk)):
        top = jnp.maximum(top, scores(ki))
    for ki in range(nk):
        if ki + depth - 1 < nk:
            top = jnp.maximum(top, scores(ki + depth - 1))
        pv = jnp.dot(vt_ref[ki], jnp.exp2(bufs[ki % depth][...] - shift).astype(BF16),
                     preferred_element_type=F32)
        acc1[...] = pv if ki == 0 else acc1[...] + pv
    flag_ref[0] = (jnp.max(top - shift) > SPEC_MARGIN).astype(jnp.int32)

    @pl.when(flag_ref[0] == 1)
    def _():
        m1[...] = jnp.full(m1.shape, M_INIT, F32)
        acc1[...] = jnp.zeros(acc1.shape, F32)

        def step(ki, carry):
            s = raw_scores(ki)
            _softmax_update(s, jnp.max(s, axis=0, keepdims=True), vt_ref[ki], m1, acc1)
            return carry

        lax.fori_loop(0, nk, step, 0)

    bg = bg_ref[rows, :].astype(F32)
    o_ref[rows, :] = (_normalised(acc1).T * (bg * jax.nn.sigmoid(bg))).astype(BF16)


def _attn_b(qt3, kb3, vt3, proj3, *, tq, tk):
    bsz, s, _ = kb3.shape
    nq, nk = s // tq, s // tk
    return pl.pallas_call(
        functools.partial(_attn_b_kernel, tq=tq, tk=tk, nq=nq, nk=nk),
        grid=(bsz, B_HEADS),
        in_specs=[
            pl.BlockSpec((nq, 2 * LANE, tq), lambda b, h: (b, h, 0)),
            pl.BlockSpec((None, s, 2 * LANE), lambda b, h: (b, 0, h)),
            pl.BlockSpec((nk, V_ROWS, tk), lambda b, h: (b, h, 0)),
            pl.BlockSpec((None, s, LANE), lambda b, h: (b, 0, COL_BG + h)),
        ],
        out_specs=pl.BlockSpec((None, s, LANE), lambda b, h: (b, 0, h)),
        out_shape=jax.ShapeDtypeStruct((bsz, s, B_WIDTH), BF16),
        scratch_shapes=[
            tuple(pltpu.VMEM((tk, tq), F32) for _ in range(SCORE_BUFS)),
            pltpu.VMEM((1, tq), F32), pltpu.VMEM((V_ROWS, tq), F32), pltpu.SMEM((1,), jnp.int32),
        ],
        compiler_params=pltpu.CompilerParams(
            dimension_semantics=("arbitrary", "arbitrary"), vmem_limit_bytes=VMEM_LIMIT),
        name="attn_b",
    )(qt3, kb3, vt3, proj3)


def _post_kernel(za_ref, zb_ref, ga_ref, gb_ref, x_ref, woa_ref, wob_ref, wout_ref, fg_ref, o_ref):
    ya = jnp.dot(za_ref[...], woa_ref[...], preferred_element_type=F32)
    yb = jnp.dot(zb_ref[...], wob_ref[...], preferred_element_type=F32)
    merged = (jax.nn.sigmoid(ga_ref[...].astype(F32)) * ya
              + jax.nn.sigmoid(gb_ref[...].astype(F32)) * yb)
    y = x_ref[...] + jnp.dot(merged.astype(BF16), wout_ref[...], preferred_element_type=F32)
    ms = jnp.mean(y * y, axis=-1, keepdims=True)
    o_ref[...] = y * lax.rsqrt(ms + NORM_EPS) * fg_ref[...]


def _post(za, zb, proj2, x2, woa, wob, wout, fg, *, tm=512):
    m = x2.shape[0]
    const = lambda shape: pl.BlockSpec(shape, lambda i: (0, 0), pipeline_mode=pl.Buffered(1))
    return pl.pallas_call(
        _post_kernel,
        grid=(m // tm,),
        in_specs=[
            pl.BlockSpec((tm, A_WIDTH), lambda i: (i, 0)),
            pl.BlockSpec((tm, B_WIDTH), lambda i: (i, 0)),
            pl.BlockSpec((tm, D_MODEL), lambda i: (i, COL_GA // 16)),
            pl.BlockSpec((tm, D_MODEL), lambda i: (i, COL_GB // 16)),
            pl.BlockSpec((tm, D_MODEL), lambda i: (i, 0)),
            const((A_WIDTH, D_MODEL)),
            const((B_WIDTH, D_MODEL)),
            const((D_MODEL, D_MODEL)),
            const((1, D_MODEL)),
        ],
        out_specs=pl.BlockSpec((tm, D_MODEL), lambda i: (i, 0)),
        out_shape=jax.ShapeDtypeStruct((m, D_MODEL), F32),
        compiler_params=pltpu.CompilerParams(
            dimension_semantics=("arbitrary",), vmem_limit_bytes=VMEM_LIMIT),
        name="post",
    )(za, zb, proj2, proj2, x2, woa, wob, wout, fg)


W_TN = 512
KR_COLS = _SIZES[6]


def _w_in_plan():
    off = dict(zip(("aq", "ak", "av", "ag", "cq", "ckv", "kr", "bg", "mg"),
                   [0] + _SPLITS))
    a_blk, b_blk, kind, scale = [], [], [], []
    for j in range(PROJ_COLS // W_TN):
        c0 = j * W_TN
        if c0 < COL_GA * LANE:
            src, k = off["aq"] + c0, 0
        elif c0 < COL_BG * LANE:
            src, k = off["mg"] + c0 - COL_GA * LANE, 1
        elif c0 < COL_CQ * LANE:
            src, k = off["bg"] + c0 - COL_BG * LANE, 1
        elif c0 < COL_KR * LANE:
            src, k = off["cq"] + c0 - COL_CQ * LANE, 0
        else:
            src, k = off["kr"], 2
        assert src % W_TN == (KR_COLS if k == 1 else 0)
        a_blk.append(src // W_TN)
        b_blk.append((src // W_TN + 1) * (W_TN // KR_COLS) if k == 1 else 0)
        kind.append(k)
        scale.append(A_HEAD_DIM ** -0.5 * LOG2E if c0 < COL_AK * LANE else 1.0)
    i32 = lambda v: jnp.asarray(v, jnp.int32)
    return i32(a_blk), i32(b_blk), i32(kind), jnp.asarray(scale, F32)


def _w_prep_kernel(ablk_ref, bblk_ref, kind_ref, scale_ref, a_ref, b_ref, o_ref):
    j = pl.program_id(0)
    kind = kind_ref[j]

    @pl.when(kind == 0)
    def _():
        o_ref[...] = (a_ref[...] * scale_ref[j]).T.astype(BF16)

    @pl.when(kind == 1)
    def _():
        t = jnp.concatenate([a_ref[KR_COLS:W_TN, :], b_ref[...]], axis=0)
        o_ref[...] = t.T.astype(BF16)

    @pl.when(kind == 2)
    def _():
        row = lax.broadcasted_iota(jnp.int32, (W_TN, 1), 0)
        o_ref[...] = jnp.where(row < KR_COLS, a_ref[...], 0.0).T.astype(BF16)


def _prep_w_in(w_t):
    d = w_t.shape[1]
    grid_spec = pltpu.PrefetchScalarGridSpec(
        num_scalar_prefetch=4,
        grid=(PROJ_COLS // W_TN,),
        in_specs=[
            pl.BlockSpec((W_TN, d), lambda j, a, b, k, s: (a[j], 0)),
            pl.BlockSpec((KR_COLS, d), lambda j, a, b, k, s: (b[j], 0)),
        ],
        out_specs=pl.BlockSpec((d, W_TN), lambda j, a, b, k, s: (0, j)),
    )
    return pl.pallas_call(
        _w_prep_kernel,
        grid_spec=grid_spec,
        out_shape=jax.ShapeDtypeStruct((d, PROJ_COLS), BF16),
        compiler_params=pltpu.CompilerParams(
            dimension_semantics=("arbitrary",), vmem_limit_bytes=VMEM_LIMIT),
        name="w_prep",
    )(*_w_in_plan(), w_t, w_t)


def _prep_w_uq_t(w):
    w = w.reshape(B_Q_LORA, B_HEADS, B_NOPE + B_ROPE) * ((B_NOPE + B_ROPE) ** -0.5 * LOG2E)
    w = jnp.pad(w, ((0, 0), (0, 0), (0, 2 * LANE - (B_NOPE + B_ROPE))))
    return w.reshape(B_Q_LORA, B_HEADS * 2 * LANE).T.astype(BF16)


def _prep_w_uv_t(w_ukv):
    w = w_ukv.reshape(B_KV_LORA, B_HEADS, B_NOPE + B_V_DIM)[:, :, B_NOPE:]
    w = jnp.pad(w, ((0, 0), (0, 0), (0, V_ROWS - B_V_DIM)))
    return w.reshape(B_KV_LORA, B_HEADS * V_ROWS).T.astype(BF16)


def _split3_bf16(c):
    c1 = c.astype(BF16).astype(F32)
    c2 = (c - c1).astype(BF16).astype(F32)
    c3 = (c - c1 - c2).astype(BF16).astype(F32)
    return c1, c2, c3


def _alibi_operands(positions, slopes, tq, tk):
    bsz, s = positions.shape
    nq, nk = s // tq, s // tk
    pmin = jnp.min(positions, axis=1, keepdims=True)
    rel = positions - pmin
    ok = ((jnp.min(rel, axis=1) >= 0) & (jnp.max(rel, axis=1) < 65536)
          & (jnp.max(jnp.abs(positions), axis=1) < (1 << 24)))
    qmin = jnp.min(positions.reshape(bsz, nq, tq), axis=-1)
    qmax = jnp.max(positions.reshape(bsz, nq, tq), axis=-1)
    kmin = jnp.min(positions.reshape(bsz, nk, tk), axis=-1)
    kmax = jnp.max(positions.reshape(bsz, nk, tk), axis=-1)
    after = qmin[:, :, None] >= kmax[:, None, :]
    before = qmax[:, :, None] <= kmin[:, None, :]
    mode = jnp.where(after, 1, jnp.where(before, -1, 0)) * ok[:, None, None].astype(jnp.int32)
    gap = jnp.where(after, qmin[:, :, None] - kmax[:, None, :], kmin[:, None, :] - qmax[:, :, None])
    dmin = jnp.where(mode != 0, gap, 0).astype(F32)
    off_diag = ~jnp.eye(nq, nk, dtype=bool)[None]
    fast = jnp.all((mode != 0) | ~off_diag, axis=-1) & (nq == nk) & (nk >= 4)

    rel = jnp.where(ok[:, None], rel, 0)
    hi = (rel >> 8).astype(F32)
    lo = (rel & 255).astype(F32)

    g = np.arange(LANE) % A_HEAD_DIM
    first, second = g < N_BIAS // 2, (g >= N_BIAS // 2) & (g < N_BIAS)
    even, odd = g % 2 == 0, g % 2 == 1
    f32 = lambda mask: jnp.asarray(mask, F32)
    q_hi, q_lo, q_one = f32(first & even), f32(first & odd), f32(second)
    k_hi, k_lo, k_one = f32(second & even), f32(second & odd), f32(first)
    hi_q, lo_q = hi.reshape(bsz, nq, 1, tq), lo.reshape(bsz, nq, 1, tq)
    col = lambda v: v[None, None, :, None]
    qb_t = (hi_q * col(q_hi) + lo_q * col(q_lo) + col(q_one)).astype(BF16)
    kb = (hi[:, :, None] * k_hi[None, None, :] + lo[:, :, None] * k_lo[None, None, :]
          + k_one[None, None, :]).astype(BF16)

    cb = slopes * LOG2E
    pieces = _split3_bf16(cb)
    cpair = jnp.stack([v for c in pieces for v in (256.0 * c, c)], axis=-1)
    sel = np.zeros((N_BIAS // 2, LANE), np.float32)
    for i in range(N_BIAS // 2):
        sel[i, (g % (N_BIAS // 2) == i) & (g < N_BIAS)] = 1.0
    spread = jnp.dot(cpair, jnp.asarray(sel), precision=lax.Precision.HIGHEST)
    cq = (spread * f32(second) + f32(first))[:, :, None]
    ck = (-spread * f32(first) + f32(second))[:, None, :]
    return (mode.reshape(-1).astype(jnp.int32), fast.reshape(-1).astype(jnp.int32), cb.astype(F32),
            jnp.pad(dmin, ((0, 0), (0, 0), (0, LANE - nk)))[:, :, None, :],
            qb_t, kb, cq, ck)


def kernel(x, positions, norm_g, w_in, lam_q1, lam_k1, lam_q2, lam_k2, a_subln_g, w_oa, q_norm_g,
           w_uq, kv_norm_g, w_ukv, w_ob, w_out, final_g):
    bsz, s, d = x.shape
    m = bsz * s
    assert norm_g.shape[0] == 1 and d == D_MODEL and s % TQ == 0 and s % TK == 0
    layer = 0
    x2 = x.reshape(m, d)
    pos_f = positions.astype(F32)

    proj2 = _inproj(x2, norm_g[layer][None, :], _prep_w_in(w_in[layer].T))
    proj3 = proj2.reshape(bsz, s, PROJ_COLS)

    slopes = jnp.asarray([2.0 ** (-8.0 * (h + 1) / A_HEADS) for h in range(A_HEADS)], dtype=F32)
    mode, fast, cb, dmin, qb_t, kb, cq, ck = _alibi_operands(positions, slopes, TQ, TK)
    lamv = jnp.stack([lam_q1[layer], lam_k1[layer], lam_q2[layer], lam_k2[layer]]).astype(F32)
    za = _attn_a(mode, fast, cb, proj3, qb_t, kb, cq, ck, pos_f.reshape(bsz, s // TQ, 1, TQ),
                 pos_f[:, :, None], dmin, lamv, a_subln_g[layer][None, :], tq=TQ, tk=TK)

    half = B_ROPE // 2
    inv = ROPE_THETA ** (-jnp.arange(half, dtype=F32) / half)
    wuk = (w_ukv[layer].reshape(B_KV_LORA, B_HEADS, B_NOPE + B_V_DIM)[:, :, :B_NOPE]
           .reshape(B_KV_LORA, B_HEADS * B_NOPE).astype(BF16))
    vones = (jnp.arange(B_HEADS * V_ROWS) % V_ROWS == B_V_DIM).astype(F32)[:, None]
    qt3, kb2, vt3 = _mla_prep(proj2, pos_f.reshape(1, m), inv[:, None],
                              q_norm_g[layer][None, :], kv_norm_g[layer][None, :],
                              _prep_w_uq_t(w_uq[layer]), wuk, _prep_w_uv_t(w_ukv[layer]), vones,
                              tm=TQ, tkv=TK_B)
    zb = _attn_b(qt3, kb2.reshape(bsz, s, -1), vt3, proj3, tq=TQ, tk=TK_B)

    out = _post(za.reshape(m, A_WIDTH), zb.reshape(m, B_WIDTH), proj2, x2, w_oa[layer].astype(BF16),
                w_ob[layer].astype(BF16), w_out[layer].astype(BF16), final_g[None, :])
    return out.reshape(bsz, s, d)
```

```python
import functools
import math

import numpy as np
import jax
import jax.numpy as jnp
from jax import lax
from jax.experimental import pallas as pl
from jax.experimental.pallas import tpu as pltpu

F32 = jnp.float32
BF16 = jnp.bfloat16

D_MODEL = 2048
A_HEADS = 8
A_HEAD_DIM = 64
A_V_DIM = 128
A_WIDTH = 1024
B_HEADS = 8
B_Q_LORA = 512
B_KV_LORA = 512
B_NOPE = 128
B_ROPE = 64
B_V_DIM = 128
B_WIDTH = 1024
ROPE_THETA = 10000.0
NORM_EPS = 1e-6
SUBLN_EPS = 1e-5
LAM_INIT = 0.8 - 0.6 * math.exp(-0.3 * 0)
LOG2E = math.log2(math.e)

_SIZES = [1024, 1024, 1024, 1024, 512, 512, 64, 1024, 4096]
_SPLITS = [int(v) for v in np.cumsum(_SIZES)[:-1]]

LANE = 128
BF16_ROWS = 16
COL_AQ, COL_AK, COL_AV, COL_AG = 0, 8, 16, 24
COL_GA, COL_GB, COL_BG = 32, 48, 64
COL_CQ, COL_CKV, COL_KR = 72, 76, 80
PROJ_COLS = 10752

TQ = 512
TK = 512
TQ_B = 1024
TK_B = 512
SCORE_BUFS = 2
V_ROWS = A_V_DIM + BF16_ROWS
N_BIAS = 12
VMEM_LIMIT = 56 * 1024 * 1024
M_INIT = -1e30
SPEC_MARGIN = 60.0
SKIP_LOG2 = 160.0
SKIP_SLACK = 1.01


def _dot_nt(a, b):
    return lax.dot_general(a, b, (((1,), (1,)), ((), ())), preferred_element_type=F32)


def _inproj_kernel(x_ref, g_ref, w_ref, o_ref, h_ref, *, tm):
    @pl.when(pl.program_id(1) == 0)
    def _():
        def body(r, c):
            rows = pl.ds(pl.multiple_of(r * 128, 128), 128)
            x = x_ref[rows, :]
            ms = jnp.mean(x * x, axis=-1, keepdims=True)
            h_ref[rows, :] = (x * lax.rsqrt(ms + NORM_EPS) * g_ref[...]).astype(BF16)
            return c
        lax.fori_loop(0, tm // 128, body, 0)

    o_ref[...] = jnp.dot(h_ref[...], w_ref[...], preferred_element_type=F32).astype(BF16)


def _inproj(x2, g, w, *, tm=1024, tn=1536):
    m = x2.shape[0]
    return pl.pallas_call(
        functools.partial(_inproj_kernel, tm=tm),
        grid=(m // tm, PROJ_COLS // tn),
        in_specs=[
            pl.BlockSpec((tm, D_MODEL), lambda i, j: (i, 0)),
            pl.BlockSpec((1, D_MODEL), lambda i, j: (0, 0)),
            pl.BlockSpec((D_MODEL, tn), lambda i, j: (0, j)),
        ],
        out_specs=pl.BlockSpec((tm, tn), lambda i, j: (i, j)),
        out_shape=jax.ShapeDtypeStruct((m, PROJ_COLS), BF16),
        scratch_shapes=[pltpu.VMEM((tm, D_MODEL), BF16)],
        compiler_params=pltpu.CompilerParams(
            dimension_semantics=("arbitrary", "arbitrary"), vmem_limit_bytes=VMEM_LIMIT),
        name="inproj",
    )(x2, g, w)


def _softmax_update(s, s_max, vt, m_ref, acc_ref):
    m_old = m_ref[...]
    m_new = jnp.maximum(m_old, s_max)
    p = jnp.exp2(s - m_new)
    alpha = jnp.exp2(m_old - m_new)
    acc_ref[...] = alpha * acc_ref[...] + jnp.dot(vt, p.astype(BF16), preferred_element_type=F32)
    m_ref[...] = m_new


def _normalised(acc_ref):
    return acc_ref[0:A_V_DIM, :] / acc_ref[A_V_DIM:A_V_DIM + 1, :]


def _attn_a_kernel(mode_ref, fast_ref, cb_ref, q_ref, k_ref, v_ref, ag_ref, qbt_ref, kb_ref,
                   cq_ref, ck_ref, posq_ref, posk_ref, dmin_ref, lam_ref, subg_ref, o_ref,
                   qa1, qa2, vt, m1, m2, acc1, acc2, flag_ref, kn_ref, list_ref, *, tq, tk, nq, nk):
    b = pl.program_id(0)
    cb = cb_ref[pl.program_id(1)]

    ones_rows = (lax.broadcasted_iota(jnp.int32, (BF16_ROWS, tk), 0) == 0).astype(BF16)
    feat = lax.broadcasted_iota(jnp.int32, (LANE, LANE), 0)
    col = lax.broadcasted_iota(jnp.int32, (LANE, LANE), 1)
    half_sum = (col == jnp.where(feat < A_HEAD_DIM, 0, 1)).astype(BF16)
    lane = lax.broadcasted_iota(jnp.int32, (1, LANE), 1)
    kn = [jnp.zeros((1, LANE), F32), jnp.zeros((1, LANE), F32)]
    for ki in range(nk):
        vt[ki, 0:A_V_DIM, :] = v_ref[ki * tk:(ki + 1) * tk, :].astype(F32).T.astype(BF16)
        vt[ki, A_V_DIM:V_ROWS, :] = ones_rows
        k = k_ref[ki * tk:(ki + 1) * tk, :].astype(F32)
        nsq = jnp.dot((k * k).astype(BF16), half_sum, preferred_element_type=F32)
        nmax = jnp.sqrt(jnp.max(nsq, axis=0, keepdims=True))
        for m in range(2):
            norm_m = jnp.max(jnp.where(lane == m, nmax, 0.0), axis=1, keepdims=True)
            kn[m] = jnp.where(lane == ki, norm_m, kn[m])
    kn_ref[0:1, :] = kn[0]
    kn_ref[1:2, :] = kn[1]

    def q_tile(qi, c):
        _attn_a_q_tile(qi, b, cb, mode_ref, fast_ref, dmin_ref, q_ref, k_ref, ag_ref, qbt_ref, kb_ref,
                       cq_ref, ck_ref, posq_ref, posk_ref, lam_ref, subg_ref, o_ref,
                       qa1, qa2, vt, m1, m2, acc1, acc2, flag_ref, kn_ref, list_ref,
                       tq=tq, tk=tk, nq=nq, nk=nk)
        return c
    lax.fori_loop(0, nq, q_tile, 0)


def _attn_a_q_tile(qi, b, cb, mode_ref, fast_ref, dmin_ref, q_ref, k_ref, ag_ref, qbt_ref, kb_ref,
                   cq_ref, ck_ref, posq_ref, posk_ref, lam_ref, subg_ref, o_ref,
                   qa1, qa2, vt, m1, m2, acc1, acc2, flag_ref, kn_ref, list_ref, *, tq, tk, nq, nk):
    rows = pl.ds(pl.multiple_of(qi * tq, tq), tq)
    posq = posq_ref[qi]

    lo_r = lax.broadcasted_iota(jnp.int32, (LANE, 1), 0) < A_HEAD_DIM
    qs = q_ref[rows, :].astype(F32).T
    qbias = qbt_ref[qi].astype(F32) * cq_ref[...]
    qa1[...] = jnp.where(lo_r, qs, qbias).astype(BF16)
    qa2[...] = jnp.where(lo_r, qbias, qs).astype(BF16)
    lo_c = lax.broadcasted_iota(jnp.int32, (1, LANE), 1) < A_HEAD_DIM
    qsq = qs * qs
    qn1 = jnp.max(jnp.sqrt(jnp.sum(jnp.where(lo_r, qsq, 0.0), axis=0, keepdims=True)),
                  axis=1, keepdims=True)
    qn2 = jnp.max(jnp.sqrt(jnp.sum(jnp.where(lo_r, 0.0, qsq), axis=0, keepdims=True)),
                  axis=1, keepdims=True)

    def raw_scores(ki, sgn):
        ks = pl.ds(pl.multiple_of(ki * tk, tk), tk)
        k = k_ref[ks, :]
        kbias = (kb_ref[ks, :].astype(F32) * (ck_ref[...] * sgn)).astype(BF16)
        s1 = jnp.dot(jnp.where(lo_c, k, kbias), qa1[...], preferred_element_type=F32)
        s2 = jnp.dot(jnp.where(lo_c, kbias, k), qa2[...], preferred_element_type=F32)
        return s1, s2

    def explicit_bias(ki):
        ks = pl.ds(pl.multiple_of(ki * tk, tk), tk)
        return -cb * jnp.abs(posk_ref[ks, :] - posq)

    def tile_scores(ki, general):
        if general:
            s1, s2 = raw_scores(ki, 0.0)
            bias = explicit_bias(ki)
            return s1 + bias, s2 + bias
        return raw_scores(ki, mode_ref[(b * nq + qi) * nk + ki].astype(F32))

    def col_max(s):
        return jnp.max(s, axis=0, keepdims=True)

    def pv(ki, s, shift):
        return jnp.dot(vt[ki], jnp.exp2(s - shift).astype(BF16), preferred_element_type=F32)

    fast = fast_ref[b * nq + qi]
    flag_ref[0] = 1 - fast

    @pl.when(fast == 1)
    def _():
        s1, s2 = tile_scores(qi, True)
        sh1, sh2 = col_max(s1), col_max(s2)
        m1[...] = sh1
        m2[...] = sh2
        acc1[...] = pv(qi, s1, sh1)
        acc2[...] = pv(qi, s2, sh2)

        def add_tile(t, off):
            s1, s2 = tile_scores(t, False)
            m1[...] = jnp.maximum(m1[...], col_max(s1) - off)
            m2[...] = jnp.maximum(m2[...], col_max(s2) - off)
            acc1[...] += pv(t, s1, sh1 + off)
            acc2[...] += pv(t, s2, sh2 + off)

        lo1 = jnp.min(sh1, axis=1, keepdims=True)
        lo2 = jnp.min(sh2, axis=1, keepdims=True)
        gap = cb * dmin_ref[qi]
        ub1 = SKIP_SLACK * qn1 * kn_ref[0:1, :] + 1.0 - gap - lo1
        ub2 = SKIP_SLACK * qn2 * kn_ref[1:2, :] + 1.0 - gap - lo2
        near_a = jnp.where(qi == 0, 2, qi - 1)
        near_b = jnp.where(qi == nk - 1, nk - 3, qi + 1)
        lane = lax.broadcasted_iota(jnp.int32, (1, LANE), 1)
        rest = (lane != qi) & (lane != near_a) & (lane != near_b) & (lane < nk)
        keep = rest & ((ub1 > -SKIP_LOG2) | (ub2 > -SKIP_LOG2))
        bits = jnp.sum(jnp.where(keep, jnp.exp2(lane.astype(F32)), 0.0)).astype(jnp.int32)
        add_tile(near_a, 0.0)
        add_tile(near_b, 0.0)
        n = jnp.int32(0)
        for t in range(nk):
            list_ref[n] = jnp.int32(t)
            n = n + (lax.shift_right_logical(bits, t) & 1)
        list_ref[n] = qi

        n_rest = nk - 3
        short = n < n_rest

        @pl.when(short)
        def _():
            def pair(p, c):
                for e in range(2):
                    idx = 2 * p + e
                    add_tile(list_ref[idx], jnp.where(idx < n, 0.0, -M_INIT))
                return c
            lax.fori_loop(0, lax.shift_right_logical(n + 1, 1), pair, 0)

        @pl.when(jnp.logical_not(short))
        def _():
            for j in range(n_rest):
                add_tile(list_ref[j], 0.0)

        excess = jnp.max(jnp.maximum(m1[...] - sh1, m2[...] - sh2))
        flag_ref[0] = (excess > SPEC_MARGIN).astype(jnp.int32)

    @pl.when(flag_ref[0] == 1)
    def _():
        for m_ref in (m1, m2):
            m_ref[...] = jnp.full(m_ref.shape, M_INIT, F32)
        for z_ref in (acc1, acc2):
            z_ref[...] = jnp.zeros(z_ref.shape, F32)

        def update(ki, general):
            for s, m_ref, acc_ref in zip(tile_scores(ki, general), (m1, m2), (acc1, acc2)):
                _softmax_update(s, col_max(s), vt[ki], m_ref, acc_ref)

        def step(ki, carry):
            mode = mode_ref[(b * nq + qi) * nk + ki]

            @pl.when(mode == 0)
            def _():
                update(ki, True)

            @pl.when(mode != 0)
            def _():
                update(ki, False)

            return carry

        lax.fori_loop(0, nk, step, 0)

    lamv = lam_ref[...]
    lam = (jnp.exp(jnp.sum(lamv[0:1] * lamv[1:2], keepdims=True))
           - jnp.exp(jnp.sum(lamv[2:3] * lamv[3:4], keepdims=True)) + LAM_INIT)
    o = _normalised(acc1) - lam * _normalised(acc2)
    ms = jnp.mean(o * o, axis=0, keepdims=True)
    on = (o * lax.rsqrt(ms + SUBLN_EPS)).T * subg_ref[...] * (1.0 - LAM_INIT)
    ag = ag_ref[rows, :].astype(F32)
    o_ref[rows, :] = (on * (ag * jax.nn.sigmoid(ag))).astype(BF16)


def _attn_a(mode, fast, cb, proj3, qbias_t, kbias, cq, ck, posq_row, posk_col, dmin, lamv, subg,
            *, tq, tk):
    bsz, s, _ = proj3.shape
    nq, nk = s // tq, s // tk
    assert nk <= 24
    kern = functools.partial(_attn_a_kernel, tq=tq, tk=tk, nq=nq, nk=nk)
    stat = pltpu.VMEM((1, tq), F32)
    grid_spec = pltpu.PrefetchScalarGridSpec(
        num_scalar_prefetch=3,
        grid=(bsz, A_HEADS),
        in_specs=[
            pl.BlockSpec((None, s, LANE), lambda b, h, *_: (b, 0, COL_AQ + h)),
            pl.BlockSpec((None, s, LANE), lambda b, h, *_: (b, 0, COL_AK + h)),
            pl.BlockSpec((None, s, LANE), lambda b, h, *_: (b, 0, COL_AV + h)),
            pl.BlockSpec((None, s, LANE), lambda b, h, *_: (b, 0, COL_AG + h)),
            pl.BlockSpec((None, nq, LANE, tq), lambda b, h, *_: (b, 0, 0, 0)),
            pl.BlockSpec((None, s, LANE), lambda b, h, *_: (b, 0, 0)),
            pl.BlockSpec((None, LANE, 1), lambda b, h, *_: (h, 0, 0)),
            pl.BlockSpec((None, 1, LANE), lambda b, h, *_: (h, 0, 0)),
            pl.BlockSpec((None, nq, 1, tq), lambda b, h, *_: (b, 0, 0, 0)),
            pl.BlockSpec((None, s, 1), lambda b, h, *_: (b, 0, 0)),
            pl.BlockSpec((None, nq, 1, LANE), lambda b, h, *_: (b, 0, 0, 0)),
            pl.BlockSpec((4, A_HEAD_DIM), lambda b, h, *_: (0, 0)),
            pl.BlockSpec((1, A_V_DIM), lambda b, h, *_: (0, 0)),
        ],
        out_specs=pl.BlockSpec((None, s, LANE), lambda b, h, *_: (b, 0, h)),
        scratch_shapes=[
            pltpu.VMEM((LANE, tq), BF16), pltpu.VMEM((LANE, tq), BF16),
            pltpu.VMEM((nk, V_ROWS, tk), BF16),
            stat, stat,
            pltpu.VMEM((V_ROWS, tq), F32), pltpu.VMEM((V_ROWS, tq), F32),
            pltpu.SMEM((1,), jnp.int32), pltpu.VMEM((2, LANE), F32), pltpu.SMEM((nk + 1,), jnp.int32),
        ],
    )
    return pl.pallas_call(
        kern,
        grid_spec=grid_spec,
        out_shape=jax.ShapeDtypeStruct((bsz, s, A_WIDTH), BF16),
        compiler_params=pltpu.CompilerParams(
            dimension_semantics=("arbitrary", "arbitrary"), vmem_limit_bytes=VMEM_LIMIT),
        name="attn_a",
    )(mode, fast, cb, proj3, proj3, proj3, proj3, qbias_t, kbias, cq, ck, posq_row, posk_col, dmin,
      lamv, subg)


def _mla_prep_kernel(cq_ref, ckv_ref, kr_ref, posr_ref, invc_ref, qg_ref, kvg_ref,
                     wuqt_ref, wuk_ref, wuvt_ref, vones_ref, qo_ref, ko_ref, vo_ref):
    def norm(ref, g_ref):
        t = ref[...].astype(F32)
        ms = jnp.mean(t * t, axis=-1, keepdims=True)
        return (t * lax.rsqrt(ms + NORM_EPS) * g_ref[...]).astype(BF16)

    half = B_ROPE // 2
    cqn = norm(cq_ref, qg_ref)
    ckvn = norm(ckv_ref, kvg_ref)

    qt = _dot_nt(wuqt_ref[...], cqn)
    ang_t = invc_ref[...] * posr_ref[...]
    cs_t = jnp.cos(ang_t)
    sn_t = jnp.sin(ang_t)
    for h in range(B_HEADS):
        r0 = h * 2 * LANE
        qo_ref[r0:r0 + LANE, :] = qt[r0:r0 + LANE, :].astype(BF16)
        t1 = qt[r0 + LANE:r0 + LANE + half, :]
        t2 = qt[r0 + LANE + half:r0 + LANE + 2 * half, :]
        qo_ref[r0 + LANE:r0 + LANE + half, :] = (t1 * cs_t - t2 * sn_t).astype(BF16)
        qo_ref[r0 + LANE + half:r0 + LANE + 2 * half, :] = (t1 * sn_t + t2 * cs_t).astype(BF16)
        qo_ref[r0 + LANE + 2 * half:r0 + 2 * LANE, :] = jnp.zeros((LANE - 2 * half, qt.shape[1]), BF16)

    kr_t = kr_ref[...].astype(F32).T
    t1, t2 = kr_t[0:half, :], kr_t[half:2 * half, :]
    krope = jnp.concatenate(
        [t1 * cs_t - t2 * sn_t, t1 * sn_t + t2 * cs_t, kr_t[2 * half:LANE, :]], axis=0
    ).T.astype(BF16)
    kf = jnp.dot(ckvn, wuk_ref[...], preferred_element_type=F32)
    for h in range(B_HEADS):
        c0 = h * 2 * LANE
        ko_ref[:, c0:c0 + LANE] = kf[:, h * LANE:(h + 1) * LANE].astype(BF16)
        ko_ref[:, c0 + LANE:c0 + 2 * LANE] = krope

    vt = (_dot_nt(wuvt_ref[...], ckvn) + vones_ref[...]).astype(BF16)
    tkv = vo_ref.shape[2]
    for c in range(vo_ref.shape[0]):
        vo_ref[c] = vt[:, c * tkv:(c + 1) * tkv]


def _mla_prep(proj2, pos_row, inv_col, qg, kvg, wuqt, wuk, wuvt, vones, *, tm, tkv):
    m = proj2.shape[0]
    wq = B_HEADS * 2 * LANE
    wv = B_HEADS * V_ROWS
    return pl.pallas_call(
        _mla_prep_kernel,
        grid=(m // tm,),
        in_specs=[
            pl.BlockSpec((tm, B_Q_LORA), lambda i: (i, COL_CQ // 4)),
            pl.BlockSpec((tm, B_KV_LORA), lambda i: (i, COL_CKV // 4)),
            pl.BlockSpec((tm, LANE), lambda i: (i, COL_KR)),
            pl.BlockSpec((1, tm), lambda i: (0, i)),
            pl.BlockSpec((B_ROPE // 2, 1), lambda i: (0, 0)),
            pl.BlockSpec((1, B_Q_LORA), lambda i: (0, 0)),
            pl.BlockSpec((1, B_KV_LORA), lambda i: (0, 0)),
            pl.BlockSpec((wq, B_Q_LORA), lambda i: (0, 0)),
            pl.BlockSpec((B_KV_LORA, B_HEADS * B_NOPE), lambda i: (0, 0)),
            pl.BlockSpec((wv, B_KV_LORA), lambda i: (0, 0)),
            pl.BlockSpec((wv, 1), lambda i: (0, 0)),
        ],
        out_specs=[
            pl.BlockSpec((None, wq, tm), lambda i: (i, 0, 0)),
            pl.BlockSpec((tm, wq), lambda i: (i, 0)),
            pl.BlockSpec((tm // tkv, wv, tkv), lambda i: (i, 0, 0)),
        ],
        out_shape=[
            jax.ShapeDtypeStruct((m // tm, wq, tm), BF16),
            jax.ShapeDtypeStruct((m, wq), BF16),
            jax.ShapeDtypeStruct((m // tkv, wv, tkv), BF16),
        ],
        compiler_params=pltpu.CompilerParams(
            dimension_semantics=("arbitrary",), vmem_limit_bytes=VMEM_LIMIT),
        name="mla_prep",
    )(proj2, proj2, proj2, pos_row, inv_col, qg, kvg, wuqt, wuk, wuvt, vones)


def _attn_b_kernel(qt_ref, k_ref, vt_ref, bg_ref, o_ref, bufs, m1, acc1, flag_ref, *, tq, tk, nq, nk):
    def q_tile(qi, c):
        _attn_b_q_tile(qi, qt_ref, k_ref, vt_ref, bg_ref, o_ref, bufs, m1, acc1, flag_ref,
                       tq=tq, tk=tk, nk=nk)
        return c
    lax.fori_loop(0, nq, q_tile, 0)


def _attn_b_q_tile(qi, qt_ref, k_ref, vt_ref, bg_ref, o_ref, bufs, m1, acc1, flag_ref, *, tq, tk, nk):
    rows = pl.ds(pl.multiple_of(qi * tq, tq), tq)

    def raw_scores(ki):
        ks = pl.ds(pl.multiple_of(ki * tk, tk), tk)
        return jnp.dot(k_ref[ks, :], qt_ref[qi], preferred_element_type=F32)

    depth = len(bufs)

    def scores(ki):
        s = raw_scores(ki)
        bufs[ki % depth][...] = s
        return jnp.max(s, axis=0, keepdims=True)

    shift = scores(0)
    top = shift
    for ki in range(1, min(depth - 1, nk)):
        top = jnp.maximum(top, scores(ki))
    for ki in range(nk):
        if ki + depth - 1 < nk:
            top = jnp.maximum(top, scores(ki + depth - 1))
        pv = jnp.dot(vt_ref[ki], jnp.exp2(bufs[ki % depth][...] - shift).astype(BF16),
                     preferred_element_type=F32)
        acc1[...] = pv if ki == 0 else acc1[...] + pv
    flag_ref[0] = (jnp.max(top - shift) > SPEC_MARGIN).astype(jnp.int32)

    @pl.when(flag_ref[0] == 1)
    def _():
        m1[...] = jnp.full(m1.shape, M_INIT, F32)
        acc1[...] = jnp.zeros(acc1.shape, F32)

        def step(ki, carry):
            s = raw_scores(ki)
            _softmax_update(s, jnp.max(s, axis=0, keepdims=True), vt_ref[ki], m1, acc1)
            return carry

        lax.fori_loop(0, nk, step, 0)

    bg = bg_ref[rows, :].astype(F32)
    o_ref[rows, :] = (_normalised(acc1).T * (bg * jax.nn.sigmoid(bg))).astype(BF16)


def _attn_b(qt3, kb3, vt3, proj3, *, tq, tk):
    bsz, s, _ = kb3.shape
    nq, nk = s // tq, s // tk
    return pl.pallas_call(
        functools.partial(_attn_b_kernel, tq=tq, tk=tk, nq=nq, nk=nk),
        grid=(bsz, B_HEADS),
        in_specs=[
            pl.BlockSpec((nq, 2 * LANE, tq), lambda b, h: (b, h, 0)),
            pl.BlockSpec((None, s, 2 * LANE), lambda b, h: (b, 0, h)),
            pl.BlockSpec((nk, V_ROWS, tk), lambda b, h: (b, h, 0)),
            pl.BlockSpec((None, s, LANE), lambda b, h: (b, 0, COL_BG + h)),
        ],
        out_specs=pl.BlockSpec((None, s, LANE), lambda b, h: (b, 0, h)),
        out_shape=jax.ShapeDtypeStruct((bsz, s, B_WIDTH), BF16),
        scratch_shapes=[
            tuple(pltpu.VMEM((tk, tq), F32) for _ in range(SCORE_BUFS)),
            pltpu.VMEM((1, tq), F32), pltpu.VMEM((V_ROWS, tq), F32), pltpu.SMEM((1,), jnp.int32),
        ],
        compiler_params=pltpu.CompilerParams(
            dimension_semantics=("arbitrary", "arbitrary"), vmem_limit_bytes=VMEM_LIMIT),
        name="attn_b",
    )(qt3, kb3, vt3, proj3)


def _post_kernel(za_ref, zb_ref, ga_ref, gb_ref, x_ref, woa_ref, wob_ref, wout_ref, fg_ref, o_ref):
    ya = jnp.dot(za_ref[...], woa_ref[...], preferred_element_type=F32)
    yb = jnp.dot(zb_ref[...], wob_ref[...], preferred_element_type=F32)
    merged = (jax.nn.sigmoid(ga_ref[...].astype(F32)) * ya
              + jax.nn.sigmoid(gb_ref[...].astype(F32)) * yb)
    y = x_ref[...] + jnp.dot(merged.astype(BF16), wout_ref[...], preferred_element_type=F32)
    ms = jnp.mean(y * y, axis=-1, keepdims=True)
    o_ref[...] = y * lax.rsqrt(ms + NORM_EPS) * fg_ref[...]


def _post(za, zb, proj2, x2, woa, wob, wout, fg, *, tm=512):
    m = x2.shape[0]
    const = lambda shape: pl.BlockSpec(shape, lambda i: (0, 0), pipeline_mode=pl.Buffered(1))
    return pl.pallas_call(
        _post_kernel,
        grid=(m // tm,),
        in_specs=[
            pl.BlockSpec((tm, A_WIDTH), lambda i: (i, 0)),
            pl.BlockSpec((tm, B_WIDTH), lambda i: (i, 0)),
            pl.BlockSpec((tm, D_MODEL), lambda i: (i, COL_GA // 16)),
            pl.BlockSpec((tm, D_MODEL), lambda i: (i, COL_GB // 16)),
            pl.BlockSpec((tm, D_MODEL), lambda i: (i, 0)),
            const((A_WIDTH, D_MODEL)),
            const((B_WIDTH, D_MODEL)),
            const((D_MODEL, D_MODEL)),
            const((1, D_MODEL)),
        ],
        out_specs=pl.BlockSpec((tm, D_MODEL), lambda i: (i, 0)),
        out_shape=jax.ShapeDtypeStruct((m, D_MODEL), F32),
        compiler_params=pltpu.CompilerParams(
            dimension_semantics=("arbitrary",), vmem_limit_bytes=VMEM_LIMIT),
        name="post",
    )(za, zb, proj2, proj2, x2, woa, wob, wout, fg)


W_TN = 512
KR_COLS = _SIZES[6]


def _w_in_plan():
    off = dict(zip(("aq", "ak", "av", "ag", "cq", "ckv", "kr", "bg", "mg"),
                   [0] + _SPLITS))
    a_blk, b_blk, kind, scale = [], [], [], []
    for j in range(PROJ_COLS // W_TN):
        c0 = j * W_TN
        if c0 < COL_GA * LANE:
            src, k = off["aq"] + c0, 0
        elif c0 < COL_BG * LANE:
            src, k = off["mg"] + c0 - COL_GA * LANE, 1
        elif c0 < COL_CQ * LANE:
            src, k = off["bg"] + c0 - COL_BG * LANE, 1
        elif c0 < COL_KR * LANE:
            src, k = off["cq"] + c0 - COL_CQ * LANE, 0
        else:
            src, k = off["kr"], 2
        assert src % W_TN == (KR_COLS if k == 1 else 0)
        a_blk.append(src // W_TN)
        b_blk.append((src // W_TN + 1) * (W_TN // KR_COLS) if k == 1 else 0)
        kind.append(k)
        scale.append(A_HEAD_DIM ** -0.5 * LOG2E if c0 < COL_AK * LANE else 1.0)
    i32 = lambda v: jnp.asarray(v, jnp.int32)
    return i32(a_blk), i32(b_blk), i32(kind), jnp.asarray(scale, F32)


def _w_prep_kernel(ablk_ref, bblk_ref, kind_ref, scale_ref, a_ref, b_ref, o_ref):
    j = pl.program_id(0)
    kind = kind_ref[j]

    @pl.when(kind == 0)
    def _():
        o_ref[...] = (a_ref[...] * scale_ref[j]).T.astype(BF16)

    @pl.when(kind == 1)
    def _():
        t = jnp.concatenate([a_ref[KR_COLS:W_TN, :], b_ref[...]], axis=0)
        o_ref[...] = t.T.astype(BF16)

    @pl.when(kind == 2)
    def _():
        row = lax.broadcasted_iota(jnp.int32, (W_TN, 1), 0)
        o_ref[...] = jnp.where(row < KR_COLS, a_ref[...], 0.0).T.astype(BF16)


def _prep_w_in(w_t):
    d = w_t.shape[1]
    grid_spec = pltpu.PrefetchScalarGridSpec(
        num_scalar_prefetch=4,
        grid=(PROJ_COLS // W_TN,),
        in_specs=[
            pl.BlockSpec((W_TN, d), lambda j, a, b, k, s: (a[j], 0)),
            pl.BlockSpec((KR_COLS, d), lambda j, a, b, k, s: (b[j], 0)),
        ],
        out_specs=pl.BlockSpec((d, W_TN), lambda j, a, b, k, s: (0, j)),
    )
    return pl.pallas_call(
        _w_prep_kernel,
        grid_spec=grid_spec,
        out_shape=jax.ShapeDtypeStruct((d, PROJ_COLS), BF16),
        compiler_params=pltpu.CompilerParams(
            dimension_semantics=("arbitrary",), vmem_limit_bytes=VMEM_LIMIT),
        name="w_prep",
    )(*_w_in_plan(), w_t, w_t)


def _prep_w_uq_t(w):
    w = w.reshape(B_Q_LORA, B_HEADS, B_NOPE + B_ROPE) * ((B_NOPE + B_ROPE) ** -0.5 * LOG2E)
    w = jnp.pad(w, ((0, 0), (0, 0), (0, 2 * LANE - (B_NOPE + B_ROPE))))
    return w.reshape(B_Q_LORA, B_HEADS * 2 * LANE).T.astype(BF16)


def _prep_w_uv_t(w_ukv):
    w = w_ukv.reshape(B_KV_LORA, B_HEADS, B_NOPE + B_V_DIM)[:, :, B_NOPE:]
    w = jnp.pad(w, ((0, 0), (0, 0), (0, V_ROWS - B_V_DIM)))
    return w.reshape(B_KV_LORA, B_HEADS * V_ROWS).T.astype(BF16)


def _split3_bf16(c):
    c1 = c.astype(BF16).astype(F32)
    c2 = (c - c1).astype(BF16).astype(F32)
    c3 = (c - c1 - c2).astype(BF16).astype(F32)
    return c1, c2, c3


def _alibi_operands(positions, slopes, tq, tk):
    bsz, s = positions.shape
    nq, nk = s // tq, s // tk
    pmin = jnp.min(positions, axis=1, keepdims=True)
    rel = positions - pmin
    ok = ((jnp.min(rel, axis=1) >= 0) & (jnp.max(rel, axis=1) < 65536)
          & (jnp.max(jnp.abs(positions), axis=1) < (1 << 24)))
    qmin = jnp.min(positions.reshape(bsz, nq, tq), axis=-1)
    qmax = jnp.max(positions.reshape(bsz, nq, tq), axis=-1)
    kmin = jnp.min(positions.reshape(bsz, nk, tk), axis=-1)
    kmax = jnp.max(positions.reshape(bsz, nk, tk), axis=-1)
    after = qmin[:, :, None] >= kmax[:, None, :]
    before = qmax[:, :, None] <= kmin[:, None, :]
    mode = jnp.where(after, 1, jnp.where(before, -1, 0)) * ok[:, None, None].astype(jnp.int32)
    gap = jnp.where(after, qmin[:, :, None] - kmax[:, None, :], kmin[:, None, :] - qmax[:, :, None])
    dmin = jnp.where(mode != 0, gap, 0).astype(F32)
    off_diag = ~jnp.eye(nq, nk, dtype=bool)[None]
    fast = jnp.all((mode != 0) | ~off_diag, axis=-1) & (nq == nk) & (nk >= 4)

    rel = jnp.where(ok[:, None], rel, 0)
    hi = (rel >> 8).astype(F32)
    lo = (rel & 255).astype(F32)

    g = np.arange(LANE) % A_HEAD_DIM
    first, second = g < N_BIAS // 2, (g >= N_BIAS // 2) & (g < N_BIAS)
    even, odd = g % 2 == 0, g % 2 == 1
    f32 = lambda mask: jnp.asarray(mask, F32)
    q_hi, q_lo, q_one = f32(first & even), f32(first & odd), f32(second)
    k_hi, k_lo, k_one = f32(second & even), f32(second & odd), f32(first)
    hi_q, lo_q = hi.reshape(bsz, nq, 1, tq), lo.reshape(bsz, nq, 1, tq)
    col = lambda v: v[None, None, :, None]
    qb_t = (hi_q * col(q_hi) + lo_q * col(q_lo) + col(q_one)).astype(BF16)
    kb = (hi[:, :, None] * k_hi[None, None, :] + lo[:, :, None] * k_lo[None, None, :]
          + k_one[None, None, :]).astype(BF16)

    cb = slopes * LOG2E
    pieces = _split3_bf16(cb)
    cpair = jnp.stack([v for c in pieces for v in (256.0 * c, c)], axis=-1)
    sel = np.zeros((N_BIAS // 2, LANE), np.float32)
    for i in range(N_BIAS // 2):
        sel[i, (g % (N_BIAS // 2) == i) & (g < N_BIAS)] = 1.0
    spread = jnp.dot(cpair, jnp.asarray(sel), precision=lax.Precision.HIGHEST)
    cq = (spread * f32(second) + f32(first))[:, :, None]
    ck = (-spread * f32(first) + f32(second))[:, None, :]
    return (mode.reshape(-1).astype(jnp.int32), fast.reshape(-1).astype(jnp.int32), cb.astype(F32),
            jnp.pad(dmin, ((0, 0), (0, 0), (0, LANE - nk)))[:, :, None, :],
            qb_t, kb, cq, ck)


def kernel(x, positions, norm_g, w_in, lam_q1, lam_k1, lam_q2, lam_k2, a_subln_g, w_oa, q_norm_g,
           w_uq, kv_norm_g, w_ukv, w_ob, w_out, final_g):
    bsz, s, d = x.shape
    m = bsz * s
    assert norm_g.shape[0] == 1 and d == D_MODEL
    assert all(s % t == 0 for t in (TQ, TK, TQ_B, TK_B))
    layer = 0
    x2 = x.reshape(m, d)
    pos_f = positions.astype(F32)

    proj2 = _inproj(x2, norm_g[layer][None, :], _prep_w_in(w_in[layer].T))
    proj3 = proj2.reshape(bsz, s, PROJ_COLS)

    slopes = jnp.asarray([2.0 ** (-8.0 * (h + 1) / A_HEADS) for h in range(A_HEADS)], dtype=F32)
    mode, fast, cb, dmin, qb_t, kb, cq, ck = _alibi_operands(positions, slopes, TQ, TK)
    lamv = jnp.stack([lam_q1[layer], lam_k1[layer], lam_q2[layer], lam_k2[layer]]).astype(F32)
    za = _attn_a(mode, fast, cb, proj3, qb_t, kb, cq, ck, pos_f.reshape(bsz, s // TQ, 1, TQ),
                 pos_f[:, :, None], dmin, lamv, a_subln_g[layer][None, :], tq=TQ, tk=TK)

    half = B_ROPE // 2
    inv = ROPE_THETA ** (-jnp.arange(half, dtype=F32) / half)
    wuk = (w_ukv[layer].reshape(B_KV_LORA, B_HEADS, B_NOPE + B_V_DIM)[:, :, :B_NOPE]
           .reshape(B_KV_LORA, B_HEADS * B_NOPE).astype(BF16))
    vones = (jnp.arange(B_HEADS * V_ROWS) % V_ROWS == B_V_DIM).astype(F32)[:, None]
    qt3, kb2, vt3 = _mla_prep(proj2, pos_f.reshape(1, m), inv[:, None],
                              q_norm_g[layer][None, :], kv_norm_g[layer][None, :],
                              _prep_w_uq_t(w_uq[layer]), wuk, _prep_w_uv_t(w_ukv[layer]), vones,
                              tm=TQ_B, tkv=TK_B)
    zb = _attn_b(qt3, kb2.reshape(bsz, s, -1), vt3, proj3, tq=TQ_B, tk=TK_B)

    out = _post(za.reshape(m, A_WIDTH), zb.reshape(m, B_WIDTH), proj2, x2, w_oa[layer].astype(BF16),
                w_ob[layer].astype(BF16), w_out[layer].astype(BF16), final_g[None, :])
    return out.reshape(bsz, s, d)
```

```python
import functools
import math

import numpy as np
import jax
import jax.numpy as jnp
from jax import lax
from jax.experimental import pallas as pl
from jax.experimental.pallas import tpu as pltpu

F32 = jnp.float32
BF16 = jnp.bfloat16

D_MODEL = 2048
A_HEADS = 8
A_HEAD_DIM = 64
A_V_DIM = 128
A_WIDTH = 1024
B_HEADS = 8
B_Q_LORA = 512
B_KV_LORA = 512
B_NOPE = 128
B_ROPE = 64
B_V_DIM = 128
B_WIDTH = 1024
ROPE_THETA = 10000.0
NORM_EPS = 1e-6
SUBLN_EPS = 1e-5
LAM_INIT = 0.8 - 0.6 * math.exp(-0.3 * 0)
LOG2E = math.log2(math.e)

_SIZES = [1024, 1024, 1024, 1024, 512, 512, 64, 1024, 4096]
_SPLITS = [int(v) for v in np.cumsum(_SIZES)[:-1]]

LANE = 128
BF16_ROWS = 16
COL_AQ, COL_AK, COL_AV, COL_AG = 0, 8, 16, 24
COL_GA, COL_GB, COL_BG = 32, 48, 64
COL_CQ, COL_CKV, COL_KR = 72, 76, 80
PROJ_COLS = 10752

TQ = 512
TK = 512
TQ_B = 1024
TK_B = 512
V_ROWS = A_V_DIM + BF16_ROWS
N_BIAS = 12
VMEM_LIMIT = 56 * 1024 * 1024
M_INIT = -1e30
SPEC_MARGIN = 60.0
SKIP_LOG2 = 160.0
SKIP_SLACK = 1.01


def _dot_nt(a, b):
    return lax.dot_general(a, b, (((1,), (1,)), ((), ())), preferred_element_type=F32)


def _inproj_kernel(x_ref, g_ref, w_ref, o_ref, h_ref, *, tm):
    @pl.when(pl.program_id(1) == 0)
    def _():
        def body(r, c):
            rows = pl.ds(pl.multiple_of(r * 128, 128), 128)
            x = x_ref[rows, :]
            ms = jnp.mean(x * x, axis=-1, keepdims=True)
            h_ref[rows, :] = (x * lax.rsqrt(ms + NORM_EPS) * g_ref[...]).astype(BF16)
            return c
        lax.fori_loop(0, tm // 128, body, 0)

    o_ref[...] = jnp.dot(h_ref[...], w_ref[...], preferred_element_type=F32).astype(BF16)


def _inproj(x2, g, w, *, tm=1024, tn=1536):
    m = x2.shape[0]
    return pl.pallas_call(
        functools.partial(_inproj_kernel, tm=tm),
        grid=(m // tm, PROJ_COLS // tn),
        in_specs=[
            pl.BlockSpec((tm, D_MODEL), lambda i, j: (i, 0)),
            pl.BlockSpec((1, D_MODEL), lambda i, j: (0, 0)),
            pl.BlockSpec((D_MODEL, tn), lambda i, j: (0, j)),
        ],
        out_specs=pl.BlockSpec((tm, tn), lambda i, j: (i, j)),
        out_shape=jax.ShapeDtypeStruct((m, PROJ_COLS), BF16),
        scratch_shapes=[pltpu.VMEM((tm, D_MODEL), BF16)],
        compiler_params=pltpu.CompilerParams(
            dimension_semantics=("arbitrary", "arbitrary"), vmem_limit_bytes=VMEM_LIMIT),
        name="inproj",
    )(x2, g, w)


def _softmax_update(s, s_max, vt, m_ref, acc_ref):
    m_old = m_ref[...]
    m_new = jnp.maximum(m_old, s_max)
    p = jnp.exp2(s - m_new)
    alpha = jnp.exp2(m_old - m_new)
    acc_ref[...] = alpha * acc_ref[...] + jnp.dot(vt, p.astype(BF16), preferred_element_type=F32)
    m_ref[...] = m_new


def _normalised(acc_ref):
    return acc_ref[0:A_V_DIM, :] / acc_ref[A_V_DIM:A_V_DIM + 1, :]


def _attn_a_kernel(mode_ref, fast_ref, cb_ref, q_ref, k_ref, v_ref, ag_ref, qbt_ref, kb_ref,
                   cq_ref, ck_ref, posq_ref, posk_ref, dmin_ref, lam_ref, subg_ref, o_ref,
                   qa1, qa2, vt, m1, m2, acc1, acc2, flag_ref, kn_ref, list_ref, *, tq, tk, nq, nk):
    b = pl.program_id(0)
    cb = cb_ref[pl.program_id(1)]

    ones_rows = (lax.broadcasted_iota(jnp.int32, (BF16_ROWS, tk), 0) == 0).astype(BF16)
    feat = lax.broadcasted_iota(jnp.int32, (LANE, LANE), 0)
    col = lax.broadcasted_iota(jnp.int32, (LANE, LANE), 1)
    half_sum = (col == jnp.where(feat < A_HEAD_DIM, 0, 1)).astype(BF16)
    lane = lax.broadcasted_iota(jnp.int32, (1, LANE), 1)
    kn = [jnp.zeros((1, LANE), F32), jnp.zeros((1, LANE), F32)]
    for ki in range(nk):
        vt[ki, 0:A_V_DIM, :] = v_ref[ki * tk:(ki + 1) * tk, :].astype(F32).T.astype(BF16)
        vt[ki, A_V_DIM:V_ROWS, :] = ones_rows
        k = k_ref[ki * tk:(ki + 1) * tk, :].astype(F32)
        nsq = jnp.dot((k * k).astype(BF16), half_sum, preferred_element_type=F32)
        nmax = jnp.sqrt(jnp.max(nsq, axis=0, keepdims=True))
        for m in range(2):
            norm_m = jnp.max(jnp.where(lane == m, nmax, 0.0), axis=1, keepdims=True)
            kn[m] = jnp.where(lane == ki, norm_m, kn[m])
    kn_ref[0:1, :] = kn[0]
    kn_ref[1:2, :] = kn[1]

    def q_tile(qi, c):
        _attn_a_q_tile(qi, b, cb, mode_ref, fast_ref, dmin_ref, q_ref, k_ref, ag_ref, qbt_ref, kb_ref,
                       cq_ref, ck_ref, posq_ref, posk_ref, lam_ref, subg_ref, o_ref,
                       qa1, qa2, vt, m1, m2, acc1, acc2, flag_ref, kn_ref, list_ref,
                       tq=tq, tk=tk, nq=nq, nk=nk)
        return c
    lax.fori_loop(0, nq, q_tile, 0)


def _attn_a_q_tile(qi, b, cb, mode_ref, fast_ref, dmin_ref, q_ref, k_ref, ag_ref, qbt_ref, kb_ref,
                   cq_ref, ck_ref, posq_ref, posk_ref, lam_ref, subg_ref, o_ref,
                   qa1, qa2, vt, m1, m2, acc1, acc2, flag_ref, kn_ref, list_ref, *, tq, tk, nq, nk):
    rows = pl.ds(pl.multiple_of(qi * tq, tq), tq)
    posq = posq_ref[qi]

    lo_r = lax.broadcasted_iota(jnp.int32, (LANE, 1), 0) < A_HEAD_DIM
    qs = q_ref[rows, :].astype(F32).T
    qbias = qbt_ref[qi].astype(F32) * cq_ref[...]
    qa1[...] = jnp.where(lo_r, qs, qbias).astype(BF16)
    qa2[...] = jnp.where(lo_r, qbias, qs).astype(BF16)
    lo_c = lax.broadcasted_iota(jnp.int32, (1, LANE), 1) < A_HEAD_DIM
    qsq = qs * qs
    qn1 = jnp.max(jnp.sqrt(jnp.sum(jnp.where(lo_r, qsq, 0.0), axis=0, keepdims=True)),
                  axis=1, keepdims=True)
    qn2 = jnp.max(jnp.sqrt(jnp.sum(jnp.where(lo_r, 0.0, qsq), axis=0, keepdims=True)),
                  axis=1, keepdims=True)

    def raw_scores(ki, sgn):
        ks = pl.ds(pl.multiple_of(ki * tk, tk), tk)
        k = k_ref[ks, :]
        kbias = (kb_ref[ks, :].astype(F32) * (ck_ref[...] * sgn)).astype(BF16)
        s1 = jnp.dot(jnp.where(lo_c, k, kbias), qa1[...], preferred_element_type=F32)
        s2 = jnp.dot(jnp.where(lo_c, kbias, k), qa2[...], preferred_element_type=F32)
        return s1, s2

    def explicit_bias(ki):
        ks = pl.ds(pl.multiple_of(ki * tk, tk), tk)
        return -cb * jnp.abs(posk_ref[ks, :] - posq)

    def tile_scores(ki, general):
        if general:
            s1, s2 = raw_scores(ki, 0.0)
            bias = explicit_bias(ki)
            return s1 + bias, s2 + bias
        return raw_scores(ki, mode_ref[(b * nq + qi) * nk + ki].astype(F32))

    def col_max(s):
        return jnp.max(s, axis=0, keepdims=True)

    def pv(ki, s, shift):
        return jnp.dot(vt[ki], jnp.exp2(s - shift).astype(BF16), preferred_element_type=F32)

    fast = fast_ref[b * nq + qi]
    flag_ref[0] = 1 - fast

    @pl.when(fast == 1)
    def _():
        s1, s2 = tile_scores(qi, True)
        sh1, sh2 = col_max(s1), col_max(s2)
        m1[...] = sh1
        m2[...] = sh2
        acc1[...] = pv(qi, s1, sh1)
        acc2[...] = pv(qi, s2, sh2)

        def add_tile(t, off):
            s1, s2 = tile_scores(t, False)
            m1[...] = jnp.maximum(m1[...], col_max(s1) - off)
            m2[...] = jnp.maximum(m2[...], col_max(s2) - off)
            acc1[...] += pv(t, s1, sh1 + off)
            acc2[...] += pv(t, s2, sh2 + off)

        lo1 = jnp.min(sh1, axis=1, keepdims=True)
        lo2 = jnp.min(sh2, axis=1, keepdims=True)
        gap = cb * dmin_ref[qi]
        ub1 = SKIP_SLACK * qn1 * kn_ref[0:1, :] + 1.0 - gap - lo1
        ub2 = SKIP_SLACK * qn2 * kn_ref[1:2, :] + 1.0 - gap - lo2
        near_a = jnp.where(qi == 0, 2, qi - 1)
        near_b = jnp.where(qi == nk - 1, nk - 3, qi + 1)
        lane = lax.broadcasted_iota(jnp.int32, (1, LANE), 1)
        rest = (lane != qi) & (lane != near_a) & (lane != near_b) & (lane < nk)
        keep = rest & ((ub1 > -SKIP_LOG2) | (ub2 > -SKIP_LOG2))
        bits = jnp.sum(jnp.where(keep, jnp.exp2(lane.astype(F32)), 0.0)).astype(jnp.int32)
        add_tile(near_a, 0.0)
        add_tile(near_b, 0.0)
        n = jnp.int32(0)
        for t in range(nk):
            list_ref[n] = jnp.int32(t)
            n = n + (lax.shift_right_logical(bits, t) & 1)
        list_ref[n] = qi

        n_rest = nk - 3
        short = n < n_rest

        @pl.when(short)
        def _():
            def pair(p, c):
                for e in range(2):
                    idx = 2 * p + e
                    add_tile(list_ref[idx], jnp.where(idx < n, 0.0, -M_INIT))
                return c
            lax.fori_loop(0, lax.shift_right_logical(n + 1, 1), pair, 0)

        @pl.when(jnp.logical_not(short))
        def _():
            for j in range(n_rest):
                add_tile(list_ref[j], 0.0)

        excess = jnp.max(jnp.maximum(m1[...] - sh1, m2[...] - sh2))
        flag_ref[0] = (excess > SPEC_MARGIN).astype(jnp.int32)

    @pl.when(flag_ref[0] == 1)
    def _():
        for m_ref in (m1, m2):
            m_ref[...] = jnp.full(m_ref.shape, M_INIT, F32)
        for z_ref in (acc1, acc2):
            z_ref[...] = jnp.zeros(z_ref.shape, F32)

        def update(ki, general):
            for s, m_ref, acc_ref in zip(tile_scores(ki, general), (m1, m2), (acc1, acc2)):
                _softmax_update(s, col_max(s), vt[ki], m_ref, acc_ref)

        def step(ki, carry):
            mode = mode_ref[(b * nq + qi) * nk + ki]

            @pl.when(mode == 0)
            def _():
                update(ki, True)

            @pl.when(mode != 0)
            def _():
                update(ki, False)

            return carry

        lax.fori_loop(0, nk, step, 0)

    lamv = lam_ref[...]
    lam = (jnp.exp(jnp.sum(lamv[0:1] * lamv[1:2], keepdims=True))
           - jnp.exp(jnp.sum(lamv[2:3] * lamv[3:4], keepdims=True)) + LAM_INIT)
    o = _normalised(acc1) - lam * _normalised(acc2)
    ms = jnp.mean(o * o, axis=0, keepdims=True)
    on = (o * lax.rsqrt(ms + SUBLN_EPS)).T * subg_ref[...] * (1.0 - LAM_INIT)
    ag = ag_ref[rows, :].astype(F32)
    o_ref[rows, :] = (on * (ag * jax.nn.sigmoid(ag))).astype(BF16)


def _attn_a(mode, fast, cb, proj3, qbias_t, kbias, cq, ck, posq_row, posk_col, dmin, lamv, subg,
            *, tq, tk):
    bsz, s, _ = proj3.shape
    nq, nk = s // tq, s // tk
    assert nk <= 24
    kern = functools.partial(_attn_a_kernel, tq=tq, tk=tk, nq=nq, nk=nk)
    stat = pltpu.VMEM((1, tq), F32)
    grid_spec = pltpu.PrefetchScalarGridSpec(
        num_scalar_prefetch=3,
        grid=(bsz, A_HEADS),
        in_specs=[
            pl.BlockSpec((None, s, LANE), lambda b, h, *_: (b, 0, COL_AQ + h)),
            pl.BlockSpec((None, s, LANE), lambda b, h, *_: (b, 0, COL_AK + h)),
            pl.BlockSpec((None, s, LANE), lambda b, h, *_: (b, 0, COL_AV + h)),
            pl.BlockSpec((None, s, LANE), lambda b, h, *_: (b, 0, COL_AG + h)),
            pl.BlockSpec((None, nq, LANE, tq), lambda b, h, *_: (b, 0, 0, 0)),
            pl.BlockSpec((None, s, LANE), lambda b, h, *_: (b, 0, 0)),
            pl.BlockSpec((None, LANE, 1), lambda b, h, *_: (h, 0, 0)),
            pl.BlockSpec((None, 1, LANE), lambda b, h, *_: (h, 0, 0)),
            pl.BlockSpec((None, nq, 1, tq), lambda b, h, *_: (b, 0, 0, 0)),
            pl.BlockSpec((None, s, 1), lambda b, h, *_: (b, 0, 0)),
            pl.BlockSpec((None, nq, 1, LANE), lambda b, h, *_: (b, 0, 0, 0)),
            pl.BlockSpec((4, A_HEAD_DIM), lambda b, h, *_: (0, 0)),
            pl.BlockSpec((1, A_V_DIM), lambda b, h, *_: (0, 0)),
        ],
        out_specs=pl.BlockSpec((None, s, LANE), lambda b, h, *_: (b, 0, h)),
        scratch_shapes=[
            pltpu.VMEM((LANE, tq), BF16), pltpu.VMEM((LANE, tq), BF16),
            pltpu.VMEM((nk, V_ROWS, tk), BF16),
            stat, stat,
            pltpu.VMEM((V_ROWS, tq), F32), pltpu.VMEM((V_ROWS, tq), F32),
            pltpu.SMEM((1,), jnp.int32), pltpu.VMEM((2, LANE), F32), pltpu.SMEM((nk + 1,), jnp.int32),
        ],
    )
    return pl.pallas_call(
        kern,
        grid_spec=grid_spec,
        out_shape=jax.ShapeDtypeStruct((bsz, s, A_WIDTH), BF16),
        compiler_params=pltpu.CompilerParams(
            dimension_semantics=("arbitrary", "arbitrary"), vmem_limit_bytes=VMEM_LIMIT),
        name="attn_a",
    )(mode, fast, cb, proj3, proj3, proj3, proj3, qbias_t, kbias, cq, ck, posq_row, posk_col, dmin,
      lamv, subg)


def _mla_prep_kernel(cq_ref, ckv_ref, kr_ref, posr_ref, invc_ref, qg_ref, kvg_ref,
                     wuqt_ref, wuk_ref, wuvt_ref, vones_ref, qo_ref, ko_ref, vo_ref):
    def norm(ref, g_ref):
        t = ref[...].astype(F32)
        ms = jnp.mean(t * t, axis=-1, keepdims=True)
        return (t * lax.rsqrt(ms + NORM_EPS) * g_ref[...]).astype(BF16)

    half = B_ROPE // 2
    cqn = norm(cq_ref, qg_ref)
    ckvn = norm(ckv_ref, kvg_ref)

    qt = _dot_nt(wuqt_ref[...], cqn)
    ang_t = invc_ref[...] * posr_ref[...]
    cs_t = jnp.cos(ang_t)
    sn_t = jnp.sin(ang_t)
    for h in range(B_HEADS):
        r0 = h * 2 * LANE
        qo_ref[r0:r0 + LANE, :] = qt[r0:r0 + LANE, :].astype(BF16)
        t1 = qt[r0 + LANE:r0 + LANE + half, :]
        t2 = qt[r0 + LANE + half:r0 + LANE + 2 * half, :]
        qo_ref[r0 + LANE:r0 + LANE + half, :] = (t1 * cs_t - t2 * sn_t).astype(BF16)
        qo_ref[r0 + LANE + half:r0 + LANE + 2 * half, :] = (t1 * sn_t + t2 * cs_t).astype(BF16)
        qo_ref[r0 + LANE + 2 * half:r0 + 2 * LANE, :] = jnp.zeros((LANE - 2 * half, qt.shape[1]), BF16)

    kr_t = kr_ref[...].astype(F32).T
    t1, t2 = kr_t[0:half, :], kr_t[half:2 * half, :]
    krope = jnp.concatenate(
        [t1 * cs_t - t2 * sn_t, t1 * sn_t + t2 * cs_t, kr_t[2 * half:LANE, :]], axis=0
    ).T.astype(BF16)
    kf = jnp.dot(ckvn, wuk_ref[...], preferred_element_type=F32)
    for h in range(B_HEADS):
        c0 = h * 2 * LANE
        ko_ref[:, c0:c0 + LANE] = kf[:, h * LANE:(h + 1) * LANE].astype(BF16)
        ko_ref[:, c0 + LANE:c0 + 2 * LANE] = krope

    vt = (_dot_nt(wuvt_ref[...], ckvn) + vones_ref[...]).astype(BF16)
    tkv = vo_ref.shape[2]
    for c in range(vo_ref.shape[0]):
        vo_ref[c] = vt[:, c * tkv:(c + 1) * tkv]


def _mla_prep(proj2, pos_row, inv_col, qg, kvg, wuqt, wuk, wuvt, vones, *, tm, tkv):
    m = proj2.shape[0]
    wq = B_HEADS * 2 * LANE
    wv = B_HEADS * V_ROWS
    return pl.pallas_call(
        _mla_prep_kernel,
        grid=(m // tm,),
        in_specs=[
            pl.BlockSpec((tm, B_Q_LORA), lambda i: (i, COL_CQ // 4)),
            pl.BlockSpec((tm, B_KV_LORA), lambda i: (i, COL_CKV // 4)),
            pl.BlockSpec((tm, LANE), lambda i: (i, COL_KR)),
            pl.BlockSpec((1, tm), lambda i: (0, i)),
            pl.BlockSpec((B_ROPE // 2, 1), lambda i: (0, 0)),
            pl.BlockSpec((1, B_Q_LORA), lambda i: (0, 0)),
            pl.BlockSpec((1, B_KV_LORA), lambda i: (0, 0)),
            pl.BlockSpec((wq, B_Q_LORA), lambda i: (0, 0)),
            pl.BlockSpec((B_KV_LORA, B_HEADS * B_NOPE), lambda i: (0, 0)),
            pl.BlockSpec((wv, B_KV_LORA), lambda i: (0, 0)),
            pl.BlockSpec((wv, 1), lambda i: (0, 0)),
        ],
        out_specs=[
            pl.BlockSpec((None, wq, tm), lambda i: (i, 0, 0)),
            pl.BlockSpec((tm, wq), lambda i: (i, 0)),
            pl.BlockSpec((tm // tkv, wv, tkv), lambda i: (i, 0, 0)),
        ],
        out_shape=[
            jax.ShapeDtypeStruct((m // tm, wq, tm), BF16),
            jax.ShapeDtypeStruct((m, wq), BF16),
            jax.ShapeDtypeStruct((m // tkv, wv, tkv), BF16),
        ],
        compiler_params=pltpu.CompilerParams(
            dimension_semantics=("arbitrary",), vmem_limit_bytes=VMEM_LIMIT),
        name="mla_prep",
    )(proj2, proj2, proj2, pos_row, inv_col, qg, kvg, wuqt, wuk, wuvt, vones)


def _attn_b_kernel(qt_ref, k_ref, vt_ref, bg_ref, o_ref, m1, acc1, flag_ref, *, tq, tk, nq, nk):
    def q_tile(qi, c):
        _attn_b_q_tile(qi, qt_ref, k_ref, vt_ref, bg_ref, o_ref, m1, acc1, flag_ref,
                       tq=tq, tk=tk, nk=nk)
        return c
    lax.fori_loop(0, nq, q_tile, 0)


def _attn_b_q_tile(qi, qt_ref, k_ref, vt_ref, bg_ref, o_ref, m1, acc1, flag_ref, *, tq, tk, nk):
    rows = pl.ds(pl.multiple_of(qi * tq, tq), tq)

    def raw_scores(ki):
        ks = pl.ds(pl.multiple_of(ki * tk, tk), tk)
        return jnp.dot(k_ref[ks, :], qt_ref[qi], preferred_element_type=F32)

    half = tq // 2
    cols = (slice(0, half), slice(half, tq))

    def half_scores(ki):
        ks = pl.ds(pl.multiple_of(ki * tk, tk), tk)
        k = k_ref[ks, :]
        return [jnp.dot(k, qt_ref[qi, :, c], preferred_element_type=F32) for c in cols]

    s2 = half_scores(0)
    shifts = [jnp.max(s, axis=0, keepdims=True) for s in s2]
    tops = list(shifts)
    for c, s, sh in zip(cols, s2, shifts):
        acc1[:, c] = jnp.dot(vt_ref[0], jnp.exp2(s - sh).astype(BF16), preferred_element_type=F32)
    for ki in range(1, nk):
        s2 = half_scores(ki)
        for j, (c, s, sh) in enumerate(zip(cols, s2, shifts)):
            tops[j] = jnp.maximum(tops[j], jnp.max(s, axis=0, keepdims=True))
            acc1[:, c] += jnp.dot(vt_ref[ki], jnp.exp2(s - sh).astype(BF16),
                                  preferred_element_type=F32)
    excess = jnp.maximum(jnp.max(tops[0] - shifts[0]), jnp.max(tops[1] - shifts[1]))
    flag_ref[0] = (excess > SPEC_MARGIN).astype(jnp.int32)

    @pl.when(flag_ref[0] == 1)
    def _():
        m1[...] = jnp.full(m1.shape, M_INIT, F32)
        acc1[...] = jnp.zeros(acc1.shape, F32)

        def step(ki, carry):
            s = raw_scores(ki)
            _softmax_update(s, jnp.max(s, axis=0, keepdims=True), vt_ref[ki], m1, acc1)
            return carry

        lax.fori_loop(0, nk, step, 0)

    bg = bg_ref[rows, :].astype(F32)
    o_ref[rows, :] = (_normalised(acc1).T * (bg * jax.nn.sigmoid(bg))).astype(BF16)


def _attn_b(qt3, kb3, vt3, proj3, *, tq, tk):
    bsz, s, _ = kb3.shape
    nq, nk = s // tq, s // tk
    return pl.pallas_call(
        functools.partial(_attn_b_kernel, tq=tq, tk=tk, nq=nq, nk=nk),
        grid=(bsz, B_HEADS),
        in_specs=[
            pl.BlockSpec((nq, 2 * LANE, tq), lambda b, h: (b, h, 0)),
            pl.BlockSpec((None, s, 2 * LANE), lambda b, h: (b, 0, h)),
            pl.BlockSpec((nk, V_ROWS, tk), lambda b, h: (b, h, 0)),
            pl.BlockSpec((None, s, LANE), lambda b, h: (b, 0, COL_BG + h)),
        ],
        out_specs=pl.BlockSpec((None, s, LANE), lambda b, h: (b, 0, h)),
        out_shape=jax.ShapeDtypeStruct((bsz, s, B_WIDTH), BF16),
        scratch_shapes=[
            pltpu.VMEM((1, tq), F32), pltpu.VMEM((V_ROWS, tq), F32), pltpu.SMEM((1,), jnp.int32),
        ],
        compiler_params=pltpu.CompilerParams(
            dimension_semantics=("arbitrary", "arbitrary"), vmem_limit_bytes=VMEM_LIMIT),
        name="attn_b",
    )(qt3, kb3, vt3, proj3)


def _post_kernel(za_ref, zb_ref, ga_ref, gb_ref, x_ref, woa_ref, wob_ref, wout_ref, fg_ref, o_ref):
    ya = jnp.dot(za_ref[...], woa_ref[...], preferred_element_type=F32)
    yb = jnp.dot(zb_ref[...], wob_ref[...], preferred_element_type=F32)
    merged = (jax.nn.sigmoid(ga_ref[...].astype(F32)) * ya
              + jax.nn.sigmoid(gb_ref[...].astype(F32)) * yb)
    y = x_ref[...] + jnp.dot(merged.astype(BF16), wout_ref[...], preferred_element_type=F32)
    ms = jnp.mean(y * y, axis=-1, keepdims=True)
    o_ref[...] = y * lax.rsqrt(ms + NORM_EPS) * fg_ref[...]


def _post(za, zb, proj2, x2, woa, wob, wout, fg, *, tm=512):
    m = x2.shape[0]
    const = lambda shape: pl.BlockSpec(shape, lambda i: (0, 0), pipeline_mode=pl.Buffered(1))
    return pl.pallas_call(
        _post_kernel,
        grid=(m // tm,),
        in_specs=[
            pl.BlockSpec((tm, A_WIDTH), lambda i: (i, 0)),
            pl.BlockSpec((tm, B_WIDTH), lambda i: (i, 0)),
            pl.BlockSpec((tm, D_MODEL), lambda i: (i, COL_GA // 16)),
            pl.BlockSpec((tm, D_MODEL), lambda i: (i, COL_GB // 16)),
            pl.BlockSpec((tm, D_MODEL), lambda i: (i, 0)),
            const((A_WIDTH, D_MODEL)),
            const((B_WIDTH, D_MODEL)),
            const((D_MODEL, D_MODEL)),
            const((1, D_MODEL)),
        ],
        out_specs=pl.BlockSpec((tm, D_MODEL), lambda i: (i, 0)),
        out_shape=jax.ShapeDtypeStruct((m, D_MODEL), F32),
        compiler_params=pltpu.CompilerParams(
            dimension_semantics=("arbitrary",), vmem_limit_bytes=VMEM_LIMIT),
        name="post",
    )(za, zb, proj2, proj2, x2, woa, wob, wout, fg)


W_TN = 512
KR_COLS = _SIZES[6]


def _w_in_plan():
    off = dict(zip(("aq", "ak", "av", "ag", "cq", "ckv", "kr", "bg", "mg"),
                   [0] + _SPLITS))
    a_blk, b_blk, kind, scale = [], [], [], []
    for j in range(PROJ_COLS // W_TN):
        c0 = j * W_TN
        if c0 < COL_GA * LANE:
            src, k = off["aq"] + c0, 0
        elif c0 < COL_BG * LANE:
            src, k = off["mg"] + c0 - COL_GA * LANE, 1
        elif c0 < COL_CQ * LANE:
            src, k = off["bg"] + c0 - COL_BG * LANE, 1
        elif c0 < COL_KR * LANE:
            src, k = off["cq"] + c0 - COL_CQ * LANE, 0
        else:
            src, k = off["kr"], 2
        assert src % W_TN == (KR_COLS if k == 1 else 0)
        a_blk.append(src // W_TN)
        b_blk.append((src // W_TN + 1) * (W_TN // KR_COLS) if k == 1 else 0)
        kind.append(k)
        scale.append(A_HEAD_DIM ** -0.5 * LOG2E if c0 < COL_AK * LANE else 1.0)
    i32 = lambda v: jnp.asarray(v, jnp.int32)
    return i32(a_blk), i32(b_blk), i32(kind), jnp.asarray(scale, F32)


def _w_prep_kernel(ablk_ref, bblk_ref, kind_ref, scale_ref, a_ref, b_ref, o_ref):
    j = pl.program_id(0)
    kind = kind_ref[j]

    @pl.when(kind == 0)
    def _():
        o_ref[...] = (a_ref[...] * scale_ref[j]).T.astype(BF16)

    @pl.when(kind == 1)
    def _():
        t = jnp.concatenate([a_ref[KR_COLS:W_TN, :], b_ref[...]], axis=0)
        o_ref[...] = t.T.astype(BF16)

    @pl.when(kind == 2)
    def _():
        row = lax.broadcasted_iota(jnp.int32, (W_TN, 1), 0)
        o_ref[...] = jnp.where(row < KR_COLS, a_ref[...], 0.0).T.astype(BF16)


def _prep_w_in(w_t):
    d = w_t.shape[1]
    grid_spec = pltpu.PrefetchScalarGridSpec(
        num_scalar_prefetch=4,
        grid=(PROJ_COLS // W_TN,),
        in_specs=[
            pl.BlockSpec((W_TN, d), lambda j, a, b, k, s: (a[j], 0)),
            pl.BlockSpec((KR_COLS, d), lambda j, a, b, k, s: (b[j], 0)),
        ],
        out_specs=pl.BlockSpec((d, W_TN), lambda j, a, b, k, s: (0, j)),
    )
    return pl.pallas_call(
        _w_prep_kernel,
        grid_spec=grid_spec,
        out_shape=jax.ShapeDtypeStruct((d, PROJ_COLS), BF16),
        compiler_params=pltpu.CompilerParams(
            dimension_semantics=("arbitrary",), vmem_limit_bytes=VMEM_LIMIT),
        name="w_prep",
    )(*_w_in_plan(), w_t, w_t)


def _prep_w_uq_t(w):
    w = w.reshape(B_Q_LORA, B_HEADS, B_NOPE + B_ROPE) * ((B_NOPE + B_ROPE) ** -0.5 * LOG2E)
    w = jnp.pad(w, ((0, 0), (0, 0), (0, 2 * LANE - (B_NOPE + B_ROPE))))
    return w.reshape(B_Q_LORA, B_HEADS * 2 * LANE).T.astype(BF16)


def _prep_w_uv_t(w_ukv):
    w = w_ukv.reshape(B_KV_LORA, B_HEADS, B_NOPE + B_V_DIM)[:, :, B_NOPE:]
    w = jnp.pad(w, ((0, 0), (0, 0), (0, V_ROWS - B_V_DIM)))
    return w.reshape(B_KV_LORA, B_HEADS * V_ROWS).T.astype(BF16)


def _split3_bf16(c):
    c1 = c.astype(BF16).astype(F32)
    c2 = (c - c1).astype(BF16).astype(F32)
    c3 = (c - c1 - c2).astype(BF16).astype(F32)
    return c1, c2, c3


def _alibi_operands(positions, slopes, tq, tk):
    bsz, s = positions.shape
    nq, nk = s // tq, s // tk
    pmin = jnp.min(positions, axis=1, keepdims=True)
    rel = positions - pmin
    ok = ((jnp.min(rel, axis=1) >= 0) & (jnp.max(rel, axis=1) < 65536)
          & (jnp.max(jnp.abs(positions), axis=1) < (1 << 24)))
    qmin = jnp.min(positions.reshape(bsz, nq, tq), axis=-1)
    qmax = jnp.max(positions.reshape(bsz, nq, tq), axis=-1)
    kmin = jnp.min(positions.reshape(bsz, nk, tk), axis=-1)
    kmax = jnp.max(positions.reshape(bsz, nk, tk), axis=-1)
    after = qmin[:, :, None] >= kmax[:, None, :]
    before = qmax[:, :, None] <= kmin[:, None, :]
    mode = jnp.where(after, 1, jnp.where(before, -1, 0)) * ok[:, None, None].astype(jnp.int32)
    gap = jnp.where(after, qmin[:, :, None] - kmax[:, None, :], kmin[:, None, :] - qmax[:, :, None])
    dmin = jnp.where(mode != 0, gap, 0).astype(F32)
    off_diag = ~jnp.eye(nq, nk, dtype=bool)[None]
    fast = jnp.all((mode != 0) | ~off_diag, axis=-1) & (nq == nk) & (nk >= 4)

    rel = jnp.where(ok[:, None], rel, 0)
    hi = (rel >> 8).astype(F32)
    lo = (rel & 255).astype(F32)

    g = np.arange(LANE) % A_HEAD_DIM
    first, second = g < N_BIAS // 2, (g >= N_BIAS // 2) & (g < N_BIAS)
    even, odd = g % 2 == 0, g % 2 == 1
    f32 = lambda mask: jnp.asarray(mask, F32)
    q_hi, q_lo, q_one = f32(first & even), f32(first & odd), f32(second)
    k_hi, k_lo, k_one = f32(second & even), f32(second & odd), f32(first)
    hi_q, lo_q = hi.reshape(bsz, nq, 1, tq), lo.reshape(bsz, nq, 1, tq)
    col = lambda v: v[None, None, :, None]
    qb_t = (hi_q * col(q_hi) + lo_q * col(q_lo) + col(q_one)).astype(BF16)
    kb = (hi[:, :, None] * k_hi[None, None, :] + lo[:, :, None] * k_lo[None, None, :]
          + k_one[None, None, :]).astype(BF16)

    cb = slopes * LOG2E
    pieces = _split3_bf16(cb)
    cpair = jnp.stack([v for c in pieces for v in (256.0 * c, c)], axis=-1)
    sel = np.zeros((N_BIAS // 2, LANE), np.float32)
    for i in range(N_BIAS // 2):
        sel[i, (g % (N_BIAS // 2) == i) & (g < N_BIAS)] = 1.0
    spread = jnp.dot(cpair, jnp.asarray(sel), precision=lax.Precision.HIGHEST)
    cq = (spread * f32(second) + f32(first))[:, :, None]
    ck = (-spread * f32(first) + f32(second))[:, None, :]
    return (mode.reshape(-1).astype(jnp.int32), fast.reshape(-1).astype(jnp.int32), cb.astype(F32),
            jnp.pad(dmin, ((0, 0), (0, 0), (0, LANE - nk)))[:, :, None, :],
            qb_t, kb, cq, ck)


def kernel(x, positions, norm_g, w_in, lam_q1, lam_k1, lam_q2, lam_k2, a_subln_g, w_oa, q_norm_g,
           w_uq, kv_norm_g, w_ukv, w_ob, w_out, final_g):
    bsz, s, d = x.shape
    m = bsz * s
    assert norm_g.shape[0] == 1 and d == D_MODEL
    assert all(s % t == 0 for t in (TQ, TK, TQ_B, TK_B))
    layer = 0
    x2 = x.reshape(m, d)
    pos_f = positions.astype(F32)

    proj2 = _inproj(x2, norm_g[layer][None, :], _prep_w_in(w_in[layer].T))
    proj3 = proj2.reshape(bsz, s, PROJ_COLS)

    slopes = jnp.asarray([2.0 ** (-8.0 * (h + 1) / A_HEADS) for h in range(A_HEADS)], dtype=F32)
    mode, fast, cb, dmin, qb_t, kb, cq, ck = _alibi_operands(positions, slopes, TQ, TK)
    lamv = jnp.stack([lam_q1[layer], lam_k1[layer], lam_q2[layer], lam_k2[layer]]).astype(F32)
    za = _attn_a(mode, fast, cb, proj3, qb_t, kb, cq, ck, pos_f.reshape(bsz, s // TQ, 1, TQ),
                 pos_f[:, :, None], dmin, lamv, a_subln_g[layer][None, :], tq=TQ, tk=TK)

    half = B_ROPE // 2
    inv = ROPE_THETA ** (-jnp.arange(half, dtype=F32) / half)
    wuk = (w_ukv[layer].reshape(B_KV_LORA, B_HEADS, B_NOPE + B_V_DIM)[:, :, :B_NOPE]
           .reshape(B_KV_LORA, B_HEADS * B_NOPE).astype(BF16))
    vones = (jnp.arange(B_HEADS * V_ROWS) % V_ROWS == B_V_DIM).astype(F32)[:, None]
    qt3, kb2, vt3 = _mla_prep(proj2, pos_f.reshape(1, m), inv[:, None],
                              q_norm_g[layer][None, :], kv_norm_g[layer][None, :],
                              _prep_w_uq_t(w_uq[layer]), wuk, _prep_w_uv_t(w_ukv[layer]), vones,
                              tm=TQ_B, tkv=TK_B)
    zb = _attn_b(qt3, kb2.reshape(bsz, s, -1), vt3, proj3, tq=TQ_B, tk=TK_B)

    out = _post(za.reshape(m, A_WIDTH), zb.reshape(m, B_WIDTH), proj2, x2, w_oa[layer].astype(BF16),
                w_ob[layer].astype(BF16), w_out[layer].astype(BF16), final_g[None, :])
    return out.reshape(bsz, s, d)
```

```python
import functools
import math

import numpy as np
import jax
import jax.numpy as jnp
from jax import lax
from jax.experimental import pallas as pl
from jax.experimental.pallas import tpu as pltpu

F32 = jnp.float32
BF16 = jnp.bfloat16

D_MODEL = 2048
A_HEADS = 8
A_HEAD_DIM = 64
A_V_DIM = 128
A_WIDTH = 1024
B_HEADS = 8
B_Q_LORA = 512
B_KV_LORA = 512
B_NOPE = 128
B_ROPE = 64
B_V_DIM = 128
B_WIDTH = 1024
ROPE_THETA = 10000.0
NORM_EPS = 1e-6
SUBLN_EPS = 1e-5
LAM_INIT = 0.8 - 0.6 * math.exp(-0.3 * 0)
LOG2E = math.log2(math.e)

_SIZES = [1024, 1024, 1024, 1024, 512, 512, 64, 1024, 4096]
_SPLITS = [int(v) for v in np.cumsum(_SIZES)[:-1]]

LANE = 128
BF16_ROWS = 16
COL_AQ, COL_AK, COL_AV, COL_AG = 0, 8, 16, 24
COL_GA, COL_GB, COL_BG = 32, 48, 64
COL_CQ, COL_CKV, COL_KR = 72, 76, 80
PROJ_COLS = 10752

TQ = 512
TK = 512
TQ_B = 1024
TK_B = 512
V_ROWS = A_V_DIM + BF16_ROWS
N_BIAS = 12
VMEM_LIMIT = 56 * 1024 * 1024
M_INIT = -1e30
SPEC_MARGIN = 60.0
SKIP_LOG2 = 160.0
SKIP_SLACK = 1.01


def _dot_nt(a, b):
    return lax.dot_general(a, b, (((1,), (1,)), ((), ())), preferred_element_type=F32)


def _inproj_kernel(x_ref, g_ref, w_ref, o_ref, h_ref, *, tm):
    @pl.when(pl.program_id(1) == 0)
    def _():
        def body(r, c):
            rows = pl.ds(pl.multiple_of(r * 128, 128), 128)
            x = x_ref[rows, :]
            ms = jnp.mean(x * x, axis=-1, keepdims=True)
            h_ref[rows, :] = (x * lax.rsqrt(ms + NORM_EPS) * g_ref[...]).astype(BF16)
            return c
        lax.fori_loop(0, tm // 128, body, 0)

    o_ref[...] = jnp.dot(h_ref[...], w_ref[...], preferred_element_type=F32).astype(BF16)


def _inproj(x2, g, w, *, tm=1024, tn=1536):
    m = x2.shape[0]
    return pl.pallas_call(
        functools.partial(_inproj_kernel, tm=tm),
        grid=(m // tm, PROJ_COLS // tn),
        in_specs=[
            pl.BlockSpec((tm, D_MODEL), lambda i, j: (i, 0)),
            pl.BlockSpec((1, D_MODEL), lambda i, j: (0, 0)),
            pl.BlockSpec((D_MODEL, tn), lambda i, j: (0, j)),
        ],
        out_specs=pl.BlockSpec((tm, tn), lambda i, j: (i, j)),
        out_shape=jax.ShapeDtypeStruct((m, PROJ_COLS), BF16),
        scratch_shapes=[pltpu.VMEM((tm, D_MODEL), BF16)],
        compiler_params=pltpu.CompilerParams(
            dimension_semantics=("arbitrary", "arbitrary"), vmem_limit_bytes=VMEM_LIMIT),
        name="inproj",
    )(x2, g, w)


def _softmax_update(s, s_max, vt, m_ref, acc_ref):
    m_old = m_ref[...]
    m_new = jnp.maximum(m_old, s_max)
    p = jnp.exp2(s - m_new)
    alpha = jnp.exp2(m_old - m_new)
    acc_ref[...] = alpha * acc_ref[...] + jnp.dot(vt, p.astype(BF16), preferred_element_type=F32)
    m_ref[...] = m_new


def _normalised(acc_ref):
    return acc_ref[0:A_V_DIM, :] / acc_ref[A_V_DIM:A_V_DIM + 1, :]


def _attn_a_kernel(mode_ref, fast_ref, cb_ref, q_ref, k_ref, v_ref, ag_ref, qbt_ref, kb_ref,
                   cq_ref, ck_ref, posq_ref, posk_ref, dmin_ref, lam_ref, subg_ref, o_ref,
                   qa1, qa2, vt, m1, m2, acc1, acc2, exc_ref, kn_ref, list_ref, *, tq, tk, nq, nk):
    b = pl.program_id(0)
    cb = cb_ref[pl.program_id(1)]

    ones_rows = (lax.broadcasted_iota(jnp.int32, (BF16_ROWS, tk), 0) == 0).astype(BF16)
    feat = lax.broadcasted_iota(jnp.int32, (LANE, LANE), 0)
    col = lax.broadcasted_iota(jnp.int32, (LANE, LANE), 1)
    half_sum = (col == jnp.where(feat < A_HEAD_DIM, 0, 1)).astype(BF16)
    lane = lax.broadcasted_iota(jnp.int32, (1, LANE), 1)
    kn = [jnp.zeros((1, LANE), F32), jnp.zeros((1, LANE), F32)]
    for ki in range(nk):
        vt[ki, 0:A_V_DIM, :] = v_ref[ki * tk:(ki + 1) * tk, :].astype(F32).T.astype(BF16)
        vt[ki, A_V_DIM:V_ROWS, :] = ones_rows
        k = k_ref[ki * tk:(ki + 1) * tk, :].astype(F32)
        nsq = jnp.dot((k * k).astype(BF16), half_sum, preferred_element_type=F32)
        nmax = jnp.sqrt(jnp.max(nsq, axis=0, keepdims=True))
        for m in range(2):
            norm_m = jnp.max(jnp.where(lane == m, nmax, 0.0), axis=1, keepdims=True)
            kn[m] = jnp.where(lane == ki, norm_m, kn[m])
    kn_ref[0:1, :] = kn[0]
    kn_ref[1:2, :] = kn[1]

    def q_tiles(force_general):
        def q_tile(qi, c):
            _attn_a_q_tile(qi, b, cb, mode_ref, fast_ref, dmin_ref, q_ref, k_ref, ag_ref, qbt_ref,
                           kb_ref, cq_ref, ck_ref, posq_ref, posk_ref, lam_ref, subg_ref, o_ref,
                           qa1, qa2, vt, m1, m2, acc1, acc2, exc_ref, kn_ref, list_ref,
                           tq=tq, tk=tk, nq=nq, nk=nk, force_general=force_general,
                           try_skip=try_skip)
            return c
        lax.fori_loop(0, nq, q_tile, 0)

    exc_ref[...] = jnp.full(exc_ref.shape, M_INIT, F32)
    try_skip = cb * jnp.max(dmin_ref[...]) >= SKIP_LOG2
    q_tiles(False)

    @pl.when(jnp.max(exc_ref[...]) > SPEC_MARGIN)
    def _():
        q_tiles(True)


def _attn_a_q_tile(qi, b, cb, mode_ref, fast_ref, dmin_ref, q_ref, k_ref, ag_ref, qbt_ref, kb_ref,
                   cq_ref, ck_ref, posq_ref, posk_ref, lam_ref, subg_ref, o_ref,
                   qa1, qa2, vt, m1, m2, acc1, acc2, exc_ref, kn_ref, list_ref,
                   *, tq, tk, nq, nk, force_general, try_skip):
    rows = pl.ds(pl.multiple_of(qi * tq, tq), tq)
    posq = posq_ref[qi]

    lo_r = lax.broadcasted_iota(jnp.int32, (LANE, 1), 0) < A_HEAD_DIM
    qs = q_ref[rows, :].astype(F32).T
    qbias = qbt_ref[qi].astype(F32) * cq_ref[...]
    qa1[...] = jnp.where(lo_r, qs, qbias).astype(BF16)
    qa2[...] = jnp.where(lo_r, qbias, qs).astype(BF16)
    lo_c = lax.broadcasted_iota(jnp.int32, (1, LANE), 1) < A_HEAD_DIM
    qsq = qs * qs
    qn1 = jnp.max(jnp.sqrt(jnp.sum(jnp.where(lo_r, qsq, 0.0), axis=0, keepdims=True)),
                  axis=1, keepdims=True)
    qn2 = jnp.max(jnp.sqrt(jnp.sum(jnp.where(lo_r, 0.0, qsq), axis=0, keepdims=True)),
                  axis=1, keepdims=True)

    def raw_scores(ki, sgn):
        ks = pl.ds(pl.multiple_of(ki * tk, tk), tk)
        k = k_ref[ks, :]
        kbias = (kb_ref[ks, :].astype(F32) * (ck_ref[...] * sgn)).astype(BF16)
        s1 = jnp.dot(jnp.where(lo_c, k, kbias), qa1[...], preferred_element_type=F32)
        s2 = jnp.dot(jnp.where(lo_c, kbias, k), qa2[...], preferred_element_type=F32)
        return s1, s2

    def explicit_bias(ki):
        ks = pl.ds(pl.multiple_of(ki * tk, tk), tk)
        return -cb * jnp.abs(posk_ref[ks, :] - posq)

    def tile_scores(ki, general):
        if general:
            s1, s2 = raw_scores(ki, 0.0)
            bias = explicit_bias(ki)
            return s1 + bias, s2 + bias
        return raw_scores(ki, mode_ref[(b * nq + qi) * nk + ki].astype(F32))

    def col_max(s):
        return jnp.max(s, axis=0, keepdims=True)

    def pv(ki, s, shift):
        return jnp.dot(vt[ki], jnp.exp2(s - shift).astype(BF16), preferred_element_type=F32)

    fast = fast_ref[b * nq + qi]

    def speculative_pass():
        s1, s2 = tile_scores(qi, True)
        sh1, sh2 = col_max(s1), col_max(s2)
        m1[...] = sh1
        m2[...] = sh2
        acc1[...] = pv(qi, s1, sh1)
        acc2[...] = pv(qi, s2, sh2)

        def add_tile(t, off):
            s1, s2 = tile_scores(t, False)
            m1[...] = jnp.maximum(m1[...], col_max(s1) - off)
            m2[...] = jnp.maximum(m2[...], col_max(s2) - off)
            acc1[...] += pv(t, s1, sh1 + off)
            acc2[...] += pv(t, s2, sh2 + off)

        def with_skipping():
            lo1 = jnp.min(sh1, axis=1, keepdims=True)
            lo2 = jnp.min(sh2, axis=1, keepdims=True)
            gap = cb * dmin_ref[qi]
            ub1 = SKIP_SLACK * qn1 * kn_ref[0:1, :] + 1.0 - gap - lo1
            ub2 = SKIP_SLACK * qn2 * kn_ref[1:2, :] + 1.0 - gap - lo2
            near_a = jnp.where(qi == 0, 2, qi - 1)
            near_b = jnp.where(qi == nk - 1, nk - 3, qi + 1)
            lane = lax.broadcasted_iota(jnp.int32, (1, LANE), 1)
            rest = (lane != qi) & (lane != near_a) & (lane != near_b) & (lane < nk)
            keep = rest & ((ub1 > -SKIP_LOG2) | (ub2 > -SKIP_LOG2))
            bits = jnp.sum(jnp.where(keep, jnp.exp2(lane.astype(F32)), 0.0)).astype(jnp.int32)
            add_tile(near_a, 0.0)
            add_tile(near_b, 0.0)
            n = jnp.int32(0)
            for t in range(nk):
                list_ref[n] = jnp.int32(t)
                n = n + (lax.shift_right_logical(bits, t) & 1)
            list_ref[n] = qi

            n_rest = nk - 3
            short = n < n_rest

            @pl.when(short)
            def _():
                def pair(p, c):
                    for e in range(2):
                        idx = 2 * p + e
                        add_tile(list_ref[idx], jnp.where(idx < n, 0.0, -M_INIT))
                    return c
                lax.fori_loop(0, lax.shift_right_logical(n + 1, 1), pair, 0)

            @pl.when(jnp.logical_not(short))
            def _():
                for j in range(n_rest):
                    add_tile(list_ref[j], 0.0)

        def without_skipping():
            for j in range(nk - 1):
                add_tile(j + jnp.where(j >= qi, 1, 0), 0.0)

        pl.when(try_skip)(with_skipping)
        pl.when(jnp.logical_not(try_skip))(without_skipping)

        exc_ref[...] = jnp.maximum(exc_ref[...], jnp.maximum(m1[...] - sh1, m2[...] - sh2))

    def general_pass():
        for m_ref in (m1, m2):
            m_ref[...] = jnp.full(m_ref.shape, M_INIT, F32)
        for z_ref in (acc1, acc2):
            z_ref[...] = jnp.zeros(z_ref.shape, F32)

        def update(ki, general):
            for s, m_ref, acc_ref in zip(tile_scores(ki, general), (m1, m2), (acc1, acc2)):
                _softmax_update(s, col_max(s), vt[ki], m_ref, acc_ref)

        def step(ki, carry):
            mode = mode_ref[(b * nq + qi) * nk + ki]

            @pl.when(mode == 0)
            def _():
                update(ki, True)

            @pl.when(mode != 0)
            def _():
                update(ki, False)

            return carry

        lax.fori_loop(0, nk, step, 0)

    if force_general:
        general_pass()
    else:
        pl.when(fast == 1)(speculative_pass)
        pl.when(fast == 0)(general_pass)

    lamv = lam_ref[...]
    lam = (jnp.exp(jnp.sum(lamv[0:1] * lamv[1:2], keepdims=True))
           - jnp.exp(jnp.sum(lamv[2:3] * lamv[3:4], keepdims=True)) + LAM_INIT)
    o = _normalised(acc1) - lam * _normalised(acc2)
    ms = jnp.mean(o * o, axis=0, keepdims=True)
    on = (o * lax.rsqrt(ms + SUBLN_EPS)).T * subg_ref[...] * (1.0 - LAM_INIT)
    ag = ag_ref[rows, :].astype(F32)
    o_ref[rows, :] = (on * (ag * jax.nn.sigmoid(ag))).astype(BF16)


def _attn_a(mode, fast, cb, proj3, qbias_t, kbias, cq, ck, posq_row, posk_col, dmin, lamv, subg,
            *, tq, tk):
    bsz, s, _ = proj3.shape
    nq, nk = s // tq, s // tk
    assert nk <= 24
    kern = functools.partial(_attn_a_kernel, tq=tq, tk=tk, nq=nq, nk=nk)
    stat = pltpu.VMEM((1, tq), F32)
    grid_spec = pltpu.PrefetchScalarGridSpec(
        num_scalar_prefetch=3,
        grid=(bsz, A_HEADS),
        in_specs=[
            pl.BlockSpec((None, s, LANE), lambda b, h, *_: (b, 0, COL_AQ + h)),
            pl.BlockSpec((None, s, LANE), lambda b, h, *_: (b, 0, COL_AK + h)),
            pl.BlockSpec((None, s, LANE), lambda b, h, *_: (b, 0, COL_AV + h)),
            pl.BlockSpec((None, s, LANE), lambda b, h, *_: (b, 0, COL_AG + h)),
            pl.BlockSpec((None, nq, LANE, tq), lambda b, h, *_: (b, 0, 0, 0)),
            pl.BlockSpec((None, s, LANE), lambda b, h, *_: (b, 0, 0)),
            pl.BlockSpec((None, LANE, 1), lambda b, h, *_: (h, 0, 0)),
            pl.BlockSpec((None, 1, LANE), lambda b, h, *_: (h, 0, 0)),
            pl.BlockSpec((None, nq, 1, tq), lambda b, h, *_: (b, 0, 0, 0)),
            pl.BlockSpec((None, s, 1), lambda b, h, *_: (b, 0, 0)),
            pl.BlockSpec((None, nq, 1, LANE), lambda b, h, *_: (b, 0, 0, 0)),
            pl.BlockSpec((4, A_HEAD_DIM), lambda b, h, *_: (0, 0)),
            pl.BlockSpec((1, A_V_DIM), lambda b, h, *_: (0, 0)),
        ],
        out_specs=pl.BlockSpec((None, s, LANE), lambda b, h, *_: (b, 0, h)),
        scratch_shapes=[
            pltpu.VMEM((LANE, tq), BF16), pltpu.VMEM((LANE, tq), BF16),
            pltpu.VMEM((nk, V_ROWS, tk), BF16),
            stat, stat,
            pltpu.VMEM((V_ROWS, tq), F32), pltpu.VMEM((V_ROWS, tq), F32),
            stat, pltpu.VMEM((2, LANE), F32), pltpu.SMEM((nk + 1,), jnp.int32),
        ],
    )
    return pl.pallas_call(
        kern,
        grid_spec=grid_spec,
        out_shape=jax.ShapeDtypeStruct((bsz, s, A_WIDTH), BF16),
        compiler_params=pltpu.CompilerParams(
            dimension_semantics=("arbitrary", "arbitrary"), vmem_limit_bytes=VMEM_LIMIT),
        name="attn_a",
    )(mode, fast, cb, proj3, proj3, proj3, proj3, qbias_t, kbias, cq, ck, posq_row, posk_col, dmin,
      lamv, subg)


def _mla_prep_kernel(cq_ref, ckv_ref, kr_ref, posr_ref, invc_ref, qg_ref, kvg_ref,
                     wuqt_ref, wuk_ref, wuvt_ref, vones_ref, qo_ref, ko_ref, vo_ref):
    def norm(ref, g_ref):
        t = ref[...].astype(F32)
        ms = jnp.mean(t * t, axis=-1, keepdims=True)
        return (t * lax.rsqrt(ms + NORM_EPS) * g_ref[...]).astype(BF16)

    half = B_ROPE // 2
    cqn = norm(cq_ref, qg_ref)
    ckvn = norm(ckv_ref, kvg_ref)

    qt = _dot_nt(wuqt_ref[...], cqn)
    ang_t = invc_ref[...] * posr_ref[...]
    cs_t = jnp.cos(ang_t)
    sn_t = jnp.sin(ang_t)
    for h in range(B_HEADS):
        r0 = h * 2 * LANE
        qo_ref[r0:r0 + LANE, :] = qt[r0:r0 + LANE, :].astype(BF16)
        t1 = qt[r0 + LANE:r0 + LANE + half, :]
        t2 = qt[r0 + LANE + half:r0 + LANE + 2 * half, :]
        qo_ref[r0 + LANE:r0 + LANE + half, :] = (t1 * cs_t - t2 * sn_t).astype(BF16)
        qo_ref[r0 + LANE + half:r0 + LANE + 2 * half, :] = (t1 * sn_t + t2 * cs_t).astype(BF16)
        qo_ref[r0 + LANE + 2 * half:r0 + 2 * LANE, :] = jnp.zeros((LANE - 2 * half, qt.shape[1]), BF16)

    kr_t = kr_ref[...].astype(F32).T
    t1, t2 = kr_t[0:half, :], kr_t[half:2 * half, :]
    krope = jnp.concatenate(
        [t1 * cs_t - t2 * sn_t, t1 * sn_t + t2 * cs_t, kr_t[2 * half:LANE, :]], axis=0
    ).T.astype(BF16)
    kf = jnp.dot(ckvn, wuk_ref[...], preferred_element_type=F32)
    for h in range(B_HEADS):
        c0 = h * 2 * LANE
        ko_ref[:, c0:c0 + LANE] = kf[:, h * LANE:(h + 1) * LANE].astype(BF16)
        ko_ref[:, c0 + LANE:c0 + 2 * LANE] = krope

    vt = (_dot_nt(wuvt_ref[...], ckvn) + vones_ref[...]).astype(BF16)
    tkv = vo_ref.shape[2]
    for c in range(vo_ref.shape[0]):
        vo_ref[c] = vt[:, c * tkv:(c + 1) * tkv]


def _mla_prep(proj2, pos_row, inv_col, qg, kvg, wuqt, wuk, wuvt, vones, *, tm, tkv):
    m = proj2.shape[0]
    wq = B_HEADS * 2 * LANE
    wv = B_HEADS * V_ROWS
    return pl.pallas_call(
        _mla_prep_kernel,
        grid=(m // tm,),
        in_specs=[
            pl.BlockSpec((tm, B_Q_LORA), lambda i: (i, COL_CQ // 4)),
            pl.BlockSpec((tm, B_KV_LORA), lambda i: (i, COL_CKV // 4)),
            pl.BlockSpec((tm, LANE), lambda i: (i, COL_KR)),
            pl.BlockSpec((1, tm), lambda i: (0, i)),
            pl.BlockSpec((B_ROPE // 2, 1), lambda i: (0, 0)),
            pl.BlockSpec((1, B_Q_LORA), lambda i: (0, 0)),
            pl.BlockSpec((1, B_KV_LORA), lambda i: (0, 0)),
            pl.BlockSpec((wq, B_Q_LORA), lambda i: (0, 0)),
            pl.BlockSpec((B_KV_LORA, B_HEADS * B_NOPE), lambda i: (0, 0)),
            pl.BlockSpec((wv, B_KV_LORA), lambda i: (0, 0)),
            pl.BlockSpec((wv, 1), lambda i: (0, 0)),
        ],
        out_specs=[
            pl.BlockSpec((None, wq, tm), lambda i: (i, 0, 0)),
            pl.BlockSpec((tm, wq), lambda i: (i, 0)),
            pl.BlockSpec((tm // tkv, wv, tkv), lambda i: (i, 0, 0)),
        ],
        out_shape=[
            jax.ShapeDtypeStruct((m // tm, wq, tm), BF16),
            jax.ShapeDtypeStruct((m, wq), BF16),
            jax.ShapeDtypeStruct((m // tkv, wv, tkv), BF16),
        ],
        compiler_params=pltpu.CompilerParams(
            dimension_semantics=("arbitrary",), vmem_limit_bytes=VMEM_LIMIT),
        name="mla_prep",
    )(proj2, proj2, proj2, pos_row, inv_col, qg, kvg, wuqt, wuk, wuvt, vones)


def _attn_b_kernel(qt_ref, k_ref, vt_ref, bg_ref, o_ref, m1, acc1, exc_ref, *, tq, tk, nq, nk):
    def q_tiles(speculative):
        def q_tile(qi, c):
            _attn_b_q_tile(qi, qt_ref, k_ref, vt_ref, bg_ref, o_ref, m1, acc1, exc_ref,
                           tq=tq, tk=tk, nk=nk, speculative=speculative)
            return c
        lax.fori_loop(0, nq, q_tile, 0)

    exc_ref[...] = jnp.full(exc_ref.shape, M_INIT, F32)
    q_tiles(True)

    @pl.when(jnp.max(exc_ref[...]) > SPEC_MARGIN)
    def _():
        q_tiles(False)


def _attn_b_q_tile(qi, qt_ref, k_ref, vt_ref, bg_ref, o_ref, m1, acc1, exc_ref,
                   *, tq, tk, nk, speculative):
    rows = pl.ds(pl.multiple_of(qi * tq, tq), tq)

    def raw_scores(ki):
        ks = pl.ds(pl.multiple_of(ki * tk, tk), tk)
        return jnp.dot(k_ref[ks, :], qt_ref[qi], preferred_element_type=F32)

    half = tq // 2
    cols = (slice(0, half), slice(half, tq))

    def half_scores(ki):
        ks = pl.ds(pl.multiple_of(ki * tk, tk), tk)
        k = k_ref[ks, :]
        return [jnp.dot(k, qt_ref[qi, :, c], preferred_element_type=F32) for c in cols]

    if speculative:
        s2 = half_scores(0)
        shifts = [jnp.max(s, axis=0, keepdims=True) for s in s2]
        tops = list(shifts)
        for c, s, sh in zip(cols, s2, shifts):
            acc1[:, c] = jnp.dot(vt_ref[0], jnp.exp2(s - sh).astype(BF16),
                                 preferred_element_type=F32)
        for ki in range(1, nk):
            s2 = half_scores(ki)
            for j, (c, s, sh) in enumerate(zip(cols, s2, shifts)):
                tops[j] = jnp.maximum(tops[j], jnp.max(s, axis=0, keepdims=True))
                acc1[:, c] += jnp.dot(vt_ref[ki], jnp.exp2(s - sh).astype(BF16),
                                      preferred_element_type=F32)
        for c, t, sh in zip(cols, tops, shifts):
            exc_ref[:, c] = jnp.maximum(exc_ref[:, c], t - sh)
    else:
        m1[...] = jnp.full(m1.shape, M_INIT, F32)
        acc1[...] = jnp.zeros(acc1.shape, F32)

        def step(ki, carry):
            s = raw_scores(ki)
            _softmax_update(s, jnp.max(s, axis=0, keepdims=True), vt_ref[ki], m1, acc1)
            return carry

        lax.fori_loop(0, nk, step, 0)

    bg = bg_ref[rows, :].astype(F32)
    o_ref[rows, :] = (_normalised(acc1).T * (bg * jax.nn.sigmoid(bg))).astype(BF16)


def _attn_b(qt3, kb3, vt3, proj3, *, tq, tk):
    bsz, s, _ = kb3.shape
    nq, nk = s // tq, s // tk
    return pl.pallas_call(
        functools.partial(_attn_b_kernel, tq=tq, tk=tk, nq=nq, nk=nk),
        grid=(bsz, B_HEADS),
        in_specs=[
            pl.BlockSpec((nq, 2 * LANE, tq), lambda b, h: (b, h, 0)),
            pl.BlockSpec((None, s, 2 * LANE), lambda b, h: (b, 0, h)),
            pl.BlockSpec((nk, V_ROWS, tk), lambda b, h: (b, h, 0)),
            pl.BlockSpec((None, s, LANE), lambda b, h: (b, 0, COL_BG + h)),
        ],
        out_specs=pl.BlockSpec((None, s, LANE), lambda b, h: (b, 0, h)),
        out_shape=jax.ShapeDtypeStruct((bsz, s, B_WIDTH), BF16),
        scratch_shapes=[
            pltpu.VMEM((1, tq), F32), pltpu.VMEM((V_ROWS, tq), F32), pltpu.VMEM((1, tq), F32),
        ],
        compiler_params=pltpu.CompilerParams(
            dimension_semantics=("arbitrary", "arbitrary"), vmem_limit_bytes=VMEM_LIMIT),
        name="attn_b",
    )(qt3, kb3, vt3, proj3)


def _post_kernel(za_ref, zb_ref, ga_ref, gb_ref, x_ref, woa_ref, wob_ref, wout_ref, fg_ref, o_ref):
    ya = jnp.dot(za_ref[...], woa_ref[...], preferred_element_type=F32)
    yb = jnp.dot(zb_ref[...], wob_ref[...], preferred_element_type=F32)
    merged = (jax.nn.sigmoid(ga_ref[...].astype(F32)) * ya
              + jax.nn.sigmoid(gb_ref[...].astype(F32)) * yb)
    y = x_ref[...] + jnp.dot(merged.astype(BF16), wout_ref[...], preferred_element_type=F32)
    ms = jnp.mean(y * y, axis=-1, keepdims=True)
    o_ref[...] = y * lax.rsqrt(ms + NORM_EPS) * fg_ref[...]


def _post(za, zb, proj2, x2, woa, wob, wout, fg, *, tm=512):
    m = x2.shape[0]
    const = lambda shape: pl.BlockSpec(shape, lambda i: (0, 0), pipeline_mode=pl.Buffered(1))
    return pl.pallas_call(
        _post_kernel,
        grid=(m // tm,),
        in_specs=[
            pl.BlockSpec((tm, A_WIDTH), lambda i: (i, 0)),
            pl.BlockSpec((tm, B_WIDTH), lambda i: (i, 0)),
            pl.BlockSpec((tm, D_MODEL), lambda i: (i, COL_GA // 16)),
            pl.BlockSpec((tm, D_MODEL), lambda i: (i, COL_GB // 16)),
            pl.BlockSpec((tm, D_MODEL), lambda i: (i, 0)),
            const((A_WIDTH, D_MODEL)),
            const((B_WIDTH, D_MODEL)),
            const((D_MODEL, D_MODEL)),
            const((1, D_MODEL)),
        ],
        out_specs=pl.BlockSpec((tm, D_MODEL), lambda i: (i, 0)),
        out_shape=jax.ShapeDtypeStruct((m, D_MODEL), F32),
        compiler_params=pltpu.CompilerParams(
            dimension_semantics=("arbitrary",), vmem_limit_bytes=VMEM_LIMIT),
        name="post",
    )(za, zb, proj2, proj2, x2, woa, wob, wout, fg)


W_TN = 512
KR_COLS = _SIZES[6]


def _w_in_plan():
    off = dict(zip(("aq", "ak", "av", "ag", "cq", "ckv", "kr", "bg", "mg"),
                   [0] + _SPLITS))
    a_blk, b_blk, kind, scale = [], [], [], []
    for j in range(PROJ_COLS // W_TN):
        c0 = j * W_TN
        if c0 < COL_GA * LANE:
            src, k = off["aq"] + c0, 0
        elif c0 < COL_BG * LANE:
            src, k = off["mg"] + c0 - COL_GA * LANE, 1
        elif c0 < COL_CQ * LANE:
            src, k = off["bg"] + c0 - COL_BG * LANE, 1
        elif c0 < COL_KR * LANE:
            src, k = off["cq"] + c0 - COL_CQ * LANE, 0
        else:
            src, k = off["kr"], 2
        assert src % W_TN == (KR_COLS if k == 1 else 0)
        a_blk.append(src // W_TN)
        b_blk.append((src // W_TN + 1) * (W_TN // KR_COLS) if k == 1 else 0)
        kind.append(k)
        scale.append(A_HEAD_DIM ** -0.5 * LOG2E if c0 < COL_AK * LANE else 1.0)
    i32 = lambda v: jnp.asarray(v, jnp.int32)
    return i32(a_blk), i32(b_blk), i32(kind), jnp.asarray(scale, F32)


def _w_prep_kernel(ablk_ref, bblk_ref, kind_ref, scale_ref, a_ref, b_ref, o_ref):
    j = pl.program_id(0)
    kind = kind_ref[j]

    @pl.when(kind == 0)
    def _():
        o_ref[...] = (a_ref[...] * scale_ref[j]).T.astype(BF16)

    @pl.when(kind == 1)
    def _():
        t = jnp.concatenate([a_ref[KR_COLS:W_TN, :], b_ref[...]], axis=0)
        o_ref[...] = t.T.astype(BF16)

    @pl.when(kind == 2)
    def _():
        row = lax.broadcasted_iota(jnp.int32, (W_TN, 1), 0)
        o_ref[...] = jnp.where(row < KR_COLS, a_ref[...], 0.0).T.astype(BF16)


def _prep_w_in(w_t):
    d = w_t.shape[1]
    grid_spec = pltpu.PrefetchScalarGridSpec(
        num_scalar_prefetch=4,
        grid=(PROJ_COLS // W_TN,),
        in_specs=[
            pl.BlockSpec((W_TN, d), lambda j, a, b, k, s: (a[j], 0)),
            pl.BlockSpec((KR_COLS, d), lambda j, a, b, k, s: (b[j], 0)),
        ],
        out_specs=pl.BlockSpec((d, W_TN), lambda j, a, b, k, s: (0, j)),
    )
    return pl.pallas_call(
        _w_prep_kernel,
        grid_spec=grid_spec,
        out_shape=jax.ShapeDtypeStruct((d, PROJ_COLS), BF16),
        compiler_params=pltpu.CompilerParams(
            dimension_semantics=("arbitrary",), vmem_limit_bytes=VMEM_LIMIT),
        name="w_prep",
    )(*_w_in_plan(), w_t, w_t)


def _prep_w_uq_t(w):
    w = w.reshape(B_Q_LORA, B_HEADS, B_NOPE + B_ROPE) * ((B_NOPE + B_ROPE) ** -0.5 * LOG2E)
    w = jnp.pad(w, ((0, 0), (0, 0), (0, 2 * LANE - (B_NOPE + B_ROPE))))
    return w.reshape(B_Q_LORA, B_HEADS * 2 * LANE).T.astype(BF16)


def _prep_w_uv_t(w_ukv):
    w = w_ukv.reshape(B_KV_LORA, B_HEADS, B_NOPE + B_V_DIM)[:, :, B_NOPE:]
    w = jnp.pad(w, ((0, 0), (0, 0), (0, V_ROWS - B_V_DIM)))
    return w.reshape(B_KV_LORA, B_HEADS * V_ROWS).T.astype(BF16)


def _split3_bf16(c):
    c1 = c.astype(BF16).astype(F32)
    c2 = (c - c1).astype(BF16).astype(F32)
    c3 = (c - c1 - c2).astype(BF16).astype(F32)
    return c1, c2, c3


def _alibi_operands(positions, slopes, tq, tk):
    bsz, s = positions.shape
    nq, nk = s // tq, s // tk
    pmin = jnp.min(positions, axis=1, keepdims=True)
    rel = positions - pmin
    ok = ((jnp.min(rel, axis=1) >= 0) & (jnp.max(rel, axis=1) < 65536)
          & (jnp.max(jnp.abs(positions), axis=1) < (1 << 24)))
    qmin = jnp.min(positions.reshape(bsz, nq, tq), axis=-1)
    qmax = jnp.max(positions.reshape(bsz, nq, tq), axis=-1)
    kmin = jnp.min(positions.reshape(bsz, nk, tk), axis=-1)
    kmax = jnp.max(positions.reshape(bsz, nk, tk), axis=-1)
    after = qmin[:, :, None] >= kmax[:, None, :]
    before = qmax[:, :, None] <= kmin[:, None, :]
    mode = jnp.where(after, 1, jnp.where(before, -1, 0)) * ok[:, None, None].astype(jnp.int32)
    gap = jnp.where(after, qmin[:, :, None] - kmax[:, None, :], kmin[:, None, :] - qmax[:, :, None])
    dmin = jnp.where(mode != 0, gap, 0).astype(F32)
    off_diag = ~jnp.eye(nq, nk, dtype=bool)[None]
    fast = jnp.all((mode != 0) | ~off_diag, axis=-1) & (nq == nk) & (nk >= 4)

    rel = jnp.where(ok[:, None], rel, 0)
    hi = (rel >> 8).astype(F32)
    lo = (rel & 255).astype(F32)

    g = np.arange(LANE) % A_HEAD_DIM
    first, second = g < N_BIAS // 2, (g >= N_BIAS // 2) & (g < N_BIAS)
    even, odd = g % 2 == 0, g % 2 == 1
    f32 = lambda mask: jnp.asarray(mask, F32)
    q_hi, q_lo, q_one = f32(first & even), f32(first & odd), f32(second)
    k_hi, k_lo, k_one = f32(second & even), f32(second & odd), f32(first)
    hi_q, lo_q = hi.reshape(bsz, nq, 1, tq), lo.reshape(bsz, nq, 1, tq)
    col = lambda v: v[None, None, :, None]
    qb_t = (hi_q * col(q_hi) + lo_q * col(q_lo) + col(q_one)).astype(BF16)
    kb = (hi[:, :, None] * k_hi[None, None, :] + lo[:, :, None] * k_lo[None, None, :]
          + k_one[None, None, :]).astype(BF16)

    cb = slopes * LOG2E
    pieces = _split3_bf16(cb)
    cpair = jnp.stack([v for c in pieces for v in (256.0 * c, c)], axis=-1)
    sel = np.zeros((N_BIAS // 2, LANE), np.float32)
    for i in range(N_BIAS // 2):
        sel[i, (g % (N_BIAS // 2) == i) & (g < N_BIAS)] = 1.0
    spread = jnp.dot(cpair, jnp.asarray(sel), precision=lax.Precision.HIGHEST)
    cq = (spread * f32(second) + f32(first))[:, :, None]
    ck = (-spread * f32(first) + f32(second))[:, None, :]
    return (mode.reshape(-1).astype(jnp.int32), fast.reshape(-1).astype(jnp.int32), cb.astype(F32),
            jnp.pad(dmin, ((0, 0), (0, 0), (0, LANE - nk)))[:, :, None, :],
            qb_t, kb, cq, ck)


def kernel(x, positions, norm_g, w_in, lam_q1, lam_k1, lam_q2, lam_k2, a_subln_g, w_oa, q_norm_g,
           w_uq, kv_norm_g, w_ukv, w_ob, w_out, final_g):
    bsz, s, d = x.shape
    m = bsz * s
    assert norm_g.shape[0] == 1 and d == D_MODEL
    assert all(s % t == 0 for t in (TQ, TK, TQ_B, TK_B))
    layer = 0
    x2 = x.reshape(m, d)
    pos_f = positions.astype(F32)

    proj2 = _inproj(x2, norm_g[layer][None, :], _prep_w_in(w_in[layer].T))
    proj3 = proj2.reshape(bsz, s, PROJ_COLS)

    slopes = jnp.asarray([2.0 ** (-8.0 * (h + 1) / A_HEADS) for h in range(A_HEADS)], dtype=F32)
    mode, fast, cb, dmin, qb_t, kb, cq, ck = _alibi_operands(positions, slopes, TQ, TK)
    lamv = jnp.stack([lam_q1[layer], lam_k1[layer], lam_q2[layer], lam_k2[layer]]).astype(F32)
    za = _attn_a(mode, fast, cb, proj3, qb_t, kb, cq, ck, pos_f.reshape(bsz, s // TQ, 1, TQ),
                 pos_f[:, :, None], dmin, lamv, a_subln_g[layer][None, :], tq=TQ, tk=TK)

    half = B_ROPE // 2
    inv = ROPE_THETA ** (-jnp.arange(half, dtype=F32) / half)
    wuk = (w_ukv[layer].reshape(B_KV_LORA, B_HEADS, B_NOPE + B_V_DIM)[:, :, :B_NOPE]
           .reshape(B_KV_LORA, B_HEADS * B_NOPE).astype(BF16))
    vones = (jnp.arange(B_HEADS * V_ROWS) % V_ROWS == B_V_DIM).astype(F32)[:, None]
    qt3, kb2, vt3 = _mla_prep(proj2, pos_f.reshape(1, m), inv[:, None],
                              q_norm_g[layer][None, :], kv_norm_g[layer][None, :],
                              _prep_w_uq_t(w_uq[layer]), wuk, _prep_w_uv_t(w_ukv[layer]), vones,
                              tm=TQ_B, tkv=TK_B)
    zb = _attn_b(qt3, kb2.reshape(bsz, s, -1), vt3, proj3, tq=TQ_B, tk=TK_B)

    out = _post(za.reshape(m, A_WIDTH), zb.reshape(m, B_WIDTH), proj2, x2, w_oa[layer].astype(BF16),
                w_ob[layer].astype(BF16), w_out[layer].astype(BF16), final_g[None, :])
    return out.reshape(bsz, s, d)
```

```python
import functools
import math

import numpy as np
import jax
import jax.numpy as jnp
from jax import lax
from jax.experimental import pallas as pl
from jax.experimental.pallas import tpu as pltpu

F32 = jnp.float32
BF16 = jnp.bfloat16

D_MODEL = 2048
A_HEADS = 8
A_HEAD_DIM = 64
A_V_DIM = 128
A_WIDTH = 1024
B_HEADS = 8
B_Q_LORA = 512
B_KV_LORA = 512
B_NOPE = 128
B_ROPE = 64
B_V_DIM = 128
B_WIDTH = 1024
ROPE_THETA = 10000.0
NORM_EPS = 1e-6
SUBLN_EPS = 1e-5
LAM_INIT = 0.8 - 0.6 * math.exp(-0.3 * 0)
LOG2E = math.log2(math.e)

_SIZES = [1024, 1024, 1024, 1024, 512, 512, 64, 1024, 4096]
_SPLITS = [int(v) for v in np.cumsum(_SIZES)[:-1]]

LANE = 128
BF16_ROWS = 16
COL_AQ, COL_AK, COL_AV, COL_AG = 0, 8, 16, 24
COL_GA, COL_GB, COL_BG = 32, 48, 64
COL_CQ, COL_CKV, COL_KR = 72, 76, 80
PROJ_COLS = 10752

TQ = 512
TK = 512
TQ_B = 1024
TK_B = 512
V_ROWS = A_V_DIM + BF16_ROWS
N_BIAS = 12
VMEM_LIMIT = 56 * 1024 * 1024
M_INIT = -1e30
SPEC_MARGIN = 60.0
SKIP_LOG2 = 160.0
SKIP_SLACK = 1.01


def _dot_nt(a, b):
    return lax.dot_general(a, b, (((1,), (1,)), ((), ())), preferred_element_type=F32)


def _inproj_kernel(x_ref, g_ref, w_ref, o_ref, h_ref, *, tm):
    @pl.when(pl.program_id(1) == 0)
    def _():
        def body(r, c):
            rows = pl.ds(pl.multiple_of(r * 128, 128), 128)
            x = x_ref[rows, :]
            ms = jnp.mean(x * x, axis=-1, keepdims=True)
            h_ref[rows, :] = (x * lax.rsqrt(ms + NORM_EPS) * g_ref[...]).astype(BF16)
            return c
        lax.fori_loop(0, tm // 128, body, 0)

    o_ref[...] = jnp.dot(h_ref[...], w_ref[...], preferred_element_type=F32).astype(BF16)


def _inproj(x2, g, w, *, tm=1024, tn=1792):
    m = x2.shape[0]
    return pl.pallas_call(
        functools.partial(_inproj_kernel, tm=tm),
        grid=(m // tm, PROJ_COLS // tn),
        in_specs=[
            pl.BlockSpec((tm, D_MODEL), lambda i, j: (i, 0)),
            pl.BlockSpec((1, D_MODEL), lambda i, j: (0, 0)),
            pl.BlockSpec((D_MODEL, tn), lambda i, j: (0, j)),
        ],
        out_specs=pl.BlockSpec((tm, tn), lambda i, j: (i, j)),
        out_shape=jax.ShapeDtypeStruct((m, PROJ_COLS), BF16),
        scratch_shapes=[pltpu.VMEM((tm, D_MODEL), BF16)],
        compiler_params=pltpu.CompilerParams(
            dimension_semantics=("arbitrary", "arbitrary"), vmem_limit_bytes=VMEM_LIMIT),
        name="inproj",
    )(x2, g, w)


def _softmax_update(s, s_max, vt, m_ref, acc_ref):
    m_old = m_ref[...]
    m_new = jnp.maximum(m_old, s_max)
    p = jnp.exp2(s - m_new)
    alpha = jnp.exp2(m_old - m_new)
    acc_ref[...] = alpha * acc_ref[...] + jnp.dot(vt, p.astype(BF16), preferred_element_type=F32)
    m_ref[...] = m_new


def _normalised(acc_ref):
    return acc_ref[0:A_V_DIM, :] / acc_ref[A_V_DIM:A_V_DIM + 1, :]


def _attn_a_kernel(mode_ref, fast_ref, cb_ref, q_ref, k_ref, v_ref, ag_ref, qbt_ref, kb_ref,
                   cq_ref, ck_ref, posq_ref, posk_ref, dmin_ref, lam_ref, subg_ref, o_ref,
                   qa1, qa2, vt, m1, m2, acc1, acc2, exc_ref, kn_ref, list_ref, *, tq, tk, nq, nk):
    b = pl.program_id(0)
    cb = cb_ref[pl.program_id(1)]

    ones_rows = (lax.broadcasted_iota(jnp.int32, (BF16_ROWS, tk), 0) == 0).astype(BF16)
    feat = lax.broadcasted_iota(jnp.int32, (LANE, LANE), 0)
    col = lax.broadcasted_iota(jnp.int32, (LANE, LANE), 1)
    half_sum = (col == jnp.where(feat < A_HEAD_DIM, 0, 1)).astype(BF16)
    lane = lax.broadcasted_iota(jnp.int32, (1, LANE), 1)
    kn = [jnp.zeros((1, LANE), F32), jnp.zeros((1, LANE), F32)]
    for ki in range(nk):
        vt[ki, 0:A_V_DIM, :] = v_ref[ki * tk:(ki + 1) * tk, :].astype(F32).T.astype(BF16)
        vt[ki, A_V_DIM:V_ROWS, :] = ones_rows
        k = k_ref[ki * tk:(ki + 1) * tk, :].astype(F32)
        nsq = jnp.dot((k * k).astype(BF16), half_sum, preferred_element_type=F32)
        nmax = jnp.sqrt(jnp.max(nsq, axis=0, keepdims=True))
        for m in range(2):
            norm_m = jnp.max(jnp.where(lane == m, nmax, 0.0), axis=1, keepdims=True)
            kn[m] = jnp.where(lane == ki, norm_m, kn[m])
    kn_ref[0:1, :] = kn[0]
    kn_ref[1:2, :] = kn[1]

    def q_tiles(force_general):
        def q_tile(qi, c):
            _attn_a_q_tile(qi, b, cb, mode_ref, fast_ref, dmin_ref, q_ref, k_ref, ag_ref, qbt_ref,
                           kb_ref, cq_ref, ck_ref, posq_ref, posk_ref, lam_ref, subg_ref, o_ref,
                           qa1, qa2, vt, m1, m2, acc1, acc2, exc_ref, kn_ref, list_ref,
                           tq=tq, tk=tk, nq=nq, nk=nk, force_general=force_general)
            return c
        lax.fori_loop(0, nq, q_tile, 0)

    exc_ref[...] = jnp.full(exc_ref.shape, M_INIT, F32)
    q_tiles(False)

    @pl.when(jnp.max(exc_ref[...]) > SPEC_MARGIN)
    def _():
        q_tiles(True)


def _attn_a_q_tile(qi, b, cb, mode_ref, fast_ref, dmin_ref, q_ref, k_ref, ag_ref, qbt_ref, kb_ref,
                   cq_ref, ck_ref, posq_ref, posk_ref, lam_ref, subg_ref, o_ref,
                   qa1, qa2, vt, m1, m2, acc1, acc2, exc_ref, kn_ref, list_ref,
                   *, tq, tk, nq, nk, force_general):
    rows = pl.ds(pl.multiple_of(qi * tq, tq), tq)
    posq = posq_ref[qi]

    lo_r = lax.broadcasted_iota(jnp.int32, (LANE, 1), 0) < A_HEAD_DIM
    qs = q_ref[rows, :].astype(F32).T
    qbias = qbt_ref[qi].astype(F32) * cq_ref[...]
    qa1[...] = jnp.where(lo_r, qs, qbias).astype(BF16)
    qa2[...] = jnp.where(lo_r, qbias, qs).astype(BF16)
    lo_c = lax.broadcasted_iota(jnp.int32, (1, LANE), 1) < A_HEAD_DIM
    qsq = qs * qs
    qn1 = jnp.max(jnp.sqrt(jnp.sum(jnp.where(lo_r, qsq, 0.0), axis=0, keepdims=True)),
                  axis=1, keepdims=True)
    qn2 = jnp.max(jnp.sqrt(jnp.sum(jnp.where(lo_r, 0.0, qsq), axis=0, keepdims=True)),
                  axis=1, keepdims=True)

    def raw_scores(ki, sgn):
        ks = pl.ds(pl.multiple_of(ki * tk, tk), tk)
        k = k_ref[ks, :]
        kbias = (kb_ref[ks, :].astype(F32) * (ck_ref[...] * sgn)).astype(BF16)
        s1 = jnp.dot(jnp.where(lo_c, k, kbias), qa1[...], preferred_element_type=F32)
        s2 = jnp.dot(jnp.where(lo_c, kbias, k), qa2[...], preferred_element_type=F32)
        return s1, s2

    def explicit_bias(ki):
        ks = pl.ds(pl.multiple_of(ki * tk, tk), tk)
        return -cb * jnp.abs(posk_ref[ks, :] - posq)

    def tile_scores(ki, general):
        if general:
            s1, s2 = raw_scores(ki, 0.0)
            bias = explicit_bias(ki)
            return s1 + bias, s2 + bias
        return raw_scores(ki, mode_ref[(b * nq + qi) * nk + ki].astype(F32))

    def col_max(s):
        return jnp.max(s, axis=0, keepdims=True)

    def pv(ki, s, shift):
        return jnp.dot(vt[ki], jnp.exp2(s - shift).astype(BF16), preferred_element_type=F32)

    fast = fast_ref[b * nq + qi]

    def speculative_pass():
        s1, s2 = tile_scores(qi, True)
        sh1, sh2 = col_max(s1), col_max(s2)
        m1[...] = sh1
        m2[...] = sh2
        acc1[...] = pv(qi, s1, sh1)
        acc2[...] = pv(qi, s2, sh2)

        def add_tile(t, off):
            s1, s2 = tile_scores(t, False)
            m1[...] = jnp.maximum(m1[...], col_max(s1) - off)
            m2[...] = jnp.maximum(m2[...], col_max(s2) - off)
            acc1[...] += pv(t, s1, sh1 + off)
            acc2[...] += pv(t, s2, sh2 + off)

        lo1 = jnp.min(sh1, axis=1, keepdims=True)
        lo2 = jnp.min(sh2, axis=1, keepdims=True)
        gap = cb * dmin_ref[qi]
        ub1 = SKIP_SLACK * qn1 * kn_ref[0:1, :] + 1.0 - gap - lo1
        ub2 = SKIP_SLACK * qn2 * kn_ref[1:2, :] + 1.0 - gap - lo2
        near_a = jnp.where(qi == 0, 2, qi - 1)
        near_b = jnp.where(qi == nk - 1, nk - 3, qi + 1)
        lane = lax.broadcasted_iota(jnp.int32, (1, LANE), 1)
        rest = (lane != qi) & (lane != near_a) & (lane != near_b) & (lane < nk)
        keep = rest & ((ub1 > -SKIP_LOG2) | (ub2 > -SKIP_LOG2))
        bits = jnp.sum(jnp.where(keep, jnp.exp2(lane.astype(F32)), 0.0)).astype(jnp.int32)
        add_tile(near_a, 0.0)
        add_tile(near_b, 0.0)
        n = jnp.int32(0)
        for t in range(nk):
            list_ref[n] = jnp.int32(t)
            n = n + (lax.shift_right_logical(bits, t) & 1)
        list_ref[n] = qi

        n_rest = nk - 3
        short = n < n_rest

        @pl.when(short)
        def _():
            def pair(p, c):
                for e in range(2):
                    idx = 2 * p + e
                    add_tile(list_ref[idx], jnp.where(idx < n, 0.0, -M_INIT))
                return c
            lax.fori_loop(0, lax.shift_right_logical(n + 1, 1), pair, 0)

        @pl.when(jnp.logical_not(short))
        def _():
            for j in range(n_rest):
                add_tile(list_ref[j], 0.0)

        exc_ref[...] = jnp.maximum(exc_ref[...], jnp.maximum(m1[...] - sh1, m2[...] - sh2))

    def general_pass():
        for m_ref in (m1, m2):
            m_ref[...] = jnp.full(m_ref.shape, M_INIT, F32)
        for z_ref in (acc1, acc2):
            z_ref[...] = jnp.zeros(z_ref.shape, F32)

        def update(ki, general):
            for s, m_ref, acc_ref in zip(tile_scores(ki, general), (m1, m2), (acc1, acc2)):
                _softmax_update(s, col_max(s), vt[ki], m_ref, acc_ref)

        def step(ki, carry):
            mode = mode_ref[(b * nq + qi) * nk + ki]

            @pl.when(mode == 0)
            def _():
                update(ki, True)

            @pl.when(mode != 0)
            def _():
                update(ki, False)

            return carry

        lax.fori_loop(0, nk, step, 0)

    if force_general:
        general_pass()
    else:
        pl.when(fast == 1)(speculative_pass)
        pl.when(fast == 0)(general_pass)

    lamv = lam_ref[...]
    lam = (jnp.exp(jnp.sum(lamv[0:1] * lamv[1:2], keepdims=True))
           - jnp.exp(jnp.sum(lamv[2:3] * lamv[3:4], keepdims=True)) + LAM_INIT)
    o = _normalised(acc1) - lam * _normalised(acc2)
    ms = jnp.mean(o * o, axis=0, keepdims=True)
    on = (o * lax.rsqrt(ms + SUBLN_EPS)).T * subg_ref[...] * (1.0 - LAM_INIT)
    ag = ag_ref[rows, :].astype(F32)
    o_ref[rows, :] = (on * (ag * jax.nn.sigmoid(ag))).astype(BF16)


def _attn_a(mode, fast, cb, proj3, qbias_t, kbias, cq, ck, posq_row, posk_col, dmin, lamv, subg,
            *, tq, tk):
    bsz, s, _ = proj3.shape
    nq, nk = s // tq, s // tk
    assert nk <= 24
    kern = functools.partial(_attn_a_kernel, tq=tq, tk=tk, nq=nq, nk=nk)
    stat = pltpu.VMEM((1, tq), F32)
    grid_spec = pltpu.PrefetchScalarGridSpec(
        num_scalar_prefetch=3,
        grid=(bsz, A_HEADS),
        in_specs=[
            pl.BlockSpec((None, s, LANE), lambda b, h, *_: (b, 0, COL_AQ + h)),
            pl.BlockSpec((None, s, LANE), lambda b, h, *_: (b, 0, COL_AK + h)),
            pl.BlockSpec((None, s, LANE), lambda b, h, *_: (b, 0, COL_AV + h)),
            pl.BlockSpec((None, s, LANE), lambda b, h, *_: (b, 0, COL_AG + h)),
            pl.BlockSpec((None, nq, LANE, tq), lambda b, h, *_: (b, 0, 0, 0)),
            pl.BlockSpec((None, s, LANE), lambda b, h, *_: (b, 0, 0)),
            pl.BlockSpec((None, LANE, 1), lambda b, h, *_: (h, 0, 0)),
            pl.BlockSpec((None, 1, LANE), lambda b, h, *_: (h, 0, 0)),
            pl.BlockSpec((None, nq, 1, tq), lambda b, h, *_: (b, 0, 0, 0)),
            pl.BlockSpec((None, s, 1), lambda b, h, *_: (b, 0, 0)),
            pl.BlockSpec((None, nq, 1, LANE), lambda b, h, *_: (b, 0, 0, 0)),
            pl.BlockSpec((4, A_HEAD_DIM), lambda b, h, *_: (0, 0)),
            pl.BlockSpec((1, A_V_DIM), lambda b, h, *_: (0, 0)),
        ],
        out_specs=pl.BlockSpec((None, s, LANE), lambda b, h, *_: (b, 0, h)),
        scratch_shapes=[
            pltpu.VMEM((LANE, tq), BF16), pltpu.VMEM((LANE, tq), BF16),
            pltpu.VMEM((nk, V_ROWS, tk), BF16),
            stat, stat,
            pltpu.VMEM((V_ROWS, tq), F32), pltpu.VMEM((V_ROWS, tq), F32),
            stat, pltpu.VMEM((2, LANE), F32), pltpu.SMEM((nk + 1,), jnp.int32),
        ],
    )
    return pl.pallas_call(
        kern,
        grid_spec=grid_spec,
        out_shape=jax.ShapeDtypeStruct((bsz, s, A_WIDTH), BF16),
        compiler_params=pltpu.CompilerParams(
            dimension_semantics=("arbitrary", "arbitrary"), vmem_limit_bytes=VMEM_LIMIT),
        name="attn_a",
    )(mode, fast, cb, proj3, proj3, proj3, proj3, qbias_t, kbias, cq, ck, posq_row, posk_col, dmin,
      lamv, subg)


def _mla_prep_kernel(cq_ref, ckv_ref, kr_ref, posr_ref, invc_ref, qg_ref, kvg_ref,
                     wuqt_ref, wuk_ref, wuvt_ref, vones_ref, qo_ref, ko_ref, vo_ref):
    def norm(ref, g_ref):
        t = ref[...].astype(F32)
        ms = jnp.mean(t * t, axis=-1, keepdims=True)
        return (t * lax.rsqrt(ms + NORM_EPS) * g_ref[...]).astype(BF16)

    half = B_ROPE // 2
    cqn = norm(cq_ref, qg_ref)
    ckvn = norm(ckv_ref, kvg_ref)

    qt = _dot_nt(wuqt_ref[...], cqn)
    ang_t = invc_ref[...] * posr_ref[...]
    cs_t = jnp.cos(ang_t)
    sn_t = jnp.sin(ang_t)
    for h in range(B_HEADS):
        r0 = h * 2 * LANE
        qo_ref[r0:r0 + LANE, :] = qt[r0:r0 + LANE, :].astype(BF16)
        t1 = qt[r0 + LANE:r0 + LANE + half, :]
        t2 = qt[r0 + LANE + half:r0 + LANE + 2 * half, :]
        qo_ref[r0 + LANE:r0 + LANE + half, :] = (t1 * cs_t - t2 * sn_t).astype(BF16)
        qo_ref[r0 + LANE + half:r0 + LANE + 2 * half, :] = (t1 * sn_t + t2 * cs_t).astype(BF16)
        qo_ref[r0 + LANE + 2 * half:r0 + 2 * LANE, :] = jnp.zeros((LANE - 2 * half, qt.shape[1]), BF16)

    kr_t = kr_ref[...].astype(F32).T
    t1, t2 = kr_t[0:half, :], kr_t[half:2 * half, :]
    krope = jnp.concatenate(
        [t1 * cs_t - t2 * sn_t, t1 * sn_t + t2 * cs_t, kr_t[2 * half:LANE, :]], axis=0
    ).T.astype(BF16)
    kf = jnp.dot(ckvn, wuk_ref[...], preferred_element_type=F32)
    for h in range(B_HEADS):
        c0 = h * 2 * LANE
        ko_ref[:, c0:c0 + LANE] = kf[:, h * LANE:(h + 1) * LANE].astype(BF16)
        ko_ref[:, c0 + LANE:c0 + 2 * LANE] = krope

    vt = (_dot_nt(wuvt_ref[...], ckvn) + vones_ref[...]).astype(BF16)
    tkv = vo_ref.shape[2]
    for c in range(vo_ref.shape[0]):
        vo_ref[c] = vt[:, c * tkv:(c + 1) * tkv]


def _mla_prep(proj2, pos_row, inv_col, qg, kvg, wuqt, wuk, wuvt, vones, *, tm, tkv):
    m = proj2.shape[0]
    wq = B_HEADS * 2 * LANE
    wv = B_HEADS * V_ROWS
    return pl.pallas_call(
        _mla_prep_kernel,
        grid=(m // tm,),
        in_specs=[
            pl.BlockSpec((tm, B_Q_LORA), lambda i: (i, COL_CQ // 4)),
            pl.BlockSpec((tm, B_KV_LORA), lambda i: (i, COL_CKV // 4)),
            pl.BlockSpec((tm, LANE), lambda i: (i, COL_KR)),
            pl.BlockSpec((1, tm), lambda i: (0, i)),
            pl.BlockSpec((B_ROPE // 2, 1), lambda i: (0, 0)),
            pl.BlockSpec((1, B_Q_LORA), lambda i: (0, 0)),
            pl.BlockSpec((1, B_KV_LORA), lambda i: (0, 0)),
            pl.BlockSpec((wq, B_Q_LORA), lambda i: (0, 0)),
            pl.BlockSpec((B_KV_LORA, B_HEADS * B_NOPE), lambda i: (0, 0)),
            pl.BlockSpec((wv, B_KV_LORA), lambda i: (0, 0)),
            pl.BlockSpec((wv, 1), lambda i: (0, 0)),
        ],
        out_specs=[
            pl.BlockSpec((None, wq, tm), lambda i: (i, 0, 0)),
            pl.BlockSpec((tm, wq), lambda i: (i, 0)),
            pl.BlockSpec((tm // tkv, wv, tkv), lambda i: (i, 0, 0)),
        ],
        out_shape=[
            jax.ShapeDtypeStruct((m // tm, wq, tm), BF16),
            jax.ShapeDtypeStruct((m, wq), BF16),
            jax.ShapeDtypeStruct((m // tkv, wv, tkv), BF16),
        ],
        compiler_params=pltpu.CompilerParams(
            dimension_semantics=("arbitrary",), vmem_limit_bytes=VMEM_LIMIT),
        name="mla_prep",
    )(proj2, proj2, proj2, pos_row, inv_col, qg, kvg, wuqt, wuk, wuvt, vones)


def _attn_b_kernel(qt_ref, k_ref, vt_ref, bg_ref, o_ref, m1, acc1, flag_ref, *, tq, tk, nq, nk):
    def q_tile(qi, c):
        _attn_b_q_tile(qi, qt_ref, k_ref, vt_ref, bg_ref, o_ref, m1, acc1, flag_ref,
                       tq=tq, tk=tk, nk=nk)
        return c
    lax.fori_loop(0, nq, q_tile, 0)


def _attn_b_q_tile(qi, qt_ref, k_ref, vt_ref, bg_ref, o_ref, m1, acc1, flag_ref, *, tq, tk, nk):
    rows = pl.ds(pl.multiple_of(qi * tq, tq), tq)

    def raw_scores(ki):
        ks = pl.ds(pl.multiple_of(ki * tk, tk), tk)
        return jnp.dot(k_ref[ks, :], qt_ref[qi], preferred_element_type=F32)

    half = tq // 2
    cols = (slice(0, half), slice(half, tq))

    def half_scores(ki):
        ks = pl.ds(pl.multiple_of(ki * tk, tk), tk)
        k = k_ref[ks, :]
        return [jnp.dot(k, qt_ref[qi, :, c], preferred_element_type=F32) for c in cols]

    s2 = half_scores(0)
    shifts = [jnp.max(s, axis=0, keepdims=True) for s in s2]
    tops = list(shifts)
    for c, s, sh in zip(cols, s2, shifts):
        acc1[:, c] = jnp.dot(vt_ref[0], jnp.exp2(s - sh).astype(BF16), preferred_element_type=F32)
    for ki in range(1, nk):
        s2 = half_scores(ki)
        for j, (c, s, sh) in enumerate(zip(cols, s2, shifts)):
            tops[j] = jnp.maximum(tops[j], jnp.max(s, axis=0, keepdims=True))
            acc1[:, c] += jnp.dot(vt_ref[ki], jnp.exp2(s - sh).astype(BF16),
                                  preferred_element_type=F32)
    excess = jnp.maximum(jnp.max(tops[0] - shifts[0]), jnp.max(tops[1] - shifts[1]))
    flag_ref[0] = (excess > SPEC_MARGIN).astype(jnp.int32)

    @pl.when(flag_ref[0] == 1)
    def _():
        m1[...] = jnp.full(m1.shape, M_INIT, F32)
        acc1[...] = jnp.zeros(acc1.shape, F32)

        def step(ki, carry):
            s = raw_scores(ki)
            _softmax_update(s, jnp.max(s, axis=0, keepdims=True), vt_ref[ki], m1, acc1)
            return carry

        lax.fori_loop(0, nk, step, 0)

    bg = bg_ref[rows, :].astype(F32)
    o_ref[rows, :] = (_normalised(acc1).T * (bg * jax.nn.sigmoid(bg))).astype(BF16)


def _attn_b(qt3, kb3, vt3, proj3, *, tq, tk):
    bsz, s, _ = kb3.shape
    nq, nk = s // tq, s // tk
    return pl.pallas_call(
        functools.partial(_attn_b_kernel, tq=tq, tk=tk, nq=nq, nk=nk),
        grid=(bsz, B_HEADS),
        in_specs=[
            pl.BlockSpec((nq, 2 * LANE, tq), lambda b, h: (b, h, 0)),
            pl.BlockSpec((None, s, 2 * LANE), lambda b, h: (b, 0, h)),
            pl.BlockSpec((nk, V_ROWS, tk), lambda b, h: (b, h, 0)),
            pl.BlockSpec((None, s, LANE), lambda b, h: (b, 0, COL_BG + h)),
        ],
        out_specs=pl.BlockSpec((None, s, LANE), lambda b, h: (b, 0, h)),
        out_shape=jax.ShapeDtypeStruct((bsz, s, B_WIDTH), BF16),
        scratch_shapes=[
            pltpu.VMEM((1, tq), F32), pltpu.VMEM((V_ROWS, tq), F32), pltpu.SMEM((1,), jnp.int32),
        ],
        compiler_params=pltpu.CompilerParams(
            dimension_semantics=("arbitrary", "arbitrary"), vmem_limit_bytes=VMEM_LIMIT),
        name="attn_b",
    )(qt3, kb3, vt3, proj3)


def _post_kernel(za_ref, zb_ref, ga_ref, gb_ref, x_ref, woa_ref, wob_ref, wout_ref, fg_ref, o_ref):
    ya = jnp.dot(za_ref[...], woa_ref[...], preferred_element_type=F32)
    yb = jnp.dot(zb_ref[...], wob_ref[...], preferred_element_type=F32)
    merged = (jax.nn.sigmoid(ga_ref[...].astype(F32)) * ya
              + jax.nn.sigmoid(gb_ref[...].astype(F32)) * yb)
    y = x_ref[...] + jnp.dot(merged.astype(BF16), wout_ref[...], preferred_element_type=F32)
    ms = jnp.mean(y * y, axis=-1, keepdims=True)
    o_ref[...] = y * lax.rsqrt(ms + NORM_EPS) * fg_ref[...]


def _post(za, zb, proj2, x2, woa, wob, wout, fg, *, tm=512):
    m = x2.shape[0]
    const = lambda shape: pl.BlockSpec(shape, lambda i: (0, 0), pipeline_mode=pl.Buffered(1))
    return pl.pallas_call(
        _post_kernel,
        grid=(m // tm,),
        in_specs=[
            pl.BlockSpec((tm, A_WIDTH), lambda i: (i, 0)),
            pl.BlockSpec((tm, B_WIDTH), lambda i: (i, 0)),
            pl.BlockSpec((tm, D_MODEL), lambda i: (i, COL_GA // 16)),
            pl.BlockSpec((tm, D_MODEL), lambda i: (i, COL_GB // 16)),
            pl.BlockSpec((tm, D_MODEL), lambda i: (i, 0)),
            const((A_WIDTH, D_MODEL)),
            const((B_WIDTH, D_MODEL)),
            const((D_MODEL, D_MODEL)),
            const((1, D_MODEL)),
        ],
        out_specs=pl.BlockSpec((tm, D_MODEL), lambda i: (i, 0)),
        out_shape=jax.ShapeDtypeStruct((m, D_MODEL), F32),
        compiler_params=pltpu.CompilerParams(
            dimension_semantics=("arbitrary",), vmem_limit_bytes=VMEM_LIMIT),
        name="post",
    )(za, zb, proj2, proj2, x2, woa, wob, wout, fg)


W_TN = 512
KR_COLS = _SIZES[6]


def _w_in_plan():
    off = dict(zip(("aq", "ak", "av", "ag", "cq", "ckv", "kr", "bg", "mg"),
                   [0] + _SPLITS))
    a_blk, b_blk, kind, scale = [], [], [], []
    for j in range(PROJ_COLS // W_TN):
        c0 = j * W_TN
        if c0 < COL_GA * LANE:
            src, k = off["aq"] + c0, 0
        elif c0 < COL_BG * LANE:
            src, k = off["mg"] + c0 - COL_GA * LANE, 1
        elif c0 < COL_CQ * LANE:
            src, k = off["bg"] + c0 - COL_BG * LANE, 1
        elif c0 < COL_KR * LANE:
            src, k = off["cq"] + c0 - COL_CQ * LANE, 0
        else:
            src, k = off["kr"], 2
        assert src % W_TN == (KR_COLS if k == 1 else 0)
        a_blk.append(src // W_TN)
        b_blk.append((src // W_TN + 1) * (W_TN // KR_COLS) if k == 1 else 0)
        kind.append(k)
        scale.append(A_HEAD_DIM ** -0.5 * LOG2E if c0 < COL_AK * LANE else 1.0)
    i32 = lambda v: jnp.asarray(v, jnp.int32)
    return i32(a_blk), i32(b_blk), i32(kind), jnp.asarray(scale, F32)


def _w_prep_kernel(ablk_ref, bblk_ref, kind_ref, scale_ref, a_ref, b_ref, o_ref):
    j = pl.program_id(0)
    kind = kind_ref[j]

    @pl.when(kind == 0)
    def _():
        o_ref[...] = (a_ref[...] * scale_ref[j]).T.astype(BF16)

    @pl.when(kind == 1)
    def _():
        t = jnp.concatenate([a_ref[KR_COLS:W_TN, :], b_ref[...]], axis=0)
        o_ref[...] = t.T.astype(BF16)

    @pl.when(kind == 2)
    def _():
        row = lax.broadcasted_iota(jnp.int32, (W_TN, 1), 0)
        o_ref[...] = jnp.where(row < KR_COLS, a_ref[...], 0.0).T.astype(BF16)


def _prep_w_in(w_t):
    d = w_t.shape[1]
    grid_spec = pltpu.PrefetchScalarGridSpec(
        num_scalar_prefetch=4,
        grid=(PROJ_COLS // W_TN,),
        in_specs=[
            pl.BlockSpec((W_TN, d), lambda j, a, b, k, s: (a[j], 0)),
            pl.BlockSpec((KR_COLS, d), lambda j, a, b, k, s: (b[j], 0)),
        ],
        out_specs=pl.BlockSpec((d, W_TN), lambda j, a, b, k, s: (0, j)),
    )
    return pl.pallas_call(
        _w_prep_kernel,
        grid_spec=grid_spec,
        out_shape=jax.ShapeDtypeStruct((d, PROJ_COLS), BF16),
        compiler_params=pltpu.CompilerParams(
            dimension_semantics=("arbitrary",), vmem_limit_bytes=VMEM_LIMIT),
        name="w_prep",
    )(*_w_in_plan(), w_t, w_t)


def _prep_w_uq_t(w):
    w = w.reshape(B_Q_LORA, B_HEADS, B_NOPE + B_ROPE) * ((B_NOPE + B_ROPE) ** -0.5 * LOG2E)
    w = jnp.pad(w, ((0, 0), (0, 0), (0, 2 * LANE - (B_NOPE + B_ROPE))))
    return w.reshape(B_Q_LORA, B_HEADS * 2 * LANE).T.astype(BF16)


def _prep_w_uv_t(w_ukv):
    w = w_ukv.reshape(B_KV_LORA, B_HEADS, B_NOPE + B_V_DIM)[:, :, B_NOPE:]
    w = jnp.pad(w, ((0, 0), (0, 0), (0, V_ROWS - B_V_DIM)))
    return w.reshape(B_KV_LORA, B_HEADS * V_ROWS).T.astype(BF16)


def _split3_bf16(c):
    c1 = c.astype(BF16).astype(F32)
    c2 = (c - c1).astype(BF16).astype(F32)
    c3 = (c - c1 - c2).astype(BF16).astype(F32)
    return c1, c2, c3


def _alibi_operands(positions, slopes, tq, tk):
    bsz, s = positions.shape
    nq, nk = s // tq, s // tk
    pmin = jnp.min(positions, axis=1, keepdims=True)
    rel = positions - pmin
    ok = ((jnp.min(rel, axis=1) >= 0) & (jnp.max(rel, axis=1) < 65536)
          & (jnp.max(jnp.abs(positions), axis=1) < (1 << 24)))
    qmin = jnp.min(positions.reshape(bsz, nq, tq), axis=-1)
    qmax = jnp.max(positions.reshape(bsz, nq, tq), axis=-1)
    kmin = jnp.min(positions.reshape(bsz, nk, tk), axis=-1)
    kmax = jnp.max(positions.reshape(bsz, nk, tk), axis=-1)
    after = qmin[:, :, None] >= kmax[:, None, :]
    before = qmax[:, :, None] <= kmin[:, None, :]
    mode = jnp.where(after, 1, jnp.where(before, -1, 0)) * ok[:, None, None].astype(jnp.int32)
    gap = jnp.where(after, qmin[:, :, None] - kmax[:, None, :], kmin[:, None, :] - qmax[:, :, None])
    dmin = jnp.where(mode != 0, gap, 0).astype(F32)
    off_diag = ~jnp.eye(nq, nk, dtype=bool)[None]
    fast = jnp.all((mode != 0) | ~off_diag, axis=-1) & (nq == nk) & (nk >= 4)

    rel = jnp.where(ok[:, None], rel, 0)
    hi = (rel >> 8).astype(F32)
    lo = (rel & 255).astype(F32)

    g = np.arange(LANE) % A_HEAD_DIM
    first, second = g < N_BIAS // 2, (g >= N_BIAS // 2) & (g < N_BIAS)
    even, odd = g % 2 == 0, g % 2 == 1
    f32 = lambda mask: jnp.asarray(mask, F32)
    q_hi, q_lo, q_one = f32(first & even), f32(first & odd), f32(second)
    k_hi, k_lo, k_one = f32(second & even), f32(second & odd), f32(first)
    hi_q, lo_q = hi.reshape(bsz, nq, 1, tq), lo.reshape(bsz, nq, 1, tq)
    col = lambda v: v[None, None, :, None]
    qb_t = (hi_q * col(q_hi) + lo_q * col(q_lo) + col(q_one)).astype(BF16)
    kb = (hi[:, :, None] * k_hi[None, None, :] + lo[:, :, None] * k_lo[None, None, :]
          + k_one[None, None, :]).astype(BF16)

    cb = slopes * LOG2E
    pieces = _split3_bf16(cb)
    cpair = jnp.stack([v for c in pieces for v in (256.0 * c, c)], axis=-1)
    sel = np.zeros((N_BIAS // 2, LANE), np.float32)
    for i in range(N_BIAS // 2):
        sel[i, (g % (N_BIAS // 2) == i) & (g < N_BIAS)] = 1.0
    spread = jnp.dot(cpair, jnp.asarray(sel), precision=lax.Precision.HIGHEST)
    cq = (spread * f32(second) + f32(first))[:, :, None]
    ck = (-spread * f32(first) + f32(second))[:, None, :]
    return (mode.reshape(-1).astype(jnp.int32), fast.reshape(-1).astype(jnp.int32), cb.astype(F32),
            jnp.pad(dmin, ((0, 0), (0, 0), (0, LANE - nk)))[:, :, None, :],
            qb_t, kb, cq, ck)


def kernel(x, positions, norm_g, w_in, lam_q1, lam_k1, lam_q2, lam_k2, a_subln_g, w_oa, q_norm_g,
           w_uq, kv_norm_g, w_ukv, w_ob, w_out, final_g):
    bsz, s, d = x.shape
    m = bsz * s
    assert norm_g.shape[0] == 1 and d == D_MODEL
    assert all(s % t == 0 for t in (TQ, TK, TQ_B, TK_B))
    layer = 0
    x2 = x.reshape(m, d)
    pos_f = positions.astype(F32)

    proj2 = _inproj(x2, norm_g[layer][None, :], _prep_w_in(w_in[layer].T))
    proj3 = proj2.reshape(bsz, s, PROJ_COLS)

    slopes = jnp.asarray([2.0 ** (-8.0 * (h + 1) / A_HEADS) for h in range(A_HEADS)], dtype=F32)
    mode, fast, cb, dmin, qb_t, kb, cq, ck = _alibi_operands(positions, slopes, TQ, TK)
    lamv = jnp.stack([lam_q1[layer], lam_k1[layer], lam_q2[layer], lam_k2[layer]]).astype(F32)
    za = _attn_a(mode, fast, cb, proj3, qb_t, kb, cq, ck, pos_f.reshape(bsz, s // TQ, 1, TQ),
                 pos_f[:, :, None], dmin, lamv, a_subln_g[layer][None, :], tq=TQ, tk=TK)

    half = B_ROPE // 2
    inv = ROPE_THETA ** (-jnp.arange(half, dtype=F32) / half)
    wuk = (w_ukv[layer].reshape(B_KV_LORA, B_HEADS, B_NOPE + B_V_DIM)[:, :, :B_NOPE]
           .reshape(B_KV_LORA, B_HEADS * B_NOPE).astype(BF16))
    vones = (jnp.arange(B_HEADS * V_ROWS) % V_ROWS == B_V_DIM).astype(F32)[:, None]
    qt3, kb2, vt3 = _mla_prep(proj2, pos_f.reshape(1, m), inv[:, None],
                              q_norm_g[layer][None, :], kv_norm_g[layer][None, :],
                              _prep_w_uq_t(w_uq[layer]), wuk, _prep_w_uv_t(w_ukv[layer]), vones,
                              tm=TQ_B, tkv=TK_B)
    zb = _attn_b(qt3, kb2.reshape(bsz, s, -1), vt3, proj3, tq=TQ_B, tk=TK_B)

    out = _post(za.reshape(m, A_WIDTH), zb.reshape(m, B_WIDTH), proj2, x2, w_oa[layer].astype(BF16),
                w_ob[layer].astype(BF16), w_out[layer].astype(BF16), final_g[None, :])
    return out.reshape(bsz, s, d)
```
